```python
import math
import jax, jax.numpy as jnp
from jax import lax
import numpy as np

D_MODEL = 2048
BATCH = 2
SEQ = 8192
DEPTH = 1
DEC_BATCH = 32
DEC_SEQ = 64
PAST_LEN = 1024

CHUNK = 64
C_CONV = 1024
CONV_W = 31
N_HEADS = 8
HEAD_K = 128
HEAD_V = 128
KEY_DIM = N_HEADS * HEAD_K
VAL_DIM = N_HEADS * HEAD_V
QKV_DIM = 2 * KEY_DIM + VAL_DIM
SHORT_W = 4
N_GROUPS = 4
EXP_PER_GROUP = 8
N_EXPERTS = N_GROUPS * EXP_PER_GROUP
TOP_K_IN_GROUP = 2
D_EXPERT = 256
ALPHA = (2.0 * DEPTH) ** 0.25
BETA_INIT = (8.0 * DEPTH) ** -0.25
LN_EPS = 1e-5
NORM_EPS = 1e-6
SECTION_SIZES = (C_CONV, C_CONV, QKV_DIM, N_HEADS, N_HEADS, VAL_DIM, D_MODEL, D_MODEL)
W_IN_COLS = 2 * C_CONV + QKV_DIM + 2 * N_HEADS + VAL_DIM + 2 * D_MODEL

kernel_name = 'hybrid_conformer_gdn_hmoe_stream_step'


def layer_norm(x, g, b):
    xf = x.astype(jnp.float32)
    mu = xf.mean(-1, keepdims=True)
    var = jnp.square(xf - mu).mean(-1, keepdims=True)
    return ((xf - mu) * lax.rsqrt(var + LN_EPS) * g + b).astype(x.dtype)


def rms_norm(x, g):
    xf = x.astype(jnp.float32)
    return xf * lax.rsqrt(jnp.mean(xf * xf, -1, keepdims=True) + NORM_EPS) * g


def l2norm(x):
    xf = x.astype(jnp.float32)
    return xf * lax.rsqrt(jnp.sum(xf * xf, -1, keepdims=True) + NORM_EPS)


def causal_depthwise(buf, u, w):
    full = jnp.concatenate([buf.astype(u.dtype), u], axis=1)
    y = lax.conv_general_dilated(full, w[:, None, :].astype(u.dtype), (1,), 'VALID',
                                 dimension_numbers=('NWC', 'WIO', 'NWC'),
                                 feature_group_count=u.shape[-1])
    return y, full[:, -buf.shape[1]:]


def gated_delta_rule(q, k, v, beta, g, s0, chunk):
    f32 = jnp.float32
    b_, t_, h_, _ = q.shape
    dv = v.shape[-1]
    n = t_ // chunk

    def blocks(a):
        a = a.astype(f32).reshape((b_, n, chunk, h_) + a.shape[3:])
        return jnp.moveaxis(a, 3, 2)

    q, k, v, beta, g = blocks(q), blocks(k), blocks(v), blocks(beta), blocks(g)
    gc = jnp.cumsum(g, axis=-1)
    idx = jnp.arange(chunk)
    incl = idx[:, None] >= idx[None, :]
    strict = idx[:, None] > idx[None, :]
    decay = jnp.exp(jnp.where(incl, gc[..., :, None] - gc[..., None, :], -jnp.inf))
    kb = k * beta[..., None]
    a_mat = jnp.where(strict, jnp.einsum('bnhck,bnhdk->bnhcd', kb, k) * decay, 0.0)
    m_mat = a_mat + jnp.eye(chunk, dtype=f32)
    rhs = jnp.concatenate([v * beta[..., None], kb * jnp.exp(gc)[..., None]], axis=-1)
    sol = lax.linalg.triangular_solve(m_mat, rhs, left_side=True, lower=True)
    u, w = sol[..., :dv], sol[..., dv:]
    qk = jnp.einsum('bnhck,bnhdk->bnhcd', q, k) * decay
    q_dec = q * jnp.exp(gc)[..., None]
    k_dec = k * jnp.exp(gc[..., -1:] - gc)[..., None]
    g_last = jnp.exp(gc[..., -1])

    def step(s, inp):
        u_i, w_i, qk_i, qd_i, kd_i, gl_i = inp
        v_new = u_i - jnp.einsum('bhck,bhkv->bhcv', w_i, s)
        o_i = jnp.einsum('bhck,bhkv->bhcv', qd_i, s) + jnp.einsum('bhcd,bhdv->bhcv', qk_i, v_new)
        s = s * gl_i[..., None, None] + jnp.einsum('bhck,bhcv->bhkv', kd_i, v_new)
        return s, o_i

    xs = (jnp.moveaxis(u, 1, 0), jnp.moveaxis(w, 1, 0), jnp.moveaxis(qk, 1, 0),
          jnp.moveaxis(q_dec, 1, 0), jnp.moveaxis(k_dec, 1, 0), jnp.moveaxis(g_last, 1, 0))
    s_fin, o = lax.scan(step, s0.astype(f32), xs)
    o = jnp.moveaxis(jnp.moveaxis(o, 0, 1), 2, 3).reshape(b_, t_, h_, dv)
    return o, s_fin


def mixer(x, conv_buf, short_buf, s0, w_in, b_in, w_dw, b_dw, lnc_g, lnc_b, w_conv_out,
          w_short, a_log, dt_bias, o_norm_g, w_o, w_out):
    b_, t_, _ = x.shape
    f32 = jnp.float32
    z = x @ w_in + b_in
    split_at = np.cumsum(SECTION_SIZES)[:-1].tolist()
    glu_a, glu_b, qkv, beta_l, a_l, g_out, gate_c, gate_d = jnp.split(z, split_at, axis=-1)
    u = glu_a * jax.nn.sigmoid(glu_b)
    c, new_conv = causal_depthwise(conv_buf, u, w_dw)
    c = jax.nn.silu(layer_norm(c + b_dw, lnc_g, lnc_b))
    conv_out = c @ w_conv_out
    qkv_c, new_short = causal_depthwise(short_buf, qkv, w_short)
    qkv_c = jax.nn.silu(qkv_c)
    q, k, v = jnp.split(qkv_c, [KEY_DIM, 2 * KEY_DIM], axis=-1)
    q = l2norm(q.reshape(b_, t_, N_HEADS, HEAD_K)) * (HEAD_K ** -0.5)
    k = l2norm(k.reshape(b_, t_, N_HEADS, HEAD_K))
    v = v.reshape(b_, t_, N_HEADS, HEAD_V)
    beta = jax.nn.sigmoid(beta_l.astype(f32))
    g = -jnp.exp(a_log.astype(f32)) * jax.nn.softplus(a_l.astype(f32) + dt_bias.astype(f32))
    chunk = CHUNK if t_ % CHUNK == 0 else t_
    o, s_new = gated_delta_rule(q, k, v, beta, g, s0, chunk)
    o = rms_norm(o, o_norm_g.astype(f32)) * jax.nn.silu(g_out.astype(f32).reshape(b_, t_, N_HEADS, HEAD_V))
    delta_out = o.reshape(b_, t_, VAL_DIM).astype(x.dtype) @ w_o
    merged = jax.nn.sigmoid(gate_c) * conv_out + jax.nn.sigmoid(gate_d) * delta_out
    return merged @ w_out, new_conv, new_short, s_new.astype(s0.dtype)


def hier_moe(h, w_rg, b_rg, w_re, b_re, w_gate, w_up, w_down):
    b_, t_, d = h.shape
    f32 = jnp.float32
    xf = h.reshape(-1, d)
    n = xf.shape[0]
    pg = jax.nn.softmax((xf @ w_rg).astype(f32) + b_rg.astype(f32), axis=-1)
    pg_top, g_idx = lax.top_k(pg, 1)
    le = ((xf @ w_re).astype(f32) + b_re.astype(f32)).reshape(n, N_GROUPS, EXP_PER_GROUP)
    le_sel = jnp.take_along_axis(le, g_idx[:, :, None], axis=1)[:, 0]
    pe_top, e_idx = lax.top_k(jax.nn.softmax(le_sel, axis=-1), TOP_K_IN_GROUP)
    pe_top = pe_top / jnp.sum(pe_top, -1, keepdims=True)
    gate_w = pg_top * pe_top
    expert_id = g_idx * EXP_PER_GROUP + e_idx
    comb = jnp.sum(jax.nn.one_hot(expert_id, N_EXPERTS, dtype=f32) * gate_w[..., None], axis=1)
    hg = jnp.einsum('nd,edf->nef', xf, w_gate)
    hu = jnp.einsum('nd,edf->nef', xf, w_up)
    act = jax.nn.silu(hg) * hu * comb[:, :, None].astype(xf.dtype)
    out = jnp.einsum('nef,efd->nd', act, w_down)
    return out.reshape(b_, t_, d)


def block(x, conv_buf, short_buf, s0, w_in, b_in, w_dw, b_dw, lnc_g, lnc_b, w_conv_out, w_short,
          a_log, dt_bias, o_norm_g, w_o, w_out, ln1_g, ln1_b, w_rg, b_rg, w_re, b_re,
          w_gate, w_up, w_down, ln2_g, ln2_b):
    mix, new_conv, new_short, new_s = mixer(x, conv_buf, short_buf, s0, w_in, b_in, w_dw, b_dw,
                                            lnc_g, lnc_b, w_conv_out, w_short, a_log, dt_bias,
                                            o_norm_g, w_o, w_out)
    h = layer_norm(ALPHA * x + mix, ln1_g, ln1_b)
    y = layer_norm(ALPHA * h + hier_moe(h, w_rg, b_rg, w_re, b_re, w_gate, w_up, w_down), ln2_g, ln2_b)
    return y, new_conv, new_short, new_s


def setup_inputs(seed: int = 0) -> dict:
    key = jax.random.key(seed)
    ks = jax.random.split(key, 32)
    f32 = jnp.float32
    L = DEPTH

    def nrm(k, shape, s):
        return jax.random.normal(k, shape, f32) * s

    dt = jnp.exp(jax.random.uniform(ks[13], (L, N_HEADS), f32, math.log(1e-3), math.log(1e-1)))
    return {
        'x_prompt': nrm(ks[0], (BATCH, SEQ, D_MODEL), 1.0),
        'x_sample': nrm(ks[1], (DEC_BATCH, DEC_SEQ, D_MODEL), 1.0),
        'cache_conv': nrm(ks[2], (L, DEC_BATCH, CONV_W - 1, C_CONV), 0.5),
        'cache_short': nrm(ks[3], (L, DEC_BATCH, SHORT_W - 1, QKV_DIM), 1.0),
        'state_delta': nrm(ks[4], (L, DEC_BATCH, N_HEADS, HEAD_K, HEAD_V), 0.05),
        'w_in': nrm(ks[5], (L, D_MODEL, W_IN_COLS), D_MODEL ** -0.5),
        'b_in': nrm(ks[6], (L, W_IN_COLS), 0.02),
        'w_dw': nrm(ks[7], (L, CONV_W, C_CONV), CONV_W ** -0.5),
        'b_dw': nrm(ks[8], (L, C_CONV), 0.02),
        'lnc_g': 1.0 + nrm(ks[9], (L, C_CONV), 0.02),
        'lnc_b': nrm(ks[10], (L, C_CONV), 0.02),
        'w_conv_out': nrm(ks[11], (L, C_CONV, D_MODEL), C_CONV ** -0.5),
        'w_short': nrm(ks[14], (L, SHORT_W, QKV_DIM), SHORT_W ** -0.5),
        'a_log': jnp.log(jax.random.uniform(ks[12], (L, N_HEADS), f32, 1.0, 16.0)),
        'dt_bias': jnp.log(jnp.expm1(dt)),
        'o_norm_g': 1.0 + nrm(ks[15], (L, HEAD_V), 0.02),
        'w_o': nrm(ks[16], (L, VAL_DIM, D_MODEL), VAL_DIM ** -0.5),
        'w_out': nrm(ks[17], (L, D_MODEL, D_MODEL), D_MODEL ** -0.5 * BETA_INIT),
        'ln1_g': 1.0 + nrm(ks[18], (L, D_MODEL), 0.02),
        'ln1_b': nrm(ks[19], (L, D_MODEL), 0.02),
        'w_rg': nrm(ks[20], (L, D_MODEL, N_GROUPS), D_MODEL ** -0.5),
        'b_rg': nrm(ks[21], (L, N_GROUPS), 0.01),
        'w_re': nrm(ks[22], (L, D_MODEL, N_EXPERTS), D_MODEL ** -0.5),
        'b_re': nrm(ks[23], (L, N_EXPERTS), 0.01),
        'w_gate': nrm(ks[24], (L, N_EXPERTS, D_MODEL, D_EXPERT), D_MODEL ** -0.5),
        'w_up': nrm(ks[25], (L, N_EXPERTS, D_MODEL, D_EXPERT), D_MODEL ** -0.5),
        'w_down': nrm(ks[26], (L, N_EXPERTS, D_EXPERT, D_MODEL), D_EXPERT ** -0.5 * BETA_INIT),
        'ln2_g': 1.0 + nrm(ks[27], (L, D_MODEL), 0.02),
        'ln2_b': nrm(ks[28], (L, D_MODEL), 0.02),
    }


def reference(x_prompt, x_sample, cache_conv, cache_short, state_delta, w_in, b_in, w_dw, b_dw,
              lnc_g, lnc_b, w_conv_out, w_short, a_log, dt_bias, o_norm_g, w_o, w_out, ln1_g, ln1_b,
              w_rg, b_rg, w_re, b_re, w_gate, w_up, w_down, ln2_g, ln2_b):
    weights = (w_in, b_in, w_dw, b_dw, lnc_g, lnc_b, w_conv_out, w_short, a_log, dt_bias, o_norm_g,
               w_o, w_out, ln1_g, ln1_b, w_rg, b_rg, w_re, b_re, w_gate, w_up, w_down, ln2_g, ln2_b)
    yp, ys = x_prompt, x_sample
    bp = x_prompt.shape[0]
    conv_p, short_p, delta_p, conv_s, short_s, delta_s = [], [], [], [], [], []
    for l in range(DEPTH):
        lw = tuple(wt[l] for wt in weights)
        zc = jnp.zeros((bp, CONV_W - 1, C_CONV), x_prompt.dtype)
        zs = jnp.zeros((bp, SHORT_W - 1, QKV_DIM), x_prompt.dtype)
        zd = jnp.zeros((bp, N_HEADS, HEAD_K, HEAD_V), state_delta.dtype)
        yp, c, s, d = block(yp, zc, zs, zd, *lw)
        conv_p.append(c)
        short_p.append(s)
        delta_p.append(d)
        ys, c, s, d = block(ys, cache_conv[l], cache_short[l], state_delta[l], *lw)
        conv_s.append(c)
        short_s.append(s)
        delta_s.append(d)
    return (yp, ys, jnp.stack(conv_p), jnp.stack(short_p), jnp.stack(delta_p),
            jnp.stack(conv_s), jnp.stack(short_s), jnp.stack(delta_s))
```

```python
import functools

import jax
import jax.numpy as jnp
from jax import lax
from jax.experimental import pallas as pl
from jax.experimental.pallas import tpu as pltpu

F32 = jnp.float32
BF16 = jnp.bfloat16

CHUNK = 64
N_HEADS = 8
HEAD_DIM = 128
N_GROUPS = 4
EXP_PER_GROUP = 8
N_EXPERTS = N_GROUPS * EXP_PER_GROUP
LN_EPS = 1e-5
NORM_EPS = 1e-6
LANES = 128
ROUTE_COL0 = N_GROUPS
VMEM_LIMIT = 56 * 1024 * 1024


def _dot(a, b):
    return jnp.dot(a, b, preferred_element_type=F32)


def _dot_nt(a, b):
    return lax.dot_general(a, b, (((1,), (1,)), ((), ())), preferred_element_type=F32)


def _dot_tn(a, b):
    return lax.dot_general(a, b, (((0,), (0,)), ((), ())), preferred_element_type=F32)


def _split3(x):
    hi = x.astype(BF16)
    r1 = x - hi.astype(F32)
    mid = r1.astype(BF16)
    lo = (r1 - mid.astype(F32)).astype(BF16)
    return hi, mid, lo


def _sigmoid(x):
    return 1.0 / (1.0 + jnp.exp(-x))


def _silu(x):
    return x * _sigmoid(x)


def _softplus(x):
    return jnp.maximum(x, 0.0) + jnp.log1p(jnp.exp(-jnp.abs(x)))


def _layer_norm(x, g, b):
    mu = jnp.mean(x, axis=-1, keepdims=True)
    xc = x - mu
    var = jnp.mean(xc * xc, axis=-1, keepdims=True)
    return xc * lax.rsqrt(var + LN_EPS) * g + b


def _clamp(v, lo, hi):
    return jnp.minimum(jnp.maximum(v, lo), hi)


_TN = 512
_J_GLU, _J_QKV, _J_BA, _J_GO, _J_GC, _J_GD, _J_END = 0, 4, 10, 11, 13, 17, 21


def _inproj_kernel(x_ref, wglu_ref, wqkv_ref, wba_ref, wbat_ref, wtail_ref,
                   bglu_ref, bqkv_ref, bba_ref, bbat_ref, btail_ref, prow_ref, pcol_ref,
                   u_ref, qkv_ref, bg_ref, bgt_ref, gsil_ref, sgc_ref, sgd_ref, xb_ref):
    j = pl.program_id(1)

    @pl.when(j == 0)
    def _():
        xb_ref[...] = x_ref[...].astype(BF16)

    @pl.when(j < _J_QKV)
    def _():
        z = _dot(xb_ref[...], wglu_ref[...]) + bglu_ref[...]
        half = _TN // 2
        u_ref[...] = z[:, :half] * _sigmoid(z[:, half:])

    @pl.when((j >= _J_QKV) & (j < _J_BA))
    def _():
        qkv_ref[...] = _dot(xb_ref[...], wqkv_ref[...]) + bqkv_ref[...]

    @pl.when(j == _J_BA)
    def _():
        xb = xb_ref[...]
        z = _dot(xb, wba_ref[...]) + bba_ref[...]
        col = lax.broadcasted_iota(jnp.int32, z.shape, 1)
        g = -jnp.exp(prow_ref[0:1, :]) * _softplus(z + prow_ref[1:2, :])
        bg_ref[...] = jnp.where(col < N_HEADS, _sigmoid(z), g)
        zt = _dot_nt(wbat_ref[...], xb) + bbat_ref[:, 0:1]
        row = lax.broadcasted_iota(jnp.int32, zt.shape, 0)
        gt = -jnp.exp(pcol_ref[:, 0:1]) * _softplus(zt + pcol_ref[:, 1:2])
        bgt_ref[...] = jnp.where(row < N_HEADS, _sigmoid(zt), gt)

    @pl.when((j >= _J_GO) & (j < _J_GC))
    def _():
        z = _dot(xb_ref[...], wtail_ref[...]) + btail_ref[...]
        gsil_ref[...] = _silu(z).astype(BF16)

    @pl.when((j >= _J_GC) & (j < _J_GD))
    def _():
        z = _dot(xb_ref[...], wtail_ref[...]) + btail_ref[...]
        sgc_ref[...] = _sigmoid(z).astype(BF16)

    @pl.when(j >= _J_GD)
    def _():
        z = _dot(xb_ref[...], wtail_ref[...]) + btail_ref[...]
        sgd_ref[...] = _sigmoid(z).astype(BF16)


def _inproj(x, pk, tm):
    n, d = x.shape
    c_conv = pk["w_glu"].shape[1] // 2
    qkv_dim = pk["w_qkv"].shape[1]
    val_dim = N_HEADS * HEAD_DIM
    assert n % tm == 0

    def cm(lo, hi):
        return lambda i, j: (0, _clamp(j - lo, 0, hi - lo - 1))

    def om(lo, hi):
        return lambda i, j: (i, _clamp(j - lo, 0, hi - lo - 1))

    in_specs = [
        pl.BlockSpec((tm, d), lambda i, j: (i, 0)),
        pl.BlockSpec((d, _TN), cm(_J_GLU, _J_QKV)),
        pl.BlockSpec((d, _TN), cm(_J_QKV, _J_BA)),
        pl.BlockSpec((d, LANES), lambda i, j: (0, 0)),
        pl.BlockSpec((2 * N_HEADS, d), lambda i, j: (0, 0)),
        pl.BlockSpec((d, _TN), cm(_J_GO, _J_END)),
        pl.BlockSpec((1, _TN), cm(_J_GLU, _J_QKV)),
        pl.BlockSpec((1, _TN), cm(_J_QKV, _J_BA)),
        pl.BlockSpec((1, LANES), lambda i, j: (0, 0)),
        pl.BlockSpec((2 * N_HEADS, LANES), lambda i, j: (0, 0)),
        pl.BlockSpec((1, _TN), cm(_J_GO, _J_END)),
        pl.BlockSpec((2, LANES), lambda i, j: (0, 0)),
        pl.BlockSpec((2 * N_HEADS, LANES), lambda i, j: (0, 0)),
    ]
    out_shape = [
        jax.ShapeDtypeStruct((n, c_conv), F32),
        jax.ShapeDtypeStruct((n, qkv_dim), F32),
        jax.ShapeDtypeStruct((n, LANES), F32),
        jax.ShapeDtypeStruct((2 * N_HEADS, n), F32),
        jax.ShapeDtypeStruct((n, val_dim), BF16),
        jax.ShapeDtypeStruct((n, d), BF16),
        jax.ShapeDtypeStruct((n, d), BF16),
    ]
    out_specs = [
        pl.BlockSpec((tm, _TN // 2), om(_J_GLU, _J_QKV)),
        pl.BlockSpec((tm, _TN), om(_J_QKV, _J_BA)),
        pl.BlockSpec((tm, LANES), lambda i, j: (i, 0)),
        pl.BlockSpec((2 * N_HEADS, tm), lambda i, j: (0, i)),
        pl.BlockSpec((tm, _TN), om(_J_GO, _J_GC)),
        pl.BlockSpec((tm, _TN), om(_J_GC, _J_GD)),
        pl.BlockSpec((tm, _TN), om(_J_GD, _J_END)),
    ]
    return pl.pallas_call(
        _inproj_kernel,
        grid=(n // tm, _J_END),
        in_specs=in_specs,
        out_specs=out_specs,
        out_shape=out_shape,
        scratch_shapes=[pltpu.VMEM((tm, d), BF16)],
        compiler_params=pltpu.CompilerParams(
            dimension_semantics=("arbitrary", "arbitrary"), vmem_limit_bytes=VMEM_LIMIT),
        name="inproj",
    )(x, pk["w_glu"], pk["w_qkv"], pk["w_ba"], pk["w_bat"], pk["w_tail"],
      pk["b_glu"], pk["b_qkv"], pk["b_ba"], pk["b_bat"], pk["b_tail"], pk["p_row"], pk["p_col"])


_HALO = 32
_CONV_RB = 64


def _conv_kernel(u_ref, cache_ref, wdw_ref, bdw_ref, lng_ref, lnb_ref, wco_ref, sgc_ref,
                 out_ref, buf_ref, cbuf_ref, cn_ref):
    t = pl.program_id(1)
    tt = u_ref.shape[1]
    c_conv = u_ref.shape[2]
    width = wdw_ref.shape[0]
    first = _HALO - (width - 1)

    @pl.when(t == 0)
    def _():
        buf_ref[0:_HALO, :] = cache_ref[0]

    buf_ref[_HALO:_HALO + tt, :] = u_ref[0]

    for rb in range(tt // _CONV_RB):
        r0 = rb * _CONV_RB
        for cb in range(c_conv // LANES):
            lanes = slice(cb * LANES, (cb + 1) * LANES)
            acc = jnp.zeros((_CONV_RB, LANES), F32)
            for r in range(8):
                s = buf_ref[r0 + r:r0 + r + _CONV_RB + _HALO, lanes]
                for a in range(_HALO // 8 + 1):
                    k = 8 * a + r - first
                    if 0 <= k < width:
                        acc = acc + wdw_ref[k:k + 1, lanes] * s[8 * a:8 * a + _CONV_RB]
            cbuf_ref[:, lanes] = acc
        y = _layer_norm(cbuf_ref[...] + bdw_ref[...], lng_ref[...], lnb_ref[...])
        cn_ref[r0:r0 + _CONV_RB, :] = _silu(y).astype(BF16)

    co = _dot(cn_ref[...], wco_ref[...])
    out_ref[0] = (co * sgc_ref[0].astype(F32)).astype(BF16)
    buf_ref[0:_HALO, :] = buf_ref[tt:tt + _HALO, :]


def _conv_branch(u, cache, sgc, p, tt):
    b, t, c_conv = u.shape
    d = sgc.shape[-1]
    width = p["w_dw"].shape[0]
    assert t % tt == 0 and tt % _CONV_RB == 0 and tt >= _HALO and width - 1 <= _HALO
    cache_p = jnp.pad(cache.astype(F32), ((0, 0), (_HALO - (width - 1), 0), (0, 0)))
    full2 = lambda shape: pl.BlockSpec(shape, lambda i, j: (0, 0))
    return pl.pallas_call(
        _conv_kernel,
        grid=(b, t // tt),
        in_specs=[
            pl.BlockSpec((1, tt, c_conv), lambda i, j: (i, j, 0)),
            pl.BlockSpec((1, _HALO, c_conv), lambda i, j: (i, 0, 0)),
            full2((width, c_conv)), full2((1, c_conv)), full2((1, c_conv)), full2((1, c_conv)),
            full2((c_conv, d)),
            pl.BlockSpec((1, tt, d), lambda i, j: (i, j, 0)),
        ],
        out_specs=pl.BlockSpec((1, tt, d), lambda i, j: (i, j, 0)),
        out_shape=jax.ShapeDtypeStruct((b, t, d), BF16),
        scratch_shapes=[pltpu.VMEM((tt + _HALO, c_conv), F32),
                        pltpu.VMEM((_CONV_RB, c_conv), F32),
                        pltpu.VMEM((tt, c_conv), BF16)],
        compiler_params=pltpu.CompilerParams(
            dimension_semantics=("arbitrary", "arbitrary"), vmem_limit_bytes=VMEM_LIMIT),
        name="conv_branch",
    )(u, cache_p, p["w_dw"], p["b_dw"], p["lnc_g"], p["lnc_b"], p["w_conv_out"], sgc)


_SHORT_PAD = 8


def _unit_lower_inverse(a, masks):
    eye, bd8, lvl = masks
    bf = lambda m: m.astype(BF16)
    ad = jnp.where(bd8, a, 0.0)
    a2 = _dot(bf(ad), bf(ad))
    a3 = _dot(bf(ad), bf(a2))
    a4 = _dot(bf(a2), bf(a2))
    t = eye - ad + a2 - a3
    t = t + _dot(bf(t), bf(a4))
    for m in lvl:
        off = jnp.where(m, a, 0.0)
        x = _dot(bf(off), bf(t))
        t = t - _dot(bf(t), bf(x))
    return t


def _delta_kernel(qkv_ref, cache_ref, wsh_ref, bg_ref, bgt_ref, gsil_ref, ong_ref, s0_ref,
                  o_ref, sfin_ref, xb_ref, s_ref):
    c = pl.program_id(1)
    nc = pl.num_programs(1)
    ck = CHUNK
    key_dim = N_HEADS * HEAD_DIM
    sw = wsh_ref.shape[0]

    @pl.when(c == 0)
    def _():
        s_ref[...] = s0_ref[0]
        xb_ref[0:_SHORT_PAD, :] = cache_ref[0]

    xb_ref[_SHORT_PAD:_SHORT_PAD + ck, :] = qkv_ref[0]

    def conv_cols(lo):
        lanes = slice(lo, lo + HEAD_DIM)
        acc = wsh_ref[sw - 1:sw, lanes] * xb_ref[_SHORT_PAD:_SHORT_PAD + ck, lanes]
        for k in range(sw - 1):
            r = _SHORT_PAD - (sw - 1) + k
            acc = acc + wsh_ref[k:k + 1, lanes] * xb_ref[r:r + ck, lanes]
        return _silu(acc)

    ri = lax.broadcasted_iota(jnp.int32, (ck, ck), 0)
    ci = lax.broadcasted_iota(jnp.int32, (ck, ck), 1)
    incl = ri >= ci
    strict = ri > ci
    eye = jnp.where(ri == ci, 1.0, 0.0).astype(F32)
    bd8 = (ri // 8) == (ci // 8)
    lvl = [((ri // (2 * s)) == (ci // (2 * s))) & ((ri // s) != (ci // s)) for s in (8, 16, 32)]
    masks = (eye, bd8, lvl)
    tri_l = jnp.where(incl, 1.0, 0.0).astype(BF16)
    tri_u = jnp.where(ri <= ci, 1.0, 0.0).astype(BF16)

    bg = bg_ref[0]
    bgt = bgt_ref[0]
    gc_cols = sum(_dot(tri_l, part) for part in _split3(bg))
    gc_rows = sum(_dot(part, tri_u) for part in _split3(bgt))

    for h in range(N_HEADS):
        q = conv_cols(h * HEAD_DIM)
        k = conv_cols(key_dim + h * HEAD_DIM)
        v = conv_cols(2 * key_dim + h * HEAD_DIM)
        q = q * lax.rsqrt(jnp.sum(q * q, axis=-1, keepdims=True) + NORM_EPS) * (HEAD_DIM ** -0.5)
        k = k * lax.rsqrt(jnp.sum(k * k, axis=-1, keepdims=True) + NORM_EPS)
        beta = bg[:, h:h + 1]
        gcc = gc_cols[:, N_HEADS + h:N_HEADS + h + 1]
        gcr = gc_rows[N_HEADS + h:N_HEADS + h + 1, :]
        decay = jnp.exp(jnp.where(incl, gcc - gcr, -jnp.inf))
        kb = k * beta
        kbf = k.astype(BF16)
        a = jnp.where(strict, _dot_nt(kb.astype(BF16), kbf) * decay, 0.0)
        tinv = _unit_lower_inverse(a, masks)
        egc = jnp.exp(gcc)
        rhs = jnp.concatenate([v * beta, kb * egc], axis=1)
        sol = _dot(tinv.astype(BF16), rhs.astype(BF16))
        u, w = sol[:, :HEAD_DIM], sol[:, HEAD_DIM:]
        qk = _dot_nt(q.astype(BF16), kbf) * decay
        g_last = gcc[ck - 1:ck, :]
        q_dec = q * egc
        k_dec = k * jnp.exp(g_last - gcc)

        s = s_ref[h]
        sb = s.astype(BF16)
        v_new = u - _dot(w.astype(BF16), sb)
        vb = v_new.astype(BF16)
        o = _dot(q_dec.astype(BF16), sb) + _dot(qk.astype(BF16), vb)
        s_ref[h] = s * jnp.exp(g_last) + _dot_tn(k_dec.astype(BF16), vb)

        o = o * lax.rsqrt(jnp.mean(o * o, axis=-1, keepdims=True) + NORM_EPS) * ong_ref[...]
        lanes = slice(h * HEAD_DIM, (h + 1) * HEAD_DIM)
        o_ref[0, :, lanes] = (o * gsil_ref[0, :, lanes].astype(F32)).astype(BF16)

    xb_ref[0:_SHORT_PAD, :] = xb_ref[ck:ck + _SHORT_PAD, :]

    @pl.when(c == nc - 1)
    def _():
        sfin_ref[0] = s_ref[...]


def _delta_branch(qkv, cache, bg, bgt, gsil, s0, p):
    b, t, qkv_dim = qkv.shape
    assert t % CHUNK == 0
    nc = t // CHUNK
    sw = p["w_short"].shape[0]
    cache_p = jnp.pad(cache.astype(F32), ((0, 0), (_SHORT_PAD - (sw - 1), 0), (0, 0)))
    val_dim = N_HEADS * HEAD_DIM
    bgt3 = bgt.reshape(2 * N_HEADS, b * nc, CHUNK).transpose(1, 0, 2)
    return pl.pallas_call(
        _delta_kernel,
        grid=(b, nc),
        in_specs=[
            pl.BlockSpec((1, CHUNK, qkv_dim), lambda i, j: (i, j, 0)),
            pl.BlockSpec((1, _SHORT_PAD, qkv_dim), lambda i, j: (i, 0, 0)),
            pl.BlockSpec((sw, qkv_dim), lambda i, j: (0, 0)),
            pl.BlockSpec((1, CHUNK, LANES), lambda i, j: (i, j, 0)),
            pl.BlockSpec((1, 2 * N_HEADS, CHUNK), lambda i, j: (i * nc + j, 0, 0)),
            pl.BlockSpec((1, CHUNK, val_dim), lambda i, j: (i, j, 0)),
            pl.BlockSpec((1, HEAD_DIM), lambda i, j: (0, 0)),
            pl.BlockSpec((1, N_HEADS, HEAD_DIM, HEAD_DIM), lambda i, j: (i, 0, 0, 0)),
        ],
        out_specs=[
            pl.BlockSpec((1, CHUNK, val_dim), lambda i, j: (i, j, 0)),
            pl.BlockSpec((1, N_HEADS, HEAD_DIM, HEAD_DIM), lambda i, j: (i, 0, 0, 0)),
        ],
        out_shape=[
            jax.ShapeDtypeStruct((b, t, val_dim), BF16),
            jax.ShapeDtypeStruct((b, N_HEADS, HEAD_DIM, HEAD_DIM), F32),
        ],
        scratch_shapes=[pltpu.VMEM((_SHORT_PAD + CHUNK, qkv_dim), F32),
                        pltpu.VMEM((N_HEADS, HEAD_DIM, HEAD_DIM), F32)],
        compiler_params=pltpu.CompilerParams(
            dimension_semantics=("arbitrary", "arbitrary"), vmem_limit_bytes=VMEM_LIMIT),
        name="delta_rule",
    )(qkv, cache_p, p["w_short"], bg.reshape(b, t, LANES), bgt3, gsil, p["o_norm_g"], s0.astype(F32))


def _merge_kernel(alpha, convg_ref, og_ref, sgd_ref, x_ref, wo_ref, wout_ref, g_ref, b_ref,
                  wr_ref, br_ref, h_ref, hb_ref, comb_ref):
    d_out = _dot(og_ref[...], wo_ref[...])
    merged = convg_ref[...].astype(F32) + d_out * sgd_ref[...].astype(F32)
    mix = _dot(merged.astype(BF16), wout_ref[...])
    h = _layer_norm(alpha * x_ref[...] + mix, g_ref[...], b_ref[...])
    h_ref[...] = h
    hb_ref[...] = h.astype(BF16)

    h_hi, h_mid, _ = _split3(h)
    w_hi, w_mid, _ = _split3(wr_ref[...])
    logits = _dot(h_hi, w_hi) + _dot(h_mid, w_hi) + _dot(h_hi, w_mid) + br_ref[...]
    col = lax.broadcasted_iota(jnp.int32, logits.shape, 1).astype(F32)
    big = float(LANES)
    is_g = col < N_GROUPS
    mg = jnp.max(jnp.where(is_g, logits, -jnp.inf), axis=-1, keepdims=True)
    sg = jnp.sum(jnp.where(is_g, jnp.exp(jnp.where(is_g, logits, mg) - mg), 0.0), axis=-1, keepdims=True)
    pg_top = 1.0 / sg
    gidx = jnp.min(jnp.where(is_g & (logits == mg), col, big), axis=-1, keepdims=True)
    lo = ROUTE_COL0 + EXP_PER_GROUP * gidx
    sel = (col >= lo) & (col < lo + EXP_PER_GROUP)
    le = jnp.where(sel, logits, -jnp.inf)
    m1 = jnp.max(le, axis=-1, keepdims=True)
    i1 = jnp.min(jnp.where(le == m1, col, big), axis=-1, keepdims=True)
    le2 = jnp.where(col == i1, -jnp.inf, le)
    m2 = jnp.max(le2, axis=-1, keepdims=True)
    i2 = jnp.min(jnp.where(le2 == m2, col, big), axis=-1, keepdims=True)
    e2 = jnp.exp(m2 - m1)
    den = 1.0 + e2
    w1 = pg_top / den
    w2 = pg_top * e2 / den
    comb_ref[...] = jnp.where(col == i1, w1, 0.0) + jnp.where(col == i2, w2, 0.0)


def _merge(convg, og, sgd, x, p, alpha, tm):
    n, d = x.shape
    val_dim = og.shape[1]
    assert n % tm == 0
    row = lambda w: pl.BlockSpec((tm, w), lambda i: (i, 0))
    full = lambda shape: pl.BlockSpec(shape, lambda i: (0, 0))
    return pl.pallas_call(
        functools.partial(_merge_kernel, alpha),
        grid=(n // tm,),
        in_specs=[row(d), row(val_dim), row(d), row(d), full((val_dim, d)), full((d, d)),
                  full((1, d)), full((1, d)), full((d, LANES)), full((1, LANES))],
        out_specs=[row(d), row(d), row(LANES)],
        out_shape=[jax.ShapeDtypeStruct((n, d), F32), jax.ShapeDtypeStruct((n, d), BF16),
                   jax.ShapeDtypeStruct((n, LANES), F32)],
        compiler_params=pltpu.CompilerParams(
            dimension_semantics=("arbitrary",), vmem_limit_bytes=VMEM_LIMIT),
        name="merge_outproj",
    )(convg, og, sgd, x, p["w_o"], p["w_out"], p["ln1_g"], p["ln1_b"], p["w_router"], p["b_router"])


def _moe_kernel(alpha, h_ref, hb_ref, comb_ref, wg_ref, wu_ref, wd_ref, g_ref, b_ref, y_ref, acc_ref):
    e = pl.program_id(1)
    ne = pl.num_programs(1)

    @pl.when(e == 0)
    def _():
        acc_ref[...] = jnp.zeros_like(acc_ref)

    hb = hb_ref[...]
    hg = _dot(hb, wg_ref[0])
    hu = _dot(hb, wu_ref[0])
    comb = comb_ref[...]
    col = lax.broadcasted_iota(jnp.int32, comb.shape, 1)
    ce = jnp.sum(jnp.where(col == ROUTE_COL0 + e, comb, 0.0), axis=-1, keepdims=True)
    act = _silu(hg) * hu * ce
    acc_ref[...] += _dot(act.astype(BF16), wd_ref[0])

    @pl.when(e == ne - 1)
    def _():
        y_ref[...] = _layer_norm(alpha * h_ref[...] + acc_ref[...], g_ref[...], b_ref[...])


def _moe(h, hb, comb, p, alpha, tm):
    n, d = h.shape
    ne, _, f = p["w_gate"].shape
    assert n % tm == 0
    return pl.pallas_call(
        functools.partial(_moe_kernel, alpha),
        grid=(n // tm, ne),
        in_specs=[
            pl.BlockSpec((tm, d), lambda i, e: (i, 0)),
            pl.BlockSpec((tm, d), lambda i, e: (i, 0)),
            pl.BlockSpec((tm, LANES), lambda i, e: (i, 0)),
            pl.BlockSpec((1, d, f), lambda i, e: (e, 0, 0)),
            pl.BlockSpec((1, d, f), lambda i, e: (e, 0, 0)),
            pl.BlockSpec((1, f, d), lambda i, e: (e, 0, 0)),
            pl.BlockSpec((1, d), lambda i, e: (0, 0)),
            pl.BlockSpec((1, d), lambda i, e: (0, 0)),
        ],
        out_specs=pl.BlockSpec((tm, d), lambda i, e: (i, 0)),
        out_shape=jax.ShapeDtypeStruct((n, d), F32),
        scratch_shapes=[pltpu.VMEM((tm, d), F32)],
        compiler_params=pltpu.CompilerParams(
            dimension_semantics=("arbitrary", "arbitrary"), vmem_limit_bytes=VMEM_LIMIT),
        name="moe_experts",
    )(h, hb, comb, p["w_gate"], p["w_up"], p["w_down"], p["ln2_g"], p["ln2_b"])


def _pack_layer(w_in, b_in, w_dw, b_dw, lnc_g, lnc_b, w_conv_out, w_short, a_log, dt_bias, o_norm_g,
                w_o, w_out, ln1_g, ln1_b, w_rg, b_rg, w_re, b_re, w_gate, w_up, w_down, ln2_g, ln2_b):
    d = w_in.shape[0]
    c_conv = w_dw.shape[1]
    qkv_dim = w_short.shape[1]
    o_qkv = 2 * c_conv
    o_ba = o_qkv + qkv_dim
    o_tail = o_ba + 2 * N_HEADS
    half = _TN // 2

    def interleave(a, b):
        r = a.shape[0]
        a = a.reshape(r, -1, 1, half)
        b = b.reshape(r, -1, 1, half)
        return jnp.concatenate([a, b], axis=2).reshape(r, -1)

    def pad_cols(a, width):
        return jnp.pad(a, ((0, 0), (0, width - a.shape[1])))

    b2 = b_in[None, :].astype(F32)
    nh = N_HEADS
    zeros_h = jnp.zeros((nh,), F32)
    head_params = jnp.stack([jnp.concatenate([zeros_h, a_log.astype(F32)]),
                             jnp.concatenate([zeros_h, dt_bias.astype(F32)])])
    w_ba = w_in[:, o_ba:o_tail]
    w_router = jnp.concatenate([w_rg, w_re], axis=1).astype(F32)
    b_router = jnp.concatenate([b_rg, b_re])[None, :].astype(F32)
    return dict(
        w_glu=interleave(w_in[:, :c_conv], w_in[:, c_conv:o_qkv]).astype(BF16),
        b_glu=interleave(b2[:, :c_conv], b2[:, c_conv:o_qkv]),
        w_qkv=w_in[:, o_qkv:o_ba].astype(BF16),
        b_qkv=b2[:, o_qkv:o_ba],
        w_ba=pad_cols(w_ba, LANES).astype(BF16),
        b_ba=pad_cols(b2[:, o_ba:o_tail], LANES),
        w_bat=w_ba.T.astype(BF16),
        b_bat=jnp.broadcast_to(b_in[o_ba:o_tail, None].astype(F32), (2 * nh, LANES)),
        w_tail=w_in[:, o_tail:].astype(BF16),
        b_tail=b2[:, o_tail:],
        p_row=pad_cols(head_params, LANES),
        p_col=pad_cols(head_params.T, LANES),
        w_dw=w_dw.astype(F32), b_dw=b_dw[None, :].astype(F32),
        lnc_g=lnc_g[None, :].astype(F32), lnc_b=lnc_b[None, :].astype(F32),
        w_conv_out=w_conv_out.astype(BF16),
        w_short=w_short.astype(F32),
        o_norm_g=o_norm_g[None, :].astype(F32),
        w_o=w_o.astype(BF16), w_out=w_out.astype(BF16),
        ln1_g=ln1_g[None, :].astype(F32), ln1_b=ln1_b[None, :].astype(F32),
        w_router=pad_cols(w_router, LANES), b_router=pad_cols(b_router, LANES),
        w_gate=w_gate.astype(BF16), w_up=w_up.astype(BF16), w_down=w_down.astype(BF16),
        ln2_g=ln2_g[None, :].astype(F32), ln2_b=ln2_b[None, :].astype(F32),
    )


def _block(x, conv_buf, short_buf, s0, p, alpha, tm, conv_tt):
    b, t, d = x.shape
    n = b * t
    x2 = x.reshape(n, d).astype(F32)
    u, qkv, bg, bgt, gsil, sgc, sgd = _inproj(x2, p, tm)
    c_conv = u.shape[1]
    u3 = u.reshape(b, t, c_conv)
    qkv3 = qkv.reshape(b, t, -1)
    convg = _conv_branch(u3, conv_buf, sgc.reshape(b, t, d), p, conv_tt)
    og, s_new = _delta_branch(qkv3, short_buf, bg, bgt, gsil.reshape(b, t, -1), s0, p)
    h, hb, comb = _merge(convg.reshape(n, d), og.reshape(n, -1), sgd, x2, p, alpha, min(tm, 256))
    y = _moe(h, hb, comb, p, alpha, min(tm, 512))
    kc = conv_buf.shape[1]
    ks = short_buf.shape[1]
    assert t >= kc and t >= ks
    return (y.reshape(b, t, d).astype(x.dtype), u3[:, t - kc:].astype(x.dtype),
            qkv3[:, t - ks:].astype(x.dtype), s_new.astype(s0.dtype))


def kernel(x_prompt, x_sample, cache_conv, cache_short, state_delta, w_in, b_in, w_dw, b_dw, lnc_g, lnc_b, w_conv_out, w_short, a_log, dt_bias, o_norm_g, w_o, w_out, ln1_g, ln1_b, w_rg, b_rg, w_re, b_re, w_gate, w_up, w_down, ln2_g, ln2_b):
    weights = (w_in, b_in, w_dw, b_dw, lnc_g, lnc_b, w_conv_out, w_short, a_log, dt_bias, o_norm_g,
               w_o, w_out, ln1_g, ln1_b, w_rg, b_rg, w_re, b_re, w_gate, w_up, w_down, ln2_g, ln2_b)
    depth = w_in.shape[0]
    alpha = (2.0 * depth) ** 0.25
    yp, ys = x_prompt, x_sample
    bp = x_prompt.shape[0]
    outs = [[] for _ in range(6)]
    for l in range(depth):
        p = _pack_layer(*(wt[l] for wt in weights))
        zc = jnp.zeros((bp,) + cache_conv.shape[2:], x_prompt.dtype)
        zs = jnp.zeros((bp,) + cache_short.shape[2:], x_prompt.dtype)
        zd = jnp.zeros((bp,) + state_delta.shape[2:], state_delta.dtype)
        yp, c, s, dl = _block(yp, zc, zs, zd, p, alpha, 1024, 256)
        outs[0].append(c), outs[1].append(s), outs[2].append(dl)
        ys, c, s, dl = _block(ys, cache_conv[l], cache_short[l], state_delta[l], p, alpha, 1024, 64)
        outs[3].append(c), outs[4].append(s), outs[5].append(dl)
    return (yp, ys) + tuple(jnp.stack(o) for o in outs)
```

```python
import functools

import jax
import jax.numpy as jnp
from jax import lax
from jax.experimental import pallas as pl
from jax.experimental.pallas import tpu as pltpu

F32 = jnp.float32
BF16 = jnp.bfloat16

CHUNK = 64
N_HEADS = 8
HEAD_DIM = 128
N_GROUPS = 4
EXP_PER_GROUP = 8
N_EXPERTS = N_GROUPS * EXP_PER_GROUP
LN_EPS = 1e-5
NORM_EPS = 1e-6
LANES = 128
ROUTE_COL0 = N_GROUPS
VMEM_LIMIT = 56 * 1024 * 1024


def _dot(a, b):
    return jnp.dot(a, b, preferred_element_type=F32)


def _dot_nt(a, b):
    return lax.dot_general(a, b, (((1,), (1,)), ((), ())), preferred_element_type=F32)


def _dot_tn(a, b):
    return lax.dot_general(a, b, (((0,), (0,)), ((), ())), preferred_element_type=F32)


def _split3(x):
    hi = x.astype(BF16)
    r1 = x - hi.astype(F32)
    mid = r1.astype(BF16)
    lo = (r1 - mid.astype(F32)).astype(BF16)
    return hi, mid, lo


def _sigmoid(x):
    return 1.0 / (1.0 + jnp.exp(-x))


def _silu(x):
    return x * _sigmoid(x)


def _softplus(x):
    return jnp.maximum(x, 0.0) + jnp.log1p(jnp.exp(-jnp.abs(x)))


def _layer_norm(x, g, b):
    mu = jnp.mean(x, axis=-1, keepdims=True)
    xc = x - mu
    var = jnp.mean(xc * xc, axis=-1, keepdims=True)
    return xc * lax.rsqrt(var + LN_EPS) * g + b


def _clamp(v, lo, hi):
    return jnp.minimum(jnp.maximum(v, lo), hi)


_TN = 512
_J_GLU, _J_QKV, _J_BA, _J_GO, _J_GC, _J_GD, _J_END = 0, 4, 10, 11, 13, 17, 21


def _inproj_kernel(x_ref, wglu_ref, wqkv_ref, wba_ref, wbat_ref, wtail_ref,
                   bglu_ref, bqkv_ref, bba_ref, bbat_ref, btail_ref, prow_ref, pcol_ref,
                   u_ref, qkv_ref, bg_ref, bgt_ref, gsil_ref, sgc_ref, sgd_ref, xb_ref):
    j = pl.program_id(1)

    @pl.when(j == 0)
    def _():
        xb_ref[...] = x_ref[...].astype(BF16)

    @pl.when(j < _J_QKV)
    def _():
        z = _dot(xb_ref[...], wglu_ref[...]) + bglu_ref[...]
        half = _TN // 2
        u_ref[...] = z[:, :half] * _sigmoid(z[:, half:])

    @pl.when((j >= _J_QKV) & (j < _J_BA))
    def _():
        qkv_ref[...] = _dot(xb_ref[...], wqkv_ref[...]) + bqkv_ref[...]

    @pl.when(j == _J_BA)
    def _():
        xb = xb_ref[...]
        z = _dot(xb, wba_ref[...]) + bba_ref[...]
        col = lax.broadcasted_iota(jnp.int32, z.shape, 1)
        g = -jnp.exp(prow_ref[0:1, :]) * _softplus(z + prow_ref[1:2, :])
        bg_ref[...] = jnp.where(col < N_HEADS, _sigmoid(z), g)
        zt = _dot_nt(wbat_ref[...], xb) + bbat_ref[:, 0:1]
        row = lax.broadcasted_iota(jnp.int32, zt.shape, 0)
        gt = -jnp.exp(pcol_ref[:, 0:1]) * _softplus(zt + pcol_ref[:, 1:2])
        bgt_ref[...] = jnp.where(row < N_HEADS, _sigmoid(zt), gt)

    @pl.when((j >= _J_GO) & (j < _J_GC))
    def _():
        z = _dot(xb_ref[...], wtail_ref[...]) + btail_ref[...]
        gsil_ref[...] = _silu(z).astype(BF16)

    @pl.when((j >= _J_GC) & (j < _J_GD))
    def _():
        z = _dot(xb_ref[...], wtail_ref[...]) + btail_ref[...]
        sgc_ref[...] = _sigmoid(z).astype(BF16)

    @pl.when(j >= _J_GD)
    def _():
        z = _dot(xb_ref[...], wtail_ref[...]) + btail_ref[...]
        sgd_ref[...] = _sigmoid(z).astype(BF16)


def _inproj(x, pk, tm):
    n, d = x.shape
    c_conv = pk["w_glu"].shape[1] // 2
    qkv_dim = pk["w_qkv"].shape[1]
    val_dim = N_HEADS * HEAD_DIM
    assert n % tm == 0

    def cm(lo, hi):
        return lambda i, j: (0, _clamp(j - lo, 0, hi - lo - 1))

    def om(lo, hi):
        return lambda i, j: (i, _clamp(j - lo, 0, hi - lo - 1))

    in_specs = [
        pl.BlockSpec((tm, d), lambda i, j: (i, 0)),
        pl.BlockSpec((d, _TN), cm(_J_GLU, _J_QKV)),
        pl.BlockSpec((d, _TN), cm(_J_QKV, _J_BA)),
        pl.BlockSpec((d, LANES), lambda i, j: (0, 0)),
        pl.BlockSpec((2 * N_HEADS, d), lambda i, j: (0, 0)),
        pl.BlockSpec((d, _TN), cm(_J_GO, _J_END)),
        pl.BlockSpec((1, _TN), cm(_J_GLU, _J_QKV)),
        pl.BlockSpec((1, _TN), cm(_J_QKV, _J_BA)),
        pl.BlockSpec((1, LANES), lambda i, j: (0, 0)),
        pl.BlockSpec((2 * N_HEADS, LANES), lambda i, j: (0, 0)),
        pl.BlockSpec((1, _TN), cm(_J_GO, _J_END)),
        pl.BlockSpec((2, LANES), lambda i, j: (0, 0)),
        pl.BlockSpec((2 * N_HEADS, LANES), lambda i, j: (0, 0)),
    ]
    out_shape = [
        jax.ShapeDtypeStruct((n, c_conv), F32),
        jax.ShapeDtypeStruct((n, qkv_dim), F32),
        jax.ShapeDtypeStruct((n, LANES), F32),
        jax.ShapeDtypeStruct((2 * N_HEADS, n), F32),
        jax.ShapeDtypeStruct((n, val_dim), BF16),
        jax.ShapeDtypeStruct((n, d), BF16),
        jax.ShapeDtypeStruct((n, d), BF16),
    ]
    out_specs = [
        pl.BlockSpec((tm, _TN // 2), om(_J_GLU, _J_QKV)),
        pl.BlockSpec((tm, _TN), om(_J_QKV, _J_BA)),
        pl.BlockSpec((tm, LANES), lambda i, j: (i, 0)),
        pl.BlockSpec((2 * N_HEADS, tm), lambda i, j: (0, i)),
        pl.BlockSpec((tm, _TN), om(_J_GO, _J_GC)),
        pl.BlockSpec((tm, _TN), om(_J_GC, _J_GD)),
        pl.BlockSpec((tm, _TN), om(_J_GD, _J_END)),
    ]
    return pl.pallas_call(
        _inproj_kernel,
        grid=(n // tm, _J_END),
        in_specs=in_specs,
        out_specs=out_specs,
        out_shape=out_shape,
        scratch_shapes=[pltpu.VMEM((tm, d), BF16)],
        compiler_params=pltpu.CompilerParams(
            dimension_semantics=("arbitrary", "arbitrary"), vmem_limit_bytes=VMEM_LIMIT),
        name="inproj",
    )(x, pk["w_glu"], pk["w_qkv"], pk["w_ba"], pk["w_bat"], pk["w_tail"],
      pk["b_glu"], pk["b_qkv"], pk["b_ba"], pk["b_bat"], pk["b_tail"], pk["p_row"], pk["p_col"])


_HALO = 32
_CONV_RB = 64


def _conv_kernel(u_ref, cache_ref, wdw_ref, bdw_ref, lng_ref, lnb_ref, wco_ref, sgc_ref,
                 out_ref, buf_ref, cbuf_ref, cn_ref):
    t = pl.program_id(1)
    tt = u_ref.shape[1]
    c_conv = u_ref.shape[2]
    width = wdw_ref.shape[0]
    first = _HALO - (width - 1)

    @pl.when(t == 0)
    def _():
        buf_ref[0:_HALO, :] = cache_ref[0]

    buf_ref[_HALO:_HALO + tt, :] = u_ref[0]

    for rb in range(tt // _CONV_RB):
        r0 = rb * _CONV_RB
        for cb in range(c_conv // LANES):
            lanes = slice(cb * LANES, (cb + 1) * LANES)
            acc = jnp.zeros((_CONV_RB, LANES), F32)
            for r in range(8):
                s = buf_ref[r0 + r:r0 + r + _CONV_RB + _HALO, lanes]
                for a in range(_HALO // 8 + 1):
                    k = 8 * a + r - first
                    if 0 <= k < width:
                        acc = acc + wdw_ref[k:k + 1, lanes] * s[8 * a:8 * a + _CONV_RB]
            cbuf_ref[:, lanes] = acc
        y = _layer_norm(cbuf_ref[...] + bdw_ref[...], lng_ref[...], lnb_ref[...])
        cn_ref[r0:r0 + _CONV_RB, :] = _silu(y).astype(BF16)

    co = _dot(cn_ref[...], wco_ref[...])
    out_ref[0] = (co * sgc_ref[0].astype(F32)).astype(BF16)
    buf_ref[0:_HALO, :] = buf_ref[tt:tt + _HALO, :]


def _conv_branch(u, cache, sgc, p, tt):
    b, t, c_conv = u.shape
    d = sgc.shape[-1]
    width = p["w_dw"].shape[0]
    assert t % tt == 0 and tt % _CONV_RB == 0 and tt >= _HALO and width - 1 <= _HALO
    cache_p = jnp.pad(cache.astype(F32), ((0, 0), (_HALO - (width - 1), 0), (0, 0)))
    full2 = lambda shape: pl.BlockSpec(shape, lambda i, j: (0, 0))
    return pl.pallas_call(
        _conv_kernel,
        grid=(b, t // tt),
        in_specs=[
            pl.BlockSpec((1, tt, c_conv), lambda i, j: (i, j, 0)),
            pl.BlockSpec((1, _HALO, c_conv), lambda i, j: (i, 0, 0)),
            full2((width, c_conv)), full2((1, c_conv)), full2((1, c_conv)), full2((1, c_conv)),
            full2((c_conv, d)),
            pl.BlockSpec((1, tt, d), lambda i, j: (i, j, 0)),
        ],
        out_specs=pl.BlockSpec((1, tt, d), lambda i, j: (i, j, 0)),
        out_shape=jax.ShapeDtypeStruct((b, t, d), BF16),
        scratch_shapes=[pltpu.VMEM((tt + _HALO, c_conv), F32),
                        pltpu.VMEM((_CONV_RB, c_conv), F32),
                        pltpu.VMEM((tt, c_conv), BF16)],
        compiler_params=pltpu.CompilerParams(
            dimension_semantics=("arbitrary", "arbitrary"), vmem_limit_bytes=VMEM_LIMIT),
        name="conv_branch",
    )(u, cache_p, p["w_dw"], p["b_dw"], p["lnc_g"], p["lnc_b"], p["w_conv_out"], sgc)


_SHORT_PAD = 8


def _delta_kernel(nc, qkv_ref, cache_ref, wsh_ref, bg_ref, bgt_ref, gsil_ref, ong_ref, s0_ref,
                  o_ref, sfin_ref, xb_ref, s_ref):
    c = pl.program_id(1)
    ck = CHUNK
    key_dim = N_HEADS * HEAD_DIM
    sw = wsh_ref.shape[0]

    @pl.when(c == 0)
    def _():
        s_ref[...] = s0_ref[0]
        xb_ref[0:_SHORT_PAD, :] = cache_ref[0]

    xb_ref[_SHORT_PAD:_SHORT_PAD + ck, :] = qkv_ref[0]

    def conv_cols(lo):
        lanes = slice(lo, lo + HEAD_DIM)
        acc = wsh_ref[sw - 1:sw, lanes] * xb_ref[_SHORT_PAD:_SHORT_PAD + ck, lanes]
        for k in range(sw - 1):
            r = _SHORT_PAD - (sw - 1) + k
            acc = acc + wsh_ref[k:k + 1, lanes] * xb_ref[r:r + ck, lanes]
        return _silu(acc)

    ri = lax.broadcasted_iota(jnp.int32, (ck, ck), 0)
    ci = lax.broadcasted_iota(jnp.int32, (ck, ck), 1)
    incl = ri >= ci
    strict = ri > ci
    eye = jnp.where(ri == ci, 1.0, 0.0).astype(F32)
    bd8 = (ri // 8) == (ci // 8)
    lvl = [((ri // (2 * s)) == (ci // (2 * s))) & ((ri // s) != (ci // s)) for s in (8, 16, 32)]
    tri_l = jnp.where(incl, 1.0, 0.0).astype(BF16)
    tri_u = jnp.where(ri <= ci, 1.0, 0.0).astype(BF16)

    bg = bg_ref[0]
    bgt = bgt_ref[0]
    gc_cols = sum(_dot(tri_l, part) for part in _split3(bg))
    gc_rows = sum(_dot(part, tri_u) for part in _split3(bgt))

    heads = range(N_HEADS)
    bf = lambda m: m.astype(BF16)
    q = [conv_cols(h * HEAD_DIM) for h in heads]
    k = [conv_cols(key_dim + h * HEAD_DIM) for h in heads]
    v = [conv_cols(2 * key_dim + h * HEAD_DIM) for h in heads]
    q = [x * lax.rsqrt(jnp.sum(x * x, axis=-1, keepdims=True) + NORM_EPS) * (HEAD_DIM ** -0.5) for x in q]
    k = [x * lax.rsqrt(jnp.sum(x * x, axis=-1, keepdims=True) + NORM_EPS) for x in k]
    beta = [bg[:, h:h + 1] for h in heads]
    gcc = [gc_cols[:, N_HEADS + h:N_HEADS + h + 1] for h in heads]
    gcr = [gc_rows[N_HEADS + h:N_HEADS + h + 1, :] for h in heads]
    decay = [jnp.exp(jnp.where(incl, gcc[h] - gcr[h], -jnp.inf)) for h in heads]
    kb = [k[h] * beta[h] for h in heads]
    kbf = [bf(x) for x in k]
    a = [jnp.where(strict, _dot_nt(bf(kb[h]), kbf[h]) * decay[h], 0.0) for h in heads]
    qk = [_dot_nt(bf(q[h]), kbf[h]) * decay[h] for h in heads]

    adb = [bf(jnp.where(bd8, x, 0.0)) for x in a]
    a2 = [_dot(x, x) for x in adb]
    a2b = [bf(x) for x in a2]
    a3 = [_dot(adb[h], a2b[h]) for h in heads]
    a4 = [_dot(x, x) for x in a2b]
    t = [eye - jnp.where(bd8, a[h], 0.0) + a2[h] - a3[h] for h in heads]
    t = [t[h] + _dot(bf(t[h]), bf(a4[h])) for h in heads]
    for m in lvl:
        x = [_dot(bf(jnp.where(m, a[h], 0.0)), bf(t[h])) for h in heads]
        t = [t[h] - _dot(bf(t[h]), bf(x[h])) for h in heads]

    egc = [jnp.exp(x) for x in gcc]
    sol = [_dot(bf(t[h]), bf(jnp.concatenate([v[h] * beta[h], kb[h] * egc[h]], axis=1))) for h in heads]
    g_last = [x[ck - 1:ck, :] for x in gcc]
    k_dec = [k[h] * jnp.exp(g_last[h] - gcc[h]) for h in heads]

    s = [s_ref[h] for h in heads]
    sb = [bf(x) for x in s]
    wq = [_dot(bf(jnp.concatenate([sol[h][:, HEAD_DIM:], q[h] * egc[h]], axis=0)), sb[h]) for h in heads]
    vb = [bf(sol[h][:, :HEAD_DIM] - wq[h][:ck]) for h in heads]
    o = [wq[h][ck:] + _dot(bf(qk[h]), vb[h]) for h in heads]
    for h in heads:
        s_ref[h] = s[h] * jnp.exp(g_last[h]) + _dot_tn(bf(k_dec[h]), vb[h])
    for h in heads:
        on = o[h] * lax.rsqrt(jnp.mean(o[h] * o[h], axis=-1, keepdims=True) + NORM_EPS) * ong_ref[...]
        lanes = slice(h * HEAD_DIM, (h + 1) * HEAD_DIM)
        o_ref[0, :, lanes] = (on * gsil_ref[0, :, lanes].astype(F32)).astype(BF16)

    xb_ref[0:_SHORT_PAD, :] = xb_ref[ck:ck + _SHORT_PAD, :]

    @pl.when(c == nc - 1)
    def _():
        sfin_ref[0] = s_ref[...]


def _delta_branch(qkv, cache, bg, bgt, gsil, s0, p):
    b, t, qkv_dim = qkv.shape
    assert t % CHUNK == 0
    nc = t // CHUNK
    sw = p["w_short"].shape[0]
    cache_p = jnp.pad(cache.astype(F32), ((0, 0), (_SHORT_PAD - (sw - 1), 0), (0, 0)))
    val_dim = N_HEADS * HEAD_DIM
    bgt3 = bgt.reshape(2 * N_HEADS, b * nc, CHUNK).transpose(1, 0, 2)
    return pl.pallas_call(
        functools.partial(_delta_kernel, nc),
        grid=(b, nc),
        in_specs=[
            pl.BlockSpec((1, CHUNK, qkv_dim), lambda i, j: (i, j, 0)),
            pl.BlockSpec((1, _SHORT_PAD, qkv_dim), lambda i, j: (i, 0, 0)),
            pl.BlockSpec((sw, qkv_dim), lambda i, j: (0, 0)),
            pl.BlockSpec((1, CHUNK, LANES), lambda i, j: (i, j, 0)),
            pl.BlockSpec((1, 2 * N_HEADS, CHUNK), lambda i, j: (i * nc + j, 0, 0)),
            pl.BlockSpec((1, CHUNK, val_dim), lambda i, j: (i, j, 0)),
            pl.BlockSpec((1, HEAD_DIM), lambda i, j: (0, 0)),
            pl.BlockSpec((1, N_HEADS, HEAD_DIM, HEAD_DIM), lambda i, j: (i, 0, 0, 0)),
        ],
        out_specs=[
            pl.BlockSpec((1, CHUNK, val_dim), lambda i, j: (i, j, 0)),
            pl.BlockSpec((1, N_HEADS, HEAD_DIM, HEAD_DIM), lambda i, j: (i, 0, 0, 0)),
        ],
        out_shape=[
            jax.ShapeDtypeStruct((b, t, val_dim), BF16),
            jax.ShapeDtypeStruct((b, N_HEADS, HEAD_DIM, HEAD_DIM), F32),
        ],
        scratch_shapes=[pltpu.VMEM((_SHORT_PAD + CHUNK, qkv_dim), F32),
                        pltpu.VMEM((N_HEADS, HEAD_DIM, HEAD_DIM), F32)],
        compiler_params=pltpu.CompilerParams(
            dimension_semantics=("arbitrary", "arbitrary"), vmem_limit_bytes=VMEM_LIMIT),
        name="delta_rule",
    )(qkv, cache_p, p["w_short"], bg.reshape(b, t, LANES), bgt3, gsil, p["o_norm_g"], s0.astype(F32))


_R_E1, _R_E2, _R_RANK1, _R_RANK2, _R_W1, _R_W2 = range(6)


def _merge_kernel(alpha, convg_ref, og_ref, sgd_ref, x_ref, wo_ref, wout_ref, g_ref, b_ref,
                  wr_ref, br_ref, h_ref, route_ref, cnt_ref, carry_ref):
    @pl.when(pl.program_id(0) == 0)
    def _():
        carry_ref[...] = jnp.zeros_like(carry_ref)

    d_out = _dot(og_ref[...], wo_ref[...])
    merged = convg_ref[...].astype(F32) + d_out * sgd_ref[...].astype(F32)
    mix = _dot(merged.astype(BF16), wout_ref[...])
    h = _layer_norm(alpha * x_ref[...] + mix, g_ref[...], b_ref[...])
    h_ref[...] = h

    h_hi, h_mid, _ = _split3(h)
    w_hi, w_mid, _ = _split3(wr_ref[...])
    logits = _dot(h_hi, w_hi) + _dot(h_mid, w_hi) + _dot(h_hi, w_mid) + br_ref[...]
    tm = logits.shape[0]
    col = lax.broadcasted_iota(jnp.int32, logits.shape, 1).astype(F32)
    big = float(LANES)
    is_g = col < N_GROUPS
    mg = jnp.max(jnp.where(is_g, logits, -jnp.inf), axis=-1, keepdims=True)
    sg = jnp.sum(jnp.where(is_g, jnp.exp(jnp.where(is_g, logits, mg) - mg), 0.0), axis=-1, keepdims=True)
    pg_top = 1.0 / sg
    gidx = jnp.min(jnp.where(is_g & (logits == mg), col, big), axis=-1, keepdims=True)
    lo = ROUTE_COL0 + EXP_PER_GROUP * gidx
    sel = (col >= lo) & (col < lo + EXP_PER_GROUP)
    le = jnp.where(sel, logits, -jnp.inf)
    m1 = jnp.max(le, axis=-1, keepdims=True)
    i1 = jnp.min(jnp.where(le == m1, col, big), axis=-1, keepdims=True)
    le2 = jnp.where(col == i1, -jnp.inf, le)
    m2 = jnp.max(le2, axis=-1, keepdims=True)
    i2 = jnp.min(jnp.where(le2 == m2, col, big), axis=-1, keepdims=True)
    e2 = jnp.exp(m2 - m1)
    den = 1.0 + e2
    w1 = pg_top / den
    w2 = pg_top * e2 / den

    hit1 = col == i1
    hit2 = col == i2
    member = jnp.where(hit1 | hit2, 1.0, 0.0)
    ri = lax.broadcasted_iota(jnp.int32, (tm, tm), 0)
    ci = lax.broadcasted_iota(jnp.int32, (tm, tm), 1)
    earlier = jnp.where(ri > ci, 1.0, 0.0).astype(BF16)
    before = _dot(earlier, member.astype(BF16)) + carry_ref[...]
    rank1 = jnp.sum(jnp.where(hit1, before, 0.0), axis=-1, keepdims=True)
    rank2 = jnp.sum(jnp.where(hit2, before, 0.0), axis=-1, keepdims=True)
    carry_ref[...] += jnp.sum(member, axis=0, keepdims=True)
    cnt_ref[...] = jnp.broadcast_to(carry_ref[...], cnt_ref.shape)

    fields = (i1 - ROUTE_COL0, i2 - ROUTE_COL0, rank1, rank2, w1, w2)
    route = jnp.zeros_like(logits)
    for c, val in enumerate(fields):
        route = jnp.where(col == float(c), val, route)
    route_ref[...] = route


def _merge(convg, og, sgd, x, p, alpha, tm):
    n, d = x.shape
    val_dim = og.shape[1]
    assert n % tm == 0
    row = lambda w: pl.BlockSpec((tm, w), lambda i: (i, 0))
    full = lambda shape: pl.BlockSpec(shape, lambda i: (0, 0))
    return pl.pallas_call(
        functools.partial(_merge_kernel, alpha),
        grid=(n // tm,),
        in_specs=[row(d), row(val_dim), row(d), row(d), full((val_dim, d)), full((d, d)),
                  full((1, d)), full((1, d)), full((d, LANES)), full((1, LANES))],
        out_specs=[row(d), row(LANES), full((8, LANES))],
        out_shape=[jax.ShapeDtypeStruct((n, d), F32), jax.ShapeDtypeStruct((n, LANES), F32),
                   jax.ShapeDtypeStruct((8, LANES), F32)],
        scratch_shapes=[pltpu.VMEM((1, LANES), F32)],
        compiler_params=pltpu.CompilerParams(
            dimension_semantics=("arbitrary",), vmem_limit_bytes=VMEM_LIMIT),
        name="merge_outproj",
    )(convg, og, sgd, x, p["w_o"], p["w_out"], p["ln1_g"], p["ln1_b"], p["w_router"], p["b_router"])


def _route_plan(route, cnt, te):
    n = route.shape[0]
    i32 = jnp.int32
    e1, e2 = route[:, _R_E1].astype(i32), route[:, _R_E2].astype(i32)
    counts = cnt[0, ROUTE_COL0:ROUTE_COL0 + N_EXPERTS].astype(i32)
    padded = (counts + te - 1) // te * te
    ends = jnp.cumsum(padded)
    offs = ends - padded
    dest1 = offs[e1] + route[:, _R_RANK1].astype(i32)
    dest2 = offs[e2] + route[:, _R_RANK2].astype(i32)
    n_tiles = (2 * n) // te + N_EXPERTS
    tile_expert = jnp.searchsorted(ends, jnp.arange(n_tiles, dtype=i32) * te, side="right")
    tile_expert = jnp.minimum(tile_expert, N_EXPERTS - 1).astype(i32)
    tiles_used = (ends[-1] // te).astype(i32).reshape(1)
    tok = jnp.arange(n, dtype=i32)
    src = jnp.zeros((n_tiles * te,), i32).at[dest1].set(tok, unique_indices=True)
    src = src.at[dest2].set(tok, unique_indices=True)
    return dest1, dest2, src, tile_expert, tiles_used


def _expert_kernel(te_ref, nt_ref, tok_ref, tokn_ref, h_hbm, wg_ref, wu_ref, wd_ref, out_ref, xs_ref, sem):
    del te_ref
    j = pl.program_id(0)
    nt = nt_ref[0]
    rows = xs_ref.shape[1]
    slot = j % 2

    def row_copy(tok, slot_, r):
        return pltpu.make_async_copy(h_hbm.at[pl.ds(tok, 1)], xs_ref.at[slot_, pl.ds(r, 1)], sem.at[slot_])

    def start(tref, slot_):
        def body(r, carry):
            row_copy(tref[0, 0, r], slot_, r).start()
            return carry
        lax.fori_loop(0, rows, body, 0, unroll=8)

    def wait(slot_):
        def body(r, carry):
            row_copy(0, slot_, r).wait()
            return carry
        lax.fori_loop(0, rows, body, 0, unroll=8)

    @pl.when(j == 0)
    def _():
        start(tok_ref, 0)

    @pl.when(j + 1 < nt)
    def _():
        start(tokn_ref, 1 - slot)

    @pl.when(j < nt)
    def _():
        wait(slot)
        x = xs_ref[slot].astype(BF16)
        hg = _dot(x, wg_ref[0])
        hu = _dot(x, wu_ref[0])
        out_ref[...] = _dot((_silu(hg) * hu).astype(BF16), wd_ref[0])

    @pl.when(j >= nt)
    def _():
        out_ref[...] = jnp.zeros_like(out_ref)


def _experts(h, src, tile_expert, tiles_used, p, te):
    n, d = h.shape
    ne, _, f = p["w_gate"].shape
    n_tiles = tile_expert.shape[0]
    src3 = src.reshape(n_tiles, 1, te)
    smem_tile = lambda fn: pl.BlockSpec((1, 1, te), fn, memory_space=pltpu.SMEM)
    grid_spec = pltpu.PrefetchScalarGridSpec(
        num_scalar_prefetch=2,
        grid=(n_tiles,),
        in_specs=[
            smem_tile(lambda j, te_, nt_: (j, 0, 0)),
            smem_tile(lambda j, te_, nt_: (jnp.minimum(j + 1, n_tiles - 1), 0, 0)),
            pl.BlockSpec(memory_space=pl.ANY),
            pl.BlockSpec((1, d, f), lambda j, te_, nt_: (te_[j], 0, 0)),
            pl.BlockSpec((1, d, f), lambda j, te_, nt_: (te_[j], 0, 0)),
            pl.BlockSpec((1, f, d), lambda j, te_, nt_: (te_[j], 0, 0)),
        ],
        out_specs=pl.BlockSpec((te, d), lambda j, te_, nt_: (j, 0)),
        scratch_shapes=[pltpu.VMEM((2, te, d), F32), pltpu.SemaphoreType.DMA((2,))],
    )
    return pl.pallas_call(
        _expert_kernel,
        grid_spec=grid_spec,
        out_shape=jax.ShapeDtypeStruct((n_tiles * te, d), F32),
        compiler_params=pltpu.CompilerParams(
            dimension_semantics=("arbitrary",), vmem_limit_bytes=VMEM_LIMIT),
        name="moe_experts",
    )(tile_expert, tiles_used, src3, src3, h, p["w_gate"], p["w_up"], p["w_down"])


def _combine_kernel(alpha, d1_ref, d2_ref, d1n_ref, d2n_ref, h_ref, route_ref, rows_hbm, g_ref, b_ref,
                    y_ref, o_ref, sem):
    i = pl.program_id(0)
    n_i = pl.num_programs(0)
    tm = h_ref.shape[0]
    slot = i % 2

    def row_copy(src_row, slot_, k, r):
        return pltpu.make_async_copy(rows_hbm.at[pl.ds(src_row, 1)], o_ref.at[slot_, k, pl.ds(r, 1)],
                                     sem.at[slot_])

    def start(da, db, slot_):
        def body(r, carry):
            row_copy(da[0, 0, r], slot_, 0, r).start()
            row_copy(db[0, 0, r], slot_, 1, r).start()
            return carry
        lax.fori_loop(0, tm, body, 0, unroll=8)

    def wait(slot_):
        def body(r, carry):
            row_copy(0, slot_, 0, r).wait()
            row_copy(0, slot_, 1, r).wait()
            return carry
        lax.fori_loop(0, tm, body, 0, unroll=8)

    @pl.when(i == 0)
    def _():
        start(d1_ref, d2_ref, 0)

    @pl.when(i + 1 < n_i)
    def _():
        start(d1n_ref, d2n_ref, 1 - slot)

    wait(slot)
    route = route_ref[...]
    col = lax.broadcasted_iota(jnp.int32, route.shape, 1)
    w1 = jnp.sum(jnp.where(col == _R_W1, route, 0.0), axis=-1, keepdims=True)
    w2 = jnp.sum(jnp.where(col == _R_W2, route, 0.0), axis=-1, keepdims=True)
    moe = w1 * o_ref[slot, 0] + w2 * o_ref[slot, 1]
    y_ref[...] = _layer_norm(alpha * h_ref[...] + moe, g_ref[...], b_ref[...])


def _combine(h, route, rows, dest1, dest2, p, alpha, tm):
    n, d = h.shape
    assert n % tm == 0
    n_i = n // tm
    d1 = dest1.reshape(n_i, 1, tm)
    d2 = dest2.reshape(n_i, 1, tm)
    cur = pl.BlockSpec((1, 1, tm), lambda i: (i, 0, 0), memory_space=pltpu.SMEM)
    nxt = pl.BlockSpec((1, 1, tm), lambda i: (jnp.minimum(i + 1, n_i - 1), 0, 0), memory_space=pltpu.SMEM)
    return pl.pallas_call(
        functools.partial(_combine_kernel, alpha),
        grid=(n_i,),
        in_specs=[cur, cur, nxt, nxt,
                  pl.BlockSpec((tm, d), lambda i: (i, 0)),
                  pl.BlockSpec((tm, LANES), lambda i: (i, 0)),
                  pl.BlockSpec(memory_space=pl.ANY),
                  pl.BlockSpec((1, d), lambda i: (0, 0)),
                  pl.BlockSpec((1, d), lambda i: (0, 0))],
        out_specs=pl.BlockSpec((tm, d), lambda i: (i, 0)),
        out_shape=jax.ShapeDtypeStruct((n, d), F32),
        scratch_shapes=[pltpu.VMEM((2, 2, tm, d), F32), pltpu.SemaphoreType.DMA((2,))],
        compiler_params=pltpu.CompilerParams(
            dimension_semantics=("arbitrary",), vmem_limit_bytes=VMEM_LIMIT),
        name="moe_combine",
    )(d1, d2, d1, d2, h, route, rows, p["ln2_g"], p["ln2_b"])


def _pack_layer(w_in, b_in, w_dw, b_dw, lnc_g, lnc_b, w_conv_out, w_short, a_log, dt_bias, o_norm_g,
                w_o, w_out, ln1_g, ln1_b, w_rg, b_rg, w_re, b_re, w_gate, w_up, w_down, ln2_g, ln2_b):
    d = w_in.shape[0]
    c_conv = w_dw.shape[1]
    qkv_dim = w_short.shape[1]
    o_qkv = 2 * c_conv
    o_ba = o_qkv + qkv_dim
    o_tail = o_ba + 2 * N_HEADS
    half = _TN // 2

    def interleave(a, b):
        r = a.shape[0]
        a = a.reshape(r, -1, 1, half)
        b = b.reshape(r, -1, 1, half)
        return jnp.concatenate([a, b], axis=2).reshape(r, -1)

    def pad_cols(a, width):
        return jnp.pad(a, ((0, 0), (0, width - a.shape[1])))

    b2 = b_in[None, :].astype(F32)
    nh = N_HEADS
    zeros_h = jnp.zeros((nh,), F32)
    head_params = jnp.stack([jnp.concatenate([zeros_h, a_log.astype(F32)]),
                             jnp.concatenate([zeros_h, dt_bias.astype(F32)])])
    w_ba = w_in[:, o_ba:o_tail]
    w_router = jnp.concatenate([w_rg, w_re], axis=1).astype(F32)
    b_router = jnp.concatenate([b_rg, b_re])[None, :].astype(F32)
    return dict(
        w_glu=interleave(w_in[:, :c_conv], w_in[:, c_conv:o_qkv]).astype(BF16),
        b_glu=interleave(b2[:, :c_conv], b2[:, c_conv:o_qkv]),
        w_qkv=w_in[:, o_qkv:o_ba].astype(BF16),
        b_qkv=b2[:, o_qkv:o_ba],
        w_ba=pad_cols(w_ba, LANES).astype(BF16),
        b_ba=pad_cols(b2[:, o_ba:o_tail], LANES),
        w_bat=w_ba.T.astype(BF16),
        b_bat=jnp.broadcast_to(b_in[o_ba:o_tail, None].astype(F32), (2 * nh, LANES)),
        w_tail=w_in[:, o_tail:].astype(BF16),
        b_tail=b2[:, o_tail:],
        p_row=pad_cols(head_params, LANES),
        p_col=pad_cols(head_params.T, LANES),
        w_dw=w_dw.astype(F32), b_dw=b_dw[None, :].astype(F32),
        lnc_g=lnc_g[None, :].astype(F32), lnc_b=lnc_b[None, :].astype(F32),
        w_conv_out=w_conv_out.astype(BF16),
        w_short=w_short.astype(F32),
        o_norm_g=o_norm_g[None, :].astype(F32),
        w_o=w_o.astype(BF16), w_out=w_out.astype(BF16),
        ln1_g=ln1_g[None, :].astype(F32), ln1_b=ln1_b[None, :].astype(F32),
        w_router=pad_cols(w_router, LANES), b_router=pad_cols(b_router, LANES),
        w_gate=w_gate.astype(BF16), w_up=w_up.astype(BF16), w_down=w_down.astype(BF16),
        ln2_g=ln2_g[None, :].astype(F32), ln2_b=ln2_b[None, :].astype(F32),
    )


_MERGE_TM = 256
_EXPERT_TE = 256


def _block(x, conv_buf, short_buf, s0, p, alpha, tm, conv_tt):
    b, t, d = x.shape
    n = b * t
    x2 = x.reshape(n, d).astype(F32)
    u, qkv, bg, bgt, gsil, sgc, sgd = _inproj(x2, p, tm)
    c_conv = u.shape[1]
    u3 = u.reshape(b, t, c_conv)
    qkv3 = qkv.reshape(b, t, -1)
    convg = _conv_branch(u3, conv_buf, sgc.reshape(b, t, d), p, conv_tt)
    og, s_new = _delta_branch(qkv3, short_buf, bg, bgt, gsil.reshape(b, t, -1), s0, p)
    h, route, cnt = _merge(convg.reshape(n, d), og.reshape(n, -1), sgd, x2, p, alpha, _MERGE_TM)
    dest1, dest2, src, tile_expert, tiles_used = _route_plan(route, cnt, _EXPERT_TE)
    rows = _experts(h, src, tile_expert, tiles_used, p, _EXPERT_TE)
    y = _combine(h, route, rows, dest1, dest2, p, alpha, _MERGE_TM)
    kc = conv_buf.shape[1]
    ks = short_buf.shape[1]
    assert t >= kc and t >= ks
    return (y.reshape(b, t, d).astype(x.dtype), u3[:, t - kc:].astype(x.dtype),
            qkv3[:, t - ks:].astype(x.dtype), s_new.astype(s0.dtype))


def kernel(x_prompt, x_sample, cache_conv, cache_short, state_delta, w_in, b_in, w_dw, b_dw, lnc_g, lnc_b, w_conv_out, w_short, a_log, dt_bias, o_norm_g, w_o, w_out, ln1_g, ln1_b, w_rg, b_rg, w_re, b_re, w_gate, w_up, w_down, ln2_g, ln2_b):
    weights = (w_in, b_in, w_dw, b_dw, lnc_g, lnc_b, w_conv_out, w_short, a_log, dt_bias, o_norm_g,
               w_o, w_out, ln1_g, ln1_b, w_rg, b_rg, w_re, b_re, w_gate, w_up, w_down, ln2_g, ln2_b)
    depth = w_in.shape[0]
    alpha = (2.0 * depth) ** 0.25
    yp, ys = x_prompt, x_sample
    bp = x_prompt.shape[0]
    outs = [[] for _ in range(6)]
    for l in range(depth):
        p = _pack_layer(*(wt[l] for wt in weights))
        zc = jnp.zeros((bp,) + cache_conv.shape[2:], x_prompt.dtype)
        zs = jnp.zeros((bp,) + cache_short.shape[2:], x_prompt.dtype)
        zd = jnp.zeros((bp,) + state_delta.shape[2:], state_delta.dtype)
        yp, c, s, dl = _block(yp, zc, zs, zd, p, alpha, 1024, 256)
        outs[0].append(c), outs[1].append(s), outs[2].append(dl)
        ys, c, s, dl = _block(ys, cache_conv[l], cache_short[l], state_delta[l], p, alpha, 1024, 64)
        outs[3].append(c), outs[4].append(s), outs[5].append(dl)
    return (yp, ys) + tuple(jnp.stack(o) for o in outs)
```

```python
import functools

import jax
import jax.numpy as jnp
from jax import lax
from jax.experimental import pallas as pl
from jax.experimental.pallas import tpu as pltpu

F32 = jnp.float32
BF16 = jnp.bfloat16

CHUNK = 64
N_HEADS = 8
HEAD_DIM = 128
N_GROUPS = 4
EXP_PER_GROUP = 8
N_EXPERTS = N_GROUPS * EXP_PER_GROUP
LN_EPS = 1e-5
NORM_EPS = 1e-6
LANES = 128
ROUTE_COL0 = N_GROUPS
VMEM_LIMIT = 56 * 1024 * 1024


def _dot(a, b):
    return jnp.dot(a, b, preferred_element_type=F32)


def _dot_nt(a, b):
    return lax.dot_general(a, b, (((1,), (1,)), ((), ())), preferred_element_type=F32)


def _dot_tn(a, b):
    return lax.dot_general(a, b, (((0,), (0,)), ((), ())), preferred_element_type=F32)


def _split3(x):
    hi = x.astype(BF16)
    r1 = x - hi.astype(F32)
    mid = r1.astype(BF16)
    lo = (r1 - mid.astype(F32)).astype(BF16)
    return hi, mid, lo


def _sigmoid(x):
    return 1.0 / (1.0 + jnp.exp(-x))


def _silu(x):
    return x * _sigmoid(x)


def _softplus(x):
    return jnp.maximum(x, 0.0) + jnp.log1p(jnp.exp(-jnp.abs(x)))


def _layer_norm(x, g, b):
    mu = jnp.mean(x, axis=-1, keepdims=True)
    xc = x - mu
    var = jnp.mean(xc * xc, axis=-1, keepdims=True)
    return xc * lax.rsqrt(var + LN_EPS) * g + b


def _clamp(v, lo, hi):
    return jnp.minimum(jnp.maximum(v, lo), hi)


_TN = 512
_J_GLU, _J_QKV, _J_BA, _J_GO, _J_GC, _J_GD, _J_END = 0, 4, 10, 11, 13, 17, 21


def _inproj_kernel(x_ref, wglu_ref, wqkv_ref, wba_ref, wbat_ref, wtail_ref,
                   bglu_ref, bqkv_ref, bba_ref, bbat_ref, btail_ref, prow_ref, pcol_ref,
                   u_ref, qkv_ref, bg_ref, bgt_ref, gsil_ref, sgc_ref, sgd_ref, xb_ref):
    j = pl.program_id(1)

    @pl.when(j == 0)
    def _():
        xb_ref[...] = x_ref[...].astype(BF16)

    @pl.when(j < _J_QKV)
    def _():
        z = _dot(xb_ref[...], wglu_ref[...]) + bglu_ref[...]
        half = _TN // 2
        u_ref[...] = z[:, :half] * _sigmoid(z[:, half:])

    @pl.when((j >= _J_QKV) & (j < _J_BA))
    def _():
        qkv_ref[...] = _dot(xb_ref[...], wqkv_ref[...]) + bqkv_ref[...]

    @pl.when(j == _J_BA)
    def _():
        xb = xb_ref[...]
        z = _dot(xb, wba_ref[...]) + bba_ref[...]
        col = lax.broadcasted_iota(jnp.int32, z.shape, 1)
        g = -jnp.exp(prow_ref[0:1, :]) * _softplus(z + prow_ref[1:2, :])
        bg_ref[...] = jnp.where(col < N_HEADS, _sigmoid(z), g)
        zt = _dot_nt(wbat_ref[...], xb) + bbat_ref[:, 0:1]
        row = lax.broadcasted_iota(jnp.int32, zt.shape, 0)
        gt = -jnp.exp(pcol_ref[:, 0:1]) * _softplus(zt + pcol_ref[:, 1:2])
        bgt_ref[...] = jnp.where(row < N_HEADS, _sigmoid(zt), gt)

    @pl.when((j >= _J_GO) & (j < _J_GC))
    def _():
        z = _dot(xb_ref[...], wtail_ref[...]) + btail_ref[...]
        gsil_ref[...] = _silu(z).astype(BF16)

    @pl.when((j >= _J_GC) & (j < _J_GD))
    def _():
        z = _dot(xb_ref[...], wtail_ref[...]) + btail_ref[...]
        sgc_ref[...] = _sigmoid(z).astype(BF16)

    @pl.when(j >= _J_GD)
    def _():
        z = _dot(xb_ref[...], wtail_ref[...]) + btail_ref[...]
        sgd_ref[...] = _sigmoid(z).astype(BF16)


def _inproj(x, pk, tm):
    n, d = x.shape
    c_conv = pk["w_glu"].shape[1] // 2
    qkv_dim = pk["w_qkv"].shape[1]
    val_dim = N_HEADS * HEAD_DIM
    assert n % tm == 0

    def cm(lo, hi):
        return lambda i, j: (0, _clamp(j - lo, 0, hi - lo - 1))

    def om(lo, hi):
        return lambda i, j: (i, _clamp(j - lo, 0, hi - lo - 1))

    in_specs = [
        pl.BlockSpec((tm, d), lambda i, j: (i, 0)),
        pl.BlockSpec((d, _TN), cm(_J_GLU, _J_QKV)),
        pl.BlockSpec((d, _TN), cm(_J_QKV, _J_BA)),
        pl.BlockSpec((d, LANES), lambda i, j: (0, 0)),
        pl.BlockSpec((2 * N_HEADS, d), lambda i, j: (0, 0)),
        pl.BlockSpec((d, _TN), cm(_J_GO, _J_END)),
        pl.BlockSpec((1, _TN), cm(_J_GLU, _J_QKV)),
        pl.BlockSpec((1, _TN), cm(_J_QKV, _J_BA)),
        pl.BlockSpec((1, LANES), lambda i, j: (0, 0)),
        pl.BlockSpec((2 * N_HEADS, LANES), lambda i, j: (0, 0)),
        pl.BlockSpec((1, _TN), cm(_J_GO, _J_END)),
        pl.BlockSpec((2, LANES), lambda i, j: (0, 0)),
        pl.BlockSpec((2 * N_HEADS, LANES), lambda i, j: (0, 0)),
    ]
    out_shape = [
        jax.ShapeDtypeStruct((n, c_conv), F32),
        jax.ShapeDtypeStruct((n, qkv_dim), F32),
        jax.ShapeDtypeStruct((n, LANES), F32),
        jax.ShapeDtypeStruct((2 * N_HEADS, n), F32),
        jax.ShapeDtypeStruct((n, val_dim), BF16),
        jax.ShapeDtypeStruct((n, d), BF16),
        jax.ShapeDtypeStruct((n, d), BF16),
    ]
    out_specs = [
        pl.BlockSpec((tm, _TN // 2), om(_J_GLU, _J_QKV)),
        pl.BlockSpec((tm, _TN), om(_J_QKV, _J_BA)),
        pl.BlockSpec((tm, LANES), lambda i, j: (i, 0)),
        pl.BlockSpec((2 * N_HEADS, tm), lambda i, j: (0, i)),
        pl.BlockSpec((tm, _TN), om(_J_GO, _J_GC)),
        pl.BlockSpec((tm, _TN), om(_J_GC, _J_GD)),
        pl.BlockSpec((tm, _TN), om(_J_GD, _J_END)),
    ]
    return pl.pallas_call(
        _inproj_kernel,
        grid=(n // tm, _J_END),
        in_specs=in_specs,
        out_specs=out_specs,
        out_shape=out_shape,
        scratch_shapes=[pltpu.VMEM((tm, d), BF16)],
        compiler_params=pltpu.CompilerParams(
            dimension_semantics=("arbitrary", "arbitrary"), vmem_limit_bytes=VMEM_LIMIT),
        name="inproj",
    )(x, pk["w_glu"], pk["w_qkv"], pk["w_ba"], pk["w_bat"], pk["w_tail"],
      pk["b_glu"], pk["b_qkv"], pk["b_ba"], pk["b_bat"], pk["b_tail"], pk["p_row"], pk["p_col"])


_HALO = 32
_CONV_RB = 64


def _conv_kernel(u_ref, cache_ref, wdw_ref, bdw_ref, lng_ref, lnb_ref, wco_ref, sgc_ref,
                 out_ref, buf_ref, cbuf_ref, cn_ref):
    t = pl.program_id(1)
    tt = u_ref.shape[1]
    c_conv = u_ref.shape[2]
    width = wdw_ref.shape[0]
    first = _HALO - (width - 1)

    @pl.when(t == 0)
    def _():
        buf_ref[0:_HALO, :] = cache_ref[0]

    buf_ref[_HALO:_HALO + tt, :] = u_ref[0]

    for rb in range(tt // _CONV_RB):
        r0 = rb * _CONV_RB
        for cb in range(c_conv // LANES):
            lanes = slice(cb * LANES, (cb + 1) * LANES)
            acc = jnp.zeros((_CONV_RB, LANES), F32)
            for r in range(8):
                s = buf_ref[r0 + r:r0 + r + _CONV_RB + _HALO, lanes]
                for a in range(_HALO // 8 + 1):
                    k = 8 * a + r - first
                    if 0 <= k < width:
                        acc = acc + wdw_ref[k:k + 1, lanes] * s[8 * a:8 * a + _CONV_RB]
            cbuf_ref[:, lanes] = acc
        y = _layer_norm(cbuf_ref[...] + bdw_ref[...], lng_ref[...], lnb_ref[...])
        cn_ref[r0:r0 + _CONV_RB, :] = _silu(y).astype(BF16)

    co = _dot(cn_ref[...], wco_ref[...])
    out_ref[0] = (co * sgc_ref[0].astype(F32)).astype(BF16)
    buf_ref[0:_HALO, :] = buf_ref[tt:tt + _HALO, :]


def _conv_branch(u, cache, sgc, p, tt):
    b, t, c_conv = u.shape
    d = sgc.shape[-1]
    width = p["w_dw"].shape[0]
    assert t % tt == 0 and tt % _CONV_RB == 0 and tt >= _HALO and width - 1 <= _HALO
    cache_p = jnp.pad(cache.astype(F32), ((0, 0), (_HALO - (width - 1), 0), (0, 0)))
    full2 = lambda shape: pl.BlockSpec(shape, lambda i, j: (0, 0))
    return pl.pallas_call(
        _conv_kernel,
        grid=(b, t // tt),
        in_specs=[
            pl.BlockSpec((1, tt, c_conv), lambda i, j: (i, j, 0)),
            pl.BlockSpec((1, _HALO, c_conv), lambda i, j: (i, 0, 0)),
            full2((width, c_conv)), full2((1, c_conv)), full2((1, c_conv)), full2((1, c_conv)),
            full2((c_conv, d)),
            pl.BlockSpec((1, tt, d), lambda i, j: (i, j, 0)),
        ],
        out_specs=pl.BlockSpec((1, tt, d), lambda i, j: (i, j, 0)),
        out_shape=jax.ShapeDtypeStruct((b, t, d), BF16),
        scratch_shapes=[pltpu.VMEM((tt + _HALO, c_conv), F32),
                        pltpu.VMEM((_CONV_RB, c_conv), F32),
                        pltpu.VMEM((tt, c_conv), BF16)],
        compiler_params=pltpu.CompilerParams(
            dimension_semantics=("arbitrary", "arbitrary"), vmem_limit_bytes=VMEM_LIMIT),
        name="conv_branch",
    )(u, cache_p, p["w_dw"], p["b_dw"], p["lnc_g"], p["lnc_b"], p["w_conv_out"], sgc)


_SHORT_PAD = 8


def _delta_kernel(nc, qkv_ref, cache_ref, wsh_ref, bg_ref, bgt_ref, gsil_ref, ong_ref, s0_ref,
                  o_ref, sfin_ref, xb_ref, s_ref):
    c = pl.program_id(1)
    ck = CHUNK
    key_dim = N_HEADS * HEAD_DIM
    sw = wsh_ref.shape[0]

    @pl.when(c == 0)
    def _():
        s_ref[...] = s0_ref[0]
        xb_ref[0:_SHORT_PAD, :] = cache_ref[0]

    xb_ref[_SHORT_PAD:_SHORT_PAD + ck, :] = qkv_ref[0]

    def conv_cols(lo):
        lanes = slice(lo, lo + HEAD_DIM)
        acc = wsh_ref[sw - 1:sw, lanes] * xb_ref[_SHORT_PAD:_SHORT_PAD + ck, lanes]
        for k in range(sw - 1):
            r = _SHORT_PAD - (sw - 1) + k
            acc = acc + wsh_ref[k:k + 1, lanes] * xb_ref[r:r + ck, lanes]
        return _silu(acc)

    ri = lax.broadcasted_iota(jnp.int32, (ck, ck), 0)
    ci = lax.broadcasted_iota(jnp.int32, (ck, ck), 1)
    incl = ri >= ci
    strict = ri > ci
    eye = jnp.where(ri == ci, 1.0, 0.0).astype(F32)
    bd8 = (ri // 8) == (ci // 8)
    lvl = [((ri // (2 * s)) == (ci // (2 * s))) & ((ri // s) != (ci // s)) for s in (8, 16, 32)]
    tri_l = jnp.where(incl, 1.0, 0.0).astype(BF16)
    tri_u = jnp.where(ri <= ci, 1.0, 0.0).astype(BF16)

    bg = bg_ref[0]
    bgt = bgt_ref[0]
    gc_cols = sum(_dot(tri_l, part) for part in _split3(bg))
    gc_rows = sum(_dot(part, tri_u) for part in _split3(bgt))

    heads = range(N_HEADS)
    bf = lambda m: m.astype(BF16)
    q = [conv_cols(h * HEAD_DIM) for h in heads]
    k = [conv_cols(key_dim + h * HEAD_DIM) for h in heads]
    v = [conv_cols(2 * key_dim + h * HEAD_DIM) for h in heads]
    q = [x * lax.rsqrt(jnp.sum(x * x, axis=-1, keepdims=True) + NORM_EPS) * (HEAD_DIM ** -0.5) for x in q]
    k = [x * lax.rsqrt(jnp.sum(x * x, axis=-1, keepdims=True) + NORM_EPS) for x in k]
    beta = [bg[:, h:h + 1] for h in heads]
    gcc = [gc_cols[:, N_HEADS + h:N_HEADS + h + 1] for h in heads]
    gcr = [gc_rows[N_HEADS + h:N_HEADS + h + 1, :] for h in heads]
    decay = [jnp.exp(jnp.where(incl, gcc[h] - gcr[h], -jnp.inf)) for h in heads]
    kb = [k[h] * beta[h] for h in heads]
    kbf = [bf(x) for x in k]
    a = [jnp.where(strict, _dot_nt(bf(kb[h]), kbf[h]) * decay[h], 0.0) for h in heads]
    qk = [_dot_nt(bf(q[h]), kbf[h]) * decay[h] for h in heads]

    adb = [bf(jnp.where(bd8, x, 0.0)) for x in a]
    a2 = [_dot(x, x) for x in adb]
    a2b = [bf(x) for x in a2]
    a3 = [_dot(adb[h], a2b[h]) for h in heads]
    a4 = [_dot(x, x) for x in a2b]
    t = [eye - jnp.where(bd8, a[h], 0.0) + a2[h] - a3[h] for h in heads]
    t = [t[h] + _dot(bf(t[h]), bf(a4[h])) for h in heads]
    for m in lvl:
        x = [_dot(bf(jnp.where(m, a[h], 0.0)), bf(t[h])) for h in heads]
        t = [t[h] - _dot(bf(t[h]), bf(x[h])) for h in heads]

    egc = [jnp.exp(x) for x in gcc]
    sol = [_dot(bf(t[h]), bf(jnp.concatenate([v[h] * beta[h], kb[h] * egc[h]], axis=1))) for h in heads]
    g_last = [x[ck - 1:ck, :] for x in gcc]
    k_dec = [k[h] * jnp.exp(g_last[h] - gcc[h]) for h in heads]

    s = [s_ref[h] for h in heads]
    sb = [bf(x) for x in s]
    wq = [_dot(bf(jnp.concatenate([sol[h][:, HEAD_DIM:], q[h] * egc[h]], axis=0)), sb[h]) for h in heads]
    vb = [bf(sol[h][:, :HEAD_DIM] - wq[h][:ck]) for h in heads]
    o = [wq[h][ck:] + _dot(bf(qk[h]), vb[h]) for h in heads]
    for h in heads:
        s_ref[h] = s[h] * jnp.exp(g_last[h]) + _dot_tn(bf(k_dec[h]), vb[h])
    for h in heads:
        on = o[h] * lax.rsqrt(jnp.mean(o[h] * o[h], axis=-1, keepdims=True) + NORM_EPS) * ong_ref[...]
        lanes = slice(h * HEAD_DIM, (h + 1) * HEAD_DIM)
        o_ref[0, :, lanes] = (on * gsil_ref[0, :, lanes].astype(F32)).astype(BF16)

    xb_ref[0:_SHORT_PAD, :] = xb_ref[ck:ck + _SHORT_PAD, :]

    @pl.when(c == nc - 1)
    def _():
        sfin_ref[0] = s_ref[...]


def _delta_branch(qkv, cache, bg, bgt, gsil, s0, p):
    b, t, qkv_dim = qkv.shape
    assert t % CHUNK == 0
    nc = t // CHUNK
    sw = p["w_short"].shape[0]
    cache_p = jnp.pad(cache.astype(F32), ((0, 0), (_SHORT_PAD - (sw - 1), 0), (0, 0)))
    val_dim = N_HEADS * HEAD_DIM
    bgt3 = bgt.reshape(2 * N_HEADS, b * nc, CHUNK).transpose(1, 0, 2)
    return pl.pallas_call(
        functools.partial(_delta_kernel, nc),
        grid=(b, nc),
        in_specs=[
            pl.BlockSpec((1, CHUNK, qkv_dim), lambda i, j: (i, j, 0)),
            pl.BlockSpec((1, _SHORT_PAD, qkv_dim), lambda i, j: (i, 0, 0)),
            pl.BlockSpec((sw, qkv_dim), lambda i, j: (0, 0)),
            pl.BlockSpec((1, CHUNK, LANES), lambda i, j: (i, j, 0)),
            pl.BlockSpec((1, 2 * N_HEADS, CHUNK), lambda i, j: (i * nc + j, 0, 0)),
            pl.BlockSpec((1, CHUNK, val_dim), lambda i, j: (i, j, 0)),
            pl.BlockSpec((1, HEAD_DIM), lambda i, j: (0, 0)),
            pl.BlockSpec((1, N_HEADS, HEAD_DIM, HEAD_DIM), lambda i, j: (i, 0, 0, 0)),
        ],
        out_specs=[
            pl.BlockSpec((1, CHUNK, val_dim), lambda i, j: (i, j, 0)),
            pl.BlockSpec((1, N_HEADS, HEAD_DIM, HEAD_DIM), lambda i, j: (i, 0, 0, 0)),
        ],
        out_shape=[
            jax.ShapeDtypeStruct((b, t, val_dim), BF16),
            jax.ShapeDtypeStruct((b, N_HEADS, HEAD_DIM, HEAD_DIM), F32),
        ],
        scratch_shapes=[pltpu.VMEM((_SHORT_PAD + CHUNK, qkv_dim), F32),
                        pltpu.VMEM((N_HEADS, HEAD_DIM, HEAD_DIM), F32)],
        compiler_params=pltpu.CompilerParams(
            dimension_semantics=("arbitrary", "arbitrary"), vmem_limit_bytes=VMEM_LIMIT),
        name="delta_rule",
    )(qkv, cache_p, p["w_short"], bg.reshape(b, t, LANES), bgt3, gsil, p["o_norm_g"], s0.astype(F32))


_R_E1, _R_E2, _R_RANK1, _R_RANK2, _R_W1, _R_W2 = range(6)


def _merge_kernel(alpha, convg_ref, og_ref, sgd_ref, x_ref, wo_ref, wout_ref, g_ref, b_ref,
                  wr_ref, br_ref, h_ref, route_ref, routet_ref, cnt_ref, carry_ref):
    @pl.when(pl.program_id(0) == 0)
    def _():
        carry_ref[...] = jnp.zeros_like(carry_ref)

    d_out = _dot(og_ref[...], wo_ref[...])
    merged = convg_ref[...].astype(F32) + d_out * sgd_ref[...].astype(F32)
    mix = _dot(merged.astype(BF16), wout_ref[...])
    h = _layer_norm(alpha * x_ref[...] + mix, g_ref[...], b_ref[...])
    h_ref[...] = h

    h_hi, h_mid, _ = _split3(h)
    w_hi, w_mid, _ = _split3(wr_ref[...])
    logits = _dot(h_hi, w_hi) + _dot(h_mid, w_hi) + _dot(h_hi, w_mid) + br_ref[...]
    tm = logits.shape[0]
    col = lax.broadcasted_iota(jnp.int32, logits.shape, 1).astype(F32)
    big = float(LANES)
    is_g = col < N_GROUPS
    mg = jnp.max(jnp.where(is_g, logits, -jnp.inf), axis=-1, keepdims=True)
    sg = jnp.sum(jnp.where(is_g, jnp.exp(jnp.where(is_g, logits, mg) - mg), 0.0), axis=-1, keepdims=True)
    pg_top = 1.0 / sg
    gidx = jnp.min(jnp.where(is_g & (logits == mg), col, big), axis=-1, keepdims=True)
    lo = ROUTE_COL0 + EXP_PER_GROUP * gidx
    sel = (col >= lo) & (col < lo + EXP_PER_GROUP)
    le = jnp.where(sel, logits, -jnp.inf)
    m1 = jnp.max(le, axis=-1, keepdims=True)
    i1 = jnp.min(jnp.where(le == m1, col, big), axis=-1, keepdims=True)
    le2 = jnp.where(col == i1, -jnp.inf, le)
    m2 = jnp.max(le2, axis=-1, keepdims=True)
    i2 = jnp.min(jnp.where(le2 == m2, col, big), axis=-1, keepdims=True)
    e2 = jnp.exp(m2 - m1)
    den = 1.0 + e2
    w1 = pg_top / den
    w2 = pg_top * e2 / den

    hit1 = col == i1
    hit2 = col == i2
    member = jnp.where(hit1 | hit2, 1.0, 0.0)
    ri = lax.broadcasted_iota(jnp.int32, (tm, tm), 0)
    ci = lax.broadcasted_iota(jnp.int32, (tm, tm), 1)
    earlier = jnp.where(ri > ci, 1.0, 0.0).astype(BF16)
    before = _dot(earlier, member.astype(BF16)) + carry_ref[...]
    rank1 = jnp.sum(jnp.where(hit1, before, 0.0), axis=-1, keepdims=True)
    rank2 = jnp.sum(jnp.where(hit2, before, 0.0), axis=-1, keepdims=True)
    carry_ref[...] += jnp.sum(member, axis=0, keepdims=True)
    cnt_ref[...] = jnp.broadcast_to(carry_ref[...], cnt_ref.shape)

    fields = (i1 - ROUTE_COL0, i2 - ROUTE_COL0, rank1, rank2, w1, w2)
    route = jnp.zeros_like(logits)
    for c, val in enumerate(fields):
        route = jnp.where(col == float(c), val, route)
    route_ref[...] = route
    sr = lax.broadcasted_iota(jnp.int32, (8, LANES), 0)
    sc = lax.broadcasted_iota(jnp.int32, (8, LANES), 1)
    pick = jnp.where(sr == sc, 1.0, 0.0).astype(BF16)
    routet_ref[...] = sum(_dot_nt(pick, part) for part in _split3(route))


def _merge(convg, og, sgd, x, p, alpha, tm):
    n, d = x.shape
    val_dim = og.shape[1]
    assert n % tm == 0
    row = lambda w: pl.BlockSpec((tm, w), lambda i: (i, 0))
    full = lambda shape: pl.BlockSpec(shape, lambda i: (0, 0))
    return pl.pallas_call(
        functools.partial(_merge_kernel, alpha),
        grid=(n // tm,),
        in_specs=[row(d), row(val_dim), row(d), row(d), full((val_dim, d)), full((d, d)),
                  full((1, d)), full((1, d)), full((d, LANES)), full((1, LANES))],
        out_specs=[row(d), row(LANES), pl.BlockSpec((8, tm), lambda i: (0, i)), full((8, LANES))],
        out_shape=[jax.ShapeDtypeStruct((n, d), F32), jax.ShapeDtypeStruct((n, LANES), F32),
                   jax.ShapeDtypeStruct((8, n), F32), jax.ShapeDtypeStruct((8, LANES), F32)],
        scratch_shapes=[pltpu.VMEM((1, LANES), F32)],
        compiler_params=pltpu.CompilerParams(
            dimension_semantics=("arbitrary",), vmem_limit_bytes=VMEM_LIMIT),
        name="merge_outproj",
    )(convg, og, sgd, x, p["w_o"], p["w_out"], p["ln1_g"], p["ln1_b"], p["w_router"], p["b_router"])


def _route_plan(route_t, cnt, te):
    n = route_t.shape[1]
    i32 = jnp.int32
    ri = route_t[:4].astype(i32)
    counts = cnt[0, ROUTE_COL0:ROUTE_COL0 + N_EXPERTS].astype(i32)
    ends = jnp.cumsum(counts)
    starts = ends - counts
    eids = jnp.arange(N_EXPERTS, dtype=i32)

    def lookup(table, idx):
        return jnp.sum(jnp.where(idx[None, :] == eids[:, None], table[:, None], 0), axis=0)

    dest = jnp.stack([lookup(starts, ri[0]) + ri[2], lookup(starts, ri[1]) + ri[3]])

    first_tile = starts // te
    last_tile = (ends - 1) // te
    items_e = jnp.where(counts > 0, last_tile - first_tile + 1, 0)
    item_end = jnp.cumsum(items_e)
    item_start = item_end - items_e
    total = item_end[-1]
    n_items = (2 * n) // te + N_EXPERTS - 1
    w = jnp.minimum(jnp.arange(n_items, dtype=i32), total - 1)
    item_e = jnp.sum((item_end[:, None] <= w[None, :]).astype(i32), axis=0)
    item_tile = lookup(first_tile, item_e) + w - lookup(item_start, item_e)
    lo = jnp.clip(lookup(starts, item_e) - item_tile * te, 0, te)
    hi = jnp.clip(lookup(ends, item_e) - item_tile * te, 0, te)
    return dest, (item_tile, item_e, lo, hi, total.reshape(1))


def _dispatch_kernel(n_steps, d_ref, h_hbm, xs_hbm, sem):
    i = pl.program_id(0)
    tm = d_ref.shape[-1]
    slot = i % 2

    def row_copy(tok, dst, slot_):
        return pltpu.make_async_copy(h_hbm.at[pl.ds(tok, 1)], xs_hbm.at[pl.ds(dst, 1)], sem.at[slot_])

    def start(r, carry):
        tok = i * tm + r
        row_copy(tok, d_ref[0, 0, 0, r], slot).start()
        row_copy(tok, d_ref[1, 0, 0, r], slot).start()
        return carry

    def wait_tile(slot_):
        def body(r, carry):
            row_copy(0, 0, slot_).wait()
            row_copy(0, 0, slot_).wait()
            return carry
        lax.fori_loop(0, tm, body, 0, unroll=8)

    lax.fori_loop(0, tm, start, 0, unroll=8)

    @pl.when(i > 0)
    def _():
        wait_tile(1 - slot)

    @pl.when(i == n_steps - 1)
    def _():
        wait_tile(slot)


def _dispatch(h, dest, tm):
    n, d = h.shape
    assert n % tm == 0
    n_steps = n // tm
    return pl.pallas_call(
        functools.partial(_dispatch_kernel, n_steps),
        grid=(n_steps,),
        in_specs=[pl.BlockSpec((2, 1, 1, tm), lambda i: (0, i, 0, 0), memory_space=pltpu.SMEM),
                  pl.BlockSpec(memory_space=pl.ANY)],
        out_specs=pl.BlockSpec(memory_space=pl.ANY),
        out_shape=jax.ShapeDtypeStruct((2 * n, d), F32),
        scratch_shapes=[pltpu.SemaphoreType.DMA((2,))],
        compiler_params=pltpu.CompilerParams(
            dimension_semantics=("arbitrary",), vmem_limit_bytes=VMEM_LIMIT),
        name="moe_dispatch",
    )(dest.reshape(2, n_steps, 1, tm), h)


def _expert_kernel(tile_ref, exp_ref, lo_ref, hi_ref, tot_ref, xs_ref, wg_ref, wu_ref, wd_ref,
                   out_ref, wgb_ref, wub_ref, wdb_ref):
    w = pl.program_id(0)
    prev = jnp.maximum(w - 1, 0)
    live = w < tot_ref[0]
    new_expert = (w == 0) | (exp_ref[w] != exp_ref[prev])
    first_of_tile = (w == 0) | (tile_ref[w] != tile_ref[prev])

    @pl.when(live & new_expert)
    def _():
        wgb_ref[...] = wg_ref[0].astype(BF16)
        wub_ref[...] = wu_ref[0].astype(BF16)
        wdb_ref[...] = wd_ref[0].astype(BF16)

    @pl.when(live)
    def _():
        x = xs_ref[...].astype(BF16)
        hg = _dot(x, wgb_ref[...])
        hu = _dot(x, wub_ref[...])
        row = lax.broadcasted_iota(jnp.int32, (x.shape[0], 1), 0)
        mine = (row >= lo_ref[w]) & (row < hi_ref[w])
        act = jnp.where(mine, _silu(hg) * hu, 0.0).astype(BF16)
        part = _dot(act, wdb_ref[...])

        @pl.when(first_of_tile)
        def _():
            out_ref[...] = part

        @pl.when(jnp.logical_not(first_of_tile))
        def _():
            out_ref[...] += part


def _experts(xs, items, p, te):
    rows, d = xs.shape
    ne, _, f = p["w_gate"].shape
    item_tile, item_e, lo, hi, total = items
    n_items = item_tile.shape[0]
    tile_map = lambda w, t_, e_, lo_, hi_, n_: (t_[w], 0)
    exp_map = lambda w, t_, e_, lo_, hi_, n_: (e_[w], 0, 0)
    grid_spec = pltpu.PrefetchScalarGridSpec(
        num_scalar_prefetch=5,
        grid=(n_items,),
        in_specs=[
            pl.BlockSpec((te, d), tile_map),
            pl.BlockSpec((1, d, f), exp_map),
            pl.BlockSpec((1, d, f), exp_map),
            pl.BlockSpec((1, f, d), exp_map),
        ],
        out_specs=pl.BlockSpec((te, d), tile_map),
        scratch_shapes=[pltpu.VMEM((d, f), BF16), pltpu.VMEM((d, f), BF16), pltpu.VMEM((f, d), BF16)],
    )
    return pl.pallas_call(
        _expert_kernel,
        grid_spec=grid_spec,
        out_shape=jax.ShapeDtypeStruct((rows, d), F32),
        compiler_params=pltpu.CompilerParams(
            dimension_semantics=("arbitrary",), vmem_limit_bytes=VMEM_LIMIT),
        name="moe_experts",
    )(item_tile, item_e, lo, hi, total, xs, p["w_gate"], p["w_up"], p["w_down"])


def _combine_kernel(alpha, d_ref, dn_ref, h_ref, route_ref, rows_hbm, g_ref, b_ref, y_ref, o_ref, sem):
    i = pl.program_id(0)
    n_i = pl.num_programs(0)
    tm = h_ref.shape[0]
    slot = i % 2

    def row_copy(src_row, slot_, k, r):
        return pltpu.make_async_copy(rows_hbm.at[pl.ds(src_row, 1)], o_ref.at[slot_, k, pl.ds(r, 1)],
                                     sem.at[slot_])

    def start(dref, slot_):
        def body(r, carry):
            row_copy(dref[0, 0, 0, r], slot_, 0, r).start()
            row_copy(dref[1, 0, 0, r], slot_, 1, r).start()
            return carry
        lax.fori_loop(0, tm, body, 0, unroll=8)

    def wait(slot_):
        def body(r, carry):
            row_copy(0, slot_, 0, r).wait()
            row_copy(0, slot_, 1, r).wait()
            return carry
        lax.fori_loop(0, tm, body, 0, unroll=8)

    @pl.when(i == 0)
    def _():
        start(d_ref, 0)

    @pl.when(i + 1 < n_i)
    def _():
        start(dn_ref, 1 - slot)

    wait(slot)
    route = route_ref[...]
    col = lax.broadcasted_iota(jnp.int32, route.shape, 1)
    w1 = jnp.sum(jnp.where(col == _R_W1, route, 0.0), axis=-1, keepdims=True)
    w2 = jnp.sum(jnp.where(col == _R_W2, route, 0.0), axis=-1, keepdims=True)
    moe = w1 * o_ref[slot, 0] + w2 * o_ref[slot, 1]
    y_ref[...] = _layer_norm(alpha * h_ref[...] + moe, g_ref[...], b_ref[...])


def _combine(h, route, rows, dest, p, alpha, tm):
    n, d = h.shape
    assert n % tm == 0
    n_i = n // tm
    dest4 = dest.reshape(2, n_i, 1, tm)
    cur = pl.BlockSpec((2, 1, 1, tm), lambda i: (0, i, 0, 0), memory_space=pltpu.SMEM)
    nxt = pl.BlockSpec((2, 1, 1, tm), lambda i: (0, jnp.minimum(i + 1, n_i - 1), 0, 0),
                       memory_space=pltpu.SMEM)
    return pl.pallas_call(
        functools.partial(_combine_kernel, alpha),
        grid=(n_i,),
        in_specs=[cur, nxt,
                  pl.BlockSpec((tm, d), lambda i: (i, 0)),
                  pl.BlockSpec((tm, LANES), lambda i: (i, 0)),
                  pl.BlockSpec(memory_space=pl.ANY),
                  pl.BlockSpec((1, d), lambda i: (0, 0)),
                  pl.BlockSpec((1, d), lambda i: (0, 0))],
        out_specs=pl.BlockSpec((tm, d), lambda i: (i, 0)),
        out_shape=jax.ShapeDtypeStruct((n, d), F32),
        scratch_shapes=[pltpu.VMEM((2, 2, tm, d), F32), pltpu.SemaphoreType.DMA((2,))],
        compiler_params=pltpu.CompilerParams(
            dimension_semantics=("arbitrary",), vmem_limit_bytes=VMEM_LIMIT),
        name="moe_combine",
    )(dest4, dest4, h, route, rows, p["ln2_g"], p["ln2_b"])


def _pack_layer(w_in, b_in, w_dw, b_dw, lnc_g, lnc_b, w_conv_out, w_short, a_log, dt_bias, o_norm_g,
                w_o, w_out, ln1_g, ln1_b, w_rg, b_rg, w_re, b_re, w_gate, w_up, w_down, ln2_g, ln2_b):
    d = w_in.shape[0]
    c_conv = w_dw.shape[1]
    qkv_dim = w_short.shape[1]
    o_qkv = 2 * c_conv
    o_ba = o_qkv + qkv_dim
    o_tail = o_ba + 2 * N_HEADS
    half = _TN // 2

    def interleave(a, b):
        r = a.shape[0]
        a = a.reshape(r, -1, 1, half)
        b = b.reshape(r, -1, 1, half)
        return jnp.concatenate([a, b], axis=2).reshape(r, -1)

    def pad_cols(a, width):
        return jnp.pad(a, ((0, 0), (0, width - a.shape[1])))

    b2 = b_in[None, :].astype(F32)
    nh = N_HEADS
    zeros_h = jnp.zeros((nh,), F32)
    head_params = jnp.stack([jnp.concatenate([zeros_h, a_log.astype(F32)]),
                             jnp.concatenate([zeros_h, dt_bias.astype(F32)])])
    w_ba = w_in[:, o_ba:o_tail]
    w_router = jnp.concatenate([w_rg, w_re], axis=1).astype(F32)
    b_router = jnp.concatenate([b_rg, b_re])[None, :].astype(F32)
    return dict(
        w_glu=interleave(w_in[:, :c_conv], w_in[:, c_conv:o_qkv]).astype(BF16),
        b_glu=interleave(b2[:, :c_conv], b2[:, c_conv:o_qkv]),
        w_qkv=w_in[:, o_qkv:o_ba].astype(BF16),
        b_qkv=b2[:, o_qkv:o_ba],
        w_ba=pad_cols(w_ba, LANES).astype(BF16),
        b_ba=pad_cols(b2[:, o_ba:o_tail], LANES),
        w_bat=w_ba.T.astype(BF16),
        b_bat=jnp.broadcast_to(b_in[o_ba:o_tail, None].astype(F32), (2 * nh, LANES)),
        w_tail=w_in[:, o_tail:].astype(BF16),
        b_tail=b2[:, o_tail:],
        p_row=pad_cols(head_params, LANES),
        p_col=pad_cols(head_params.T, LANES),
        w_dw=w_dw.astype(F32), b_dw=b_dw[None, :].astype(F32),
        lnc_g=lnc_g[None, :].astype(F32), lnc_b=lnc_b[None, :].astype(F32),
        w_conv_out=w_conv_out.astype(BF16),
        w_short=w_short.astype(F32),
        o_norm_g=o_norm_g[None, :].astype(F32),
        w_o=w_o.astype(BF16), w_out=w_out.astype(BF16),
        ln1_g=ln1_g[None, :].astype(F32), ln1_b=ln1_b[None, :].astype(F32),
        w_router=pad_cols(w_router, LANES), b_router=pad_cols(b_router, LANES),
        w_gate=w_gate.astype(F32), w_up=w_up.astype(F32), w_down=w_down.astype(F32),
        ln2_g=ln2_g[None, :].astype(F32), ln2_b=ln2_b[None, :].astype(F32),
    )


_MERGE_TM = 256
_EXPERT_TE = 256
_DISPATCH_TM = 256


def _block(x, conv_buf, short_buf, s0, p, alpha, tm, conv_tt):
    b, t, d = x.shape
    n = b * t
    x2 = x.reshape(n, d).astype(F32)
    u, qkv, bg, bgt, gsil, sgc, sgd = _inproj(x2, p, tm)
    c_conv = u.shape[1]
    u3 = u.reshape(b, t, c_conv)
    qkv3 = qkv.reshape(b, t, -1)
    convg = _conv_branch(u3, conv_buf, sgc.reshape(b, t, d), p, conv_tt)
    og, s_new = _delta_branch(qkv3, short_buf, bg, bgt, gsil.reshape(b, t, -1), s0, p)
    h, route, route_t, cnt = _merge(convg.reshape(n, d), og.reshape(n, -1), sgd, x2, p, alpha, _MERGE_TM)
    dest, items = _route_plan(route_t, cnt, _EXPERT_TE)
    xs = _dispatch(h, dest, _DISPATCH_TM)
    rows = _experts(xs, items, p, _EXPERT_TE)
    y = _combine(h, route, rows, dest, p, alpha, _MERGE_TM)
    kc = conv_buf.shape[1]
    ks = short_buf.shape[1]
    assert t >= kc and t >= ks
    return (y.reshape(b, t, d).astype(x.dtype), u3[:, t - kc:].astype(x.dtype),
            qkv3[:, t - ks:].astype(x.dtype), s_new.astype(s0.dtype))


def kernel(x_prompt, x_sample, cache_conv, cache_short, state_delta, w_in, b_in, w_dw, b_dw, lnc_g, lnc_b, w_conv_out, w_short, a_log, dt_bias, o_norm_g, w_o, w_out, ln1_g, ln1_b, w_rg, b_rg, w_re, b_re, w_gate, w_up, w_down, ln2_g, ln2_b):
    weights = (w_in, b_in, w_dw, b_dw, lnc_g, lnc_b, w_conv_out, w_short, a_log, dt_bias, o_norm_g,
               w_o, w_out, ln1_g, ln1_b, w_rg, b_rg, w_re, b_re, w_gate, w_up, w_down, ln2_g, ln2_b)
    depth = w_in.shape[0]
    alpha = (2.0 * depth) ** 0.25
    yp, ys = x_prompt, x_sample
    bp = x_prompt.shape[0]
    outs = [[] for _ in range(6)]
    for l in range(depth):
        p = _pack_layer(*(wt[l] for wt in weights))
        zc = jnp.zeros((bp,) + cache_conv.shape[2:], x_prompt.dtype)
        zs = jnp.zeros((bp,) + cache_short.shape[2:], x_prompt.dtype)
        zd = jnp.zeros((bp,) + state_delta.shape[2:], state_delta.dtype)
        yp, c, s, dl = _block(yp, zc, zs, zd, p, alpha, 1024, 256)
        outs[0].append(c), outs[1].append(s), outs[2].append(dl)
        ys, c, s, dl = _block(ys, cache_conv[l], cache_short[l], state_delta[l], p, alpha, 1024, 64)
        outs[3].append(c), outs[4].append(s), outs[5].append(dl)
    return (yp, ys) + tuple(jnp.stack(o) for o in outs)
```

```python
import functools

import jax
import jax.numpy as jnp
from jax import lax
from jax.experimental import pallas as pl
from jax.experimental.pallas import tpu as pltpu

F32 = jnp.float32
BF16 = jnp.bfloat16

CHUNK = 64
N_HEADS = 8
HEAD_DIM = 128
N_GROUPS = 4
EXP_PER_GROUP = 8
N_EXPERTS = N_GROUPS * EXP_PER_GROUP
LN_EPS = 1e-5
NORM_EPS = 1e-6
LANES = 128
ROUTE_COL0 = N_GROUPS
VMEM_LIMIT = 56 * 1024 * 1024


def _dot(a, b):
    return jnp.dot(a, b, preferred_element_type=F32)


def _dot_nt(a, b):
    return lax.dot_general(a, b, (((1,), (1,)), ((), ())), preferred_element_type=F32)


def _dot_tn(a, b):
    return lax.dot_general(a, b, (((0,), (0,)), ((), ())), preferred_element_type=F32)


def _split3(x):
    hi = x.astype(BF16)
    r1 = x - hi.astype(F32)
    mid = r1.astype(BF16)
    lo = (r1 - mid.astype(F32)).astype(BF16)
    return hi, mid, lo


def _sigmoid(x):
    return 1.0 / (1.0 + jnp.exp(-x))


def _silu(x):
    return x * _sigmoid(x)


def _softplus(x):
    return jnp.maximum(x, 0.0) + jnp.log1p(jnp.exp(-jnp.abs(x)))


def _layer_norm(x, g, b):
    mu = jnp.mean(x, axis=-1, keepdims=True)
    xc = x - mu
    var = jnp.mean(xc * xc, axis=-1, keepdims=True)
    return xc * lax.rsqrt(var + LN_EPS) * g + b


def _clamp(v, lo, hi):
    return jnp.minimum(jnp.maximum(v, lo), hi)


_TN = 512
_J_GLU, _J_QKV, _J_BA, _J_GO, _J_GC, _J_GD, _J_END = 0, 4, 10, 11, 13, 17, 21


def _inproj_kernel(x_ref, wglu_ref, wqkv_ref, wba_ref, wbat_ref, wtail_ref,
                   bglu_ref, bqkv_ref, bba_ref, bbat_ref, btail_ref, prow_ref, pcol_ref,
                   u_ref, qkv_ref, bg_ref, bgt_ref, gsil_ref, sgc_ref, sgd_ref, xb_ref):
    j = pl.program_id(1)

    @pl.when(j == 0)
    def _():
        xb_ref[...] = x_ref[...].astype(BF16)

    @pl.when(j < _J_QKV)
    def _():
        z = _dot(xb_ref[...], wglu_ref[...]) + bglu_ref[...]
        half = _TN // 2
        u_ref[...] = z[:, :half] * _sigmoid(z[:, half:])

    @pl.when((j >= _J_QKV) & (j < _J_BA))
    def _():
        qkv_ref[...] = _dot(xb_ref[...], wqkv_ref[...]) + bqkv_ref[...]

    @pl.when(j == _J_BA)
    def _():
        xb = xb_ref[...]
        z = _dot(xb, wba_ref[...]) + bba_ref[...]
        col = lax.broadcasted_iota(jnp.int32, z.shape, 1)
        g = -jnp.exp(prow_ref[0:1, :]) * _softplus(z + prow_ref[1:2, :])
        bg_ref[...] = jnp.where(col < N_HEADS, _sigmoid(z), g)
        zt = _dot_nt(wbat_ref[...], xb) + bbat_ref[:, 0:1]
        row = lax.broadcasted_iota(jnp.int32, zt.shape, 0)
        gt = -jnp.exp(pcol_ref[:, 0:1]) * _softplus(zt + pcol_ref[:, 1:2])
        bgt_ref[...] = jnp.where(row < N_HEADS, _sigmoid(zt), gt)

    @pl.when((j >= _J_GO) & (j < _J_GC))
    def _():
        z = _dot(xb_ref[...], wtail_ref[...]) + btail_ref[...]
        gsil_ref[...] = _silu(z).astype(BF16)

    @pl.when((j >= _J_GC) & (j < _J_GD))
    def _():
        z = _dot(xb_ref[...], wtail_ref[...]) + btail_ref[...]
        sgc_ref[...] = _sigmoid(z).astype(BF16)

    @pl.when(j >= _J_GD)
    def _():
        z = _dot(xb_ref[...], wtail_ref[...]) + btail_ref[...]
        sgd_ref[...] = _sigmoid(z).astype(BF16)


def _inproj(x, pk, tm):
    n, d = x.shape
    c_conv = pk["w_glu"].shape[1] // 2
    qkv_dim = pk["w_qkv"].shape[1]
    val_dim = N_HEADS * HEAD_DIM
    assert n % tm == 0

    def cm(lo, hi):
        return lambda i, j: (0, _clamp(j - lo, 0, hi - lo - 1))

    def om(lo, hi):
        return lambda i, j: (i, _clamp(j - lo, 0, hi - lo - 1))

    in_specs = [
        pl.BlockSpec((tm, d), lambda i, j: (i, 0)),
        pl.BlockSpec((d, _TN), cm(_J_GLU, _J_QKV)),
        pl.BlockSpec((d, _TN), cm(_J_QKV, _J_BA)),
        pl.BlockSpec((d, LANES), lambda i, j: (0, 0)),
        pl.BlockSpec((2 * N_HEADS, d), lambda i, j: (0, 0)),
        pl.BlockSpec((d, _TN), cm(_J_GO, _J_END)),
        pl.BlockSpec((1, _TN), cm(_J_GLU, _J_QKV)),
        pl.BlockSpec((1, _TN), cm(_J_QKV, _J_BA)),
        pl.BlockSpec((1, LANES), lambda i, j: (0, 0)),
        pl.BlockSpec((2 * N_HEADS, LANES), lambda i, j: (0, 0)),
        pl.BlockSpec((1, _TN), cm(_J_GO, _J_END)),
        pl.BlockSpec((2, LANES), lambda i, j: (0, 0)),
        pl.BlockSpec((2 * N_HEADS, LANES), lambda i, j: (0, 0)),
    ]
    out_shape = [
        jax.ShapeDtypeStruct((n, c_conv), F32),
        jax.ShapeDtypeStruct((n, qkv_dim), F32),
        jax.ShapeDtypeStruct((n, LANES), F32),
        jax.ShapeDtypeStruct((2 * N_HEADS, n), F32),
        jax.ShapeDtypeStruct((n, val_dim), BF16),
        jax.ShapeDtypeStruct((n, d), BF16),
        jax.ShapeDtypeStruct((n, d), BF16),
    ]
    out_specs = [
        pl.BlockSpec((tm, _TN // 2), om(_J_GLU, _J_QKV)),
        pl.BlockSpec((tm, _TN), om(_J_QKV, _J_BA)),
        pl.BlockSpec((tm, LANES), lambda i, j: (i, 0)),
        pl.BlockSpec((2 * N_HEADS, tm), lambda i, j: (0, i)),
        pl.BlockSpec((tm, _TN), om(_J_GO, _J_GC)),
        pl.BlockSpec((tm, _TN), om(_J_GC, _J_GD)),
        pl.BlockSpec((tm, _TN), om(_J_GD, _J_END)),
    ]
    return pl.pallas_call(
        _inproj_kernel,
        grid=(n // tm, _J_END),
        in_specs=in_specs,
        out_specs=out_specs,
        out_shape=out_shape,
        scratch_shapes=[pltpu.VMEM((tm, d), BF16)],
        compiler_params=pltpu.CompilerParams(
            dimension_semantics=("arbitrary", "arbitrary"), vmem_limit_bytes=VMEM_LIMIT),
        name="inproj",
    )(x, pk["w_glu"], pk["w_qkv"], pk["w_ba"], pk["w_bat"], pk["w_tail"],
      pk["b_glu"], pk["b_qkv"], pk["b_ba"], pk["b_bat"], pk["b_tail"], pk["p_row"], pk["p_col"])


_HALO = 32
_CONV_RB = 64
_CONV_FB = 8


def _conv_kernel(u_ref, cache_ref, wdw_ref, bdw_ref, lng_ref, lnb_ref, wco_ref, sgc_ref,
                 out_ref, xt_ref, yt_ref, cn_ref):
    t = pl.program_id(1)
    tt = u_ref.shape[1]
    c_conv = u_ref.shape[2]
    width = wdw_ref.shape[0]
    nfold = c_conv // LANES
    first = _HALO - (width - 1)

    @pl.when(t == 0)
    def _():
        for s in range(nfold):
            xt_ref[pl.ds(s, _HALO, stride=nfold), :] = cache_ref[0, :, s * LANES:(s + 1) * LANES]

    for s in range(nfold):
        xt_ref[pl.ds(_HALO * nfold + s, tt, stride=nfold), :] = u_ref[0, :, s * LANES:(s + 1) * LANES]

    def frames(ib, carry):
        f0 = ib * _CONV_FB
        acc = [None] * _CONV_FB
        for k in range(width):
            wk = wdw_ref[k]
            for j in range(_CONV_FB):
                row = pl.multiple_of((first + f0 + j + k) * nfold, nfold)
                term = wk * xt_ref[pl.ds(row, nfold), :]
                acc[j] = term if acc[j] is None else acc[j] + term
        for j in range(_CONV_FB):
            yt_ref[pl.ds(pl.multiple_of((f0 + j) * nfold, nfold), nfold), :] = acc[j]
        return carry

    lax.fori_loop(0, tt // _CONV_FB, frames, 0)

    for rb in range(tt // _CONV_RB):
        r0 = rb * _CONV_RB
        y = jnp.concatenate([yt_ref[pl.ds(r0 * nfold + s, _CONV_RB, stride=nfold), :] for s in range(nfold)],
                            axis=1)
        y = _layer_norm(y + bdw_ref[...], lng_ref[...], lnb_ref[...])
        cn_ref[r0:r0 + _CONV_RB, :] = _silu(y).astype(BF16)

    co = _dot(cn_ref[...], wco_ref[...])
    out_ref[0] = (co * sgc_ref[0].astype(F32)).astype(BF16)
    xt_ref[0:_HALO * nfold, :] = xt_ref[tt * nfold:(tt + _HALO) * nfold, :]


def _conv_branch(u, cache, sgc, p, tt):
    b, t, c_conv = u.shape
    d = sgc.shape[-1]
    width = p["w_dw"].shape[0]
    assert t % tt == 0 and tt % _CONV_RB == 0 and tt >= _HALO and width - 1 <= _HALO
    assert c_conv % (8 * LANES) == 0
    nfold = c_conv // LANES
    cache_p = jnp.pad(cache.astype(F32), ((0, 0), (_HALO - (width - 1), 0), (0, 0)))
    full2 = lambda shape: pl.BlockSpec(shape, lambda i, j: (0, 0))
    return pl.pallas_call(
        _conv_kernel,
        grid=(b, t // tt),
        in_specs=[
            pl.BlockSpec((1, tt, c_conv), lambda i, j: (i, j, 0)),
            pl.BlockSpec((1, _HALO, c_conv), lambda i, j: (i, 0, 0)),
            pl.BlockSpec((width, nfold, LANES), lambda i, j: (0, 0, 0)),
            full2((1, c_conv)), full2((1, c_conv)), full2((1, c_conv)),
            full2((c_conv, d)),
            pl.BlockSpec((1, tt, d), lambda i, j: (i, j, 0)),
        ],
        out_specs=pl.BlockSpec((1, tt, d), lambda i, j: (i, j, 0)),
        out_shape=jax.ShapeDtypeStruct((b, t, d), BF16),
        scratch_shapes=[pltpu.VMEM(((tt + _HALO) * nfold, LANES), F32),
                        pltpu.VMEM((tt * nfold, LANES), F32),
                        pltpu.VMEM((tt, c_conv), BF16)],
        compiler_params=pltpu.CompilerParams(
            dimension_semantics=("arbitrary", "arbitrary"), vmem_limit_bytes=VMEM_LIMIT),
        name="conv_branch",
    )(u, cache_p, p["w_dw"].reshape(width, nfold, LANES), p["b_dw"], p["lnc_g"], p["lnc_b"],
      p["w_conv_out"], sgc)


_SHORT_PAD = 8
_DELTA_CPS = 4


def _delta_kernel(n_steps, cps, qkv_ref, cache_ref, wsh_ref, bg_ref, bgt_ref, gsil_ref, ong_ref, s0_ref,
                  o_ref, sfin_ref, xb_ref, s_ref):
    c = pl.program_id(1)
    ck = CHUNK
    rows = cps * ck
    key_dim = N_HEADS * HEAD_DIM
    sw = wsh_ref.shape[0]

    @pl.when(c == 0)
    def _():
        s_ref[...] = s0_ref[0]
        xb_ref[0:_SHORT_PAD, :] = cache_ref[0]

    xb_ref[_SHORT_PAD:_SHORT_PAD + rows, :] = qkv_ref[0]

    def conv_cols(cc, lo):
        lanes = slice(lo, lo + HEAD_DIM)
        base = _SHORT_PAD + cc * ck
        acc = wsh_ref[sw - 1:sw, lanes] * xb_ref[base:base + ck, lanes]
        for k in range(sw - 1):
            r = base - (sw - 1) + k
            acc = acc + wsh_ref[k:k + 1, lanes] * xb_ref[r:r + ck, lanes]
        return _silu(acc)

    rowp = lax.broadcasted_iota(jnp.int32, (ck, 2 * ck), 0)
    lanep = lax.broadcasted_iota(jnp.int32, (ck, 2 * ck), 1)
    odd = lanep >= ck
    lcol = jnp.where(odd, lanep - ck, lanep)
    incl = rowp >= lcol
    strict = rowp > lcol
    eye = jnp.where(rowp == lcol, 1.0, 0.0).astype(F32)
    bd8 = (rowp // 8) == (lcol // 8)
    lvl = [((rowp // (2 * s)) == (lcol // (2 * s))) & ((rowp // s) != (lcol // s)) for s in (8, 16, 32)]
    ri = lax.broadcasted_iota(jnp.int32, (ck, ck), 0)
    ci = lax.broadcasted_iota(jnp.int32, (ck, ck), 1)
    tri_l = jnp.where(ri >= ci, 1.0, 0.0).astype(BF16)
    r2 = lax.broadcasted_iota(jnp.int32, (2 * ck, 2 * ck), 0)
    c2 = lax.broadcasted_iota(jnp.int32, (2 * ck, 2 * ck), 1)
    tri_u2 = jnp.where(((r2 >= ck) == (c2 >= ck)) & (r2 <= c2), 1.0, 0.0).astype(BF16)

    bf = lambda m: m.astype(BF16)

    def block_diag(pm):
        return bf(jnp.concatenate([jnp.where(odd, 0.0, pm), jnp.where(odd, pm, 0.0)], axis=0))

    def block_rows(top, bottom):
        z = jnp.zeros_like(top)
        return jnp.concatenate([jnp.concatenate([top, z], axis=1), jnp.concatenate([z, bottom], axis=1)], axis=0)

    bg = [bg_ref[0, cc * ck:(cc + 1) * ck, :] for cc in range(cps)]
    gc_cols = [sum(_dot(tri_l, part) for part in _split3(bg[cc])) for cc in range(cps)]
    gc_rows = [sum(_dot(part, tri_u2) for part in _split3(bgt_ref[cc])) for cc in range(cps)]

    nh2 = N_HEADS // 2
    heads = range(cps * N_HEADS)
    pairs = range(cps * nh2)
    hcc = lambda i: (i // N_HEADS, i % N_HEADS)
    ev = lambda j: (j // nh2) * N_HEADS + 2 * (j % nh2)
    q = [conv_cols(hcc(i)[0], hcc(i)[1] * HEAD_DIM) for i in heads]
    k = [conv_cols(hcc(i)[0], key_dim + hcc(i)[1] * HEAD_DIM) for i in heads]
    v = [conv_cols(hcc(i)[0], 2 * key_dim + hcc(i)[1] * HEAD_DIM) for i in heads]
    q = [x * lax.rsqrt(jnp.sum(x * x, axis=-1, keepdims=True) + NORM_EPS) * (HEAD_DIM ** -0.5) for x in q]
    k = [x * lax.rsqrt(jnp.sum(x * x, axis=-1, keepdims=True) + NORM_EPS) for x in k]
    beta = [bg[hcc(i)[0]][:, hcc(i)[1]:hcc(i)[1] + 1] for i in heads]
    gcc = [gc_cols[hcc(i)[0]][:, N_HEADS + hcc(i)[1]:N_HEADS + hcc(i)[1] + 1] for i in heads]
    kb = [k[i] * beta[i] for i in heads]
    kbf = [bf(x) for x in k]
    kk = [block_rows(kbf[ev(j)], kbf[ev(j) + 1]) for j in pairs]
    gcc2 = [jnp.where(odd, gcc[ev(j) + 1], gcc[ev(j)]) for j in pairs]
    gcr2 = [gc_rows[j // nh2][nh2 + j % nh2:nh2 + j % nh2 + 1, :] for j in pairs]
    decay = [jnp.exp(jnp.where(incl, gcc2[j] - gcr2[j], -jnp.inf)) for j in pairs]
    a = [jnp.where(strict,
                   _dot_nt(jnp.concatenate([bf(kb[ev(j)]), bf(kb[ev(j) + 1])], axis=1), kk[j]) * decay[j], 0.0)
         for j in pairs]
    qk = [_dot_nt(jnp.concatenate([bf(q[ev(j)]), bf(q[ev(j) + 1])], axis=1), kk[j]) * decay[j] for j in pairs]

    ad = [jnp.where(bd8, x, 0.0) for x in a]
    adb = [bf(x) for x in ad]
    a2 = [_dot(adb[j], block_diag(ad[j])) for j in pairs]
    a2d = [block_diag(x) for x in a2]
    a3 = [_dot(adb[j], a2d[j]) for j in pairs]
    a4 = [_dot(bf(a2[j]), a2d[j]) for j in pairs]
    t = [eye - ad[j] + a2[j] - a3[j] for j in pairs]
    t = [t[j] + _dot(bf(t[j]), block_diag(a4[j])) for j in pairs]
    for m in lvl:
        x = [_dot(bf(jnp.where(m, a[j], 0.0)), block_diag(t[j])) for j in pairs]
        t = [t[j] - _dot(bf(t[j]), block_diag(x[j])) for j in pairs]

    egc = [jnp.exp(x) for x in gcc]
    rhs = [bf(jnp.concatenate([v[i] * beta[i], kb[i] * egc[i]], axis=1)) for i in heads]
    sol2 = [_dot(bf(t[j]), block_rows(rhs[ev(j)], rhs[ev(j) + 1])) for j in pairs]
    sol = [sol2[(i // N_HEADS) * nh2 + (i % N_HEADS) // 2][:, (i % 2) * 2 * HEAD_DIM:(i % 2 + 1) * 2 * HEAD_DIM]
           for i in heads]
    g_last = [x[ck - 1:ck, :] for x in gcc]
    k_dec = [bf(k[i] * jnp.exp(g_last[i] - gcc[i])) for i in heads]
    wq_lhs = [bf(jnp.concatenate([sol[i][:, HEAD_DIM:], q[i] * egc[i]], axis=0)) for i in heads]
    qkb = [bf(x) for x in qk]

    s = [s_ref[h] for h in range(N_HEADS)]
    for cc in range(cps):
        hs = range(N_HEADS)
        u0 = cc * N_HEADS
        sb = [bf(x) for x in s]
        wq = [_dot(wq_lhs[u0 + h], sb[h]) for h in hs]
        vb = [bf(sol[u0 + h][:, :HEAD_DIM] - wq[h][:ck]) for h in hs]
        o2 = [_dot(qkb[cc * nh2 + p], block_rows(vb[2 * p], vb[2 * p + 1])) for p in range(nh2)]
        s = [s[h] * jnp.exp(g_last[u0 + h]) + _dot_tn(k_dec[u0 + h], vb[h]) for h in hs]
        for h in hs:
            o = wq[h][ck:] + o2[h // 2][:, (h % 2) * HEAD_DIM:(h % 2 + 1) * HEAD_DIM]
            on = o * lax.rsqrt(jnp.mean(o * o, axis=-1, keepdims=True) + NORM_EPS) * ong_ref[...]
            lanes = slice(h * HEAD_DIM, (h + 1) * HEAD_DIM)
            o_ref[0, cc * ck:(cc + 1) * ck, lanes] = (
                on * gsil_ref[0, cc * ck:(cc + 1) * ck, lanes].astype(F32)).astype(BF16)
    for h in range(N_HEADS):
        s_ref[h] = s[h]

    xb_ref[0:_SHORT_PAD, :] = xb_ref[rows:rows + _SHORT_PAD, :]

    @pl.when(c == n_steps - 1)
    def _():
        sfin_ref[0] = s_ref[...]


def _delta_branch(qkv, cache, bg, bgt, gsil, s0, p):
    b, t, qkv_dim = qkv.shape
    assert t % CHUNK == 0
    nc = t // CHUNK
    cps = _DELTA_CPS if nc % _DELTA_CPS == 0 else 1
    n_steps = nc // cps
    rows = cps * CHUNK
    sw = p["w_short"].shape[0]
    cache_p = jnp.pad(cache.astype(F32), ((0, 0), (_SHORT_PAD - (sw - 1), 0), (0, 0)))
    val_dim = N_HEADS * HEAD_DIM
    bgt3 = bgt.reshape(2 * N_HEADS, b * nc, CHUNK).transpose(1, 0, 2).reshape(b * nc, N_HEADS, 2 * CHUNK)
    return pl.pallas_call(
        functools.partial(_delta_kernel, n_steps, cps),
        grid=(b, n_steps),
        in_specs=[
            pl.BlockSpec((1, rows, qkv_dim), lambda i, j: (i, j, 0)),
            pl.BlockSpec((1, _SHORT_PAD, qkv_dim), lambda i, j: (i, 0, 0)),
            pl.BlockSpec((sw, qkv_dim), lambda i, j: (0, 0)),
            pl.BlockSpec((1, rows, LANES), lambda i, j: (i, j, 0)),
            pl.BlockSpec((cps, N_HEADS, 2 * CHUNK), lambda i, j: (i * n_steps + j, 0, 0)),
            pl.BlockSpec((1, rows, val_dim), lambda i, j: (i, j, 0)),
            pl.BlockSpec((1, HEAD_DIM), lambda i, j: (0, 0)),
            pl.BlockSpec((1, N_HEADS, HEAD_DIM, HEAD_DIM), lambda i, j: (i, 0, 0, 0)),
        ],
        out_specs=[
            pl.BlockSpec((1, rows, val_dim), lambda i, j: (i, j, 0)),
            pl.BlockSpec((1, N_HEADS, HEAD_DIM, HEAD_DIM), lambda i, j: (i, 0, 0, 0)),
        ],
        out_shape=[
            jax.ShapeDtypeStruct((b, t, val_dim), BF16),
            jax.ShapeDtypeStruct((b, N_HEADS, HEAD_DIM, HEAD_DIM), F32),
        ],
        scratch_shapes=[pltpu.VMEM((_SHORT_PAD + rows, qkv_dim), F32),
                        pltpu.VMEM((N_HEADS, HEAD_DIM, HEAD_DIM), F32)],
        compiler_params=pltpu.CompilerParams(
            dimension_semantics=("arbitrary", "arbitrary"), vmem_limit_bytes=VMEM_LIMIT),
        name="delta_rule",
    )(qkv, cache_p, p["w_short"], bg.reshape(b, t, LANES), bgt3, gsil, p["o_norm_g"], s0.astype(F32))


_R_E1, _R_E2, _R_RANK1, _R_RANK2, _R_W1, _R_W2 = range(6)


def _merge_kernel(alpha, convg_ref, og_ref, sgd_ref, x_ref, wo_ref, wout_ref, g_ref, b_ref,
                  wr_ref, br_ref, h_ref, route_ref, routet_ref, cnt_ref, carry_ref):
    @pl.when(pl.program_id(0) == 0)
    def _():
        carry_ref[...] = jnp.zeros_like(carry_ref)

    d_out = _dot(og_ref[...], wo_ref[...])
    merged = convg_ref[...].astype(F32) + d_out * sgd_ref[...].astype(F32)
    mix = _dot(merged.astype(BF16), wout_ref[...])
    h = _layer_norm(alpha * x_ref[...] + mix, g_ref[...], b_ref[...])
    h_ref[...] = h

    h_hi, h_mid, _ = _split3(h)
    w_hi, w_mid, _ = _split3(wr_ref[...])
    logits = _dot(h_hi, w_hi) + _dot(h_mid, w_hi) + _dot(h_hi, w_mid) + br_ref[...]
    tm = logits.shape[0]
    col = lax.broadcasted_iota(jnp.int32, logits.shape, 1).astype(F32)
    big = float(LANES)
    is_g = col < N_GROUPS
    mg = jnp.max(jnp.where(is_g, logits, -jnp.inf), axis=-1, keepdims=True)
    sg = jnp.sum(jnp.where(is_g, jnp.exp(jnp.where(is_g, logits, mg) - mg), 0.0), axis=-1, keepdims=True)
    pg_top = 1.0 / sg
    gidx = jnp.min(jnp.where(is_g & (logits == mg), col, big), axis=-1, keepdims=True)
    lo = ROUTE_COL0 + EXP_PER_GROUP * gidx
    sel = (col >= lo) & (col < lo + EXP_PER_GROUP)
    le = jnp.where(sel, logits, -jnp.inf)
    m1 = jnp.max(le, axis=-1, keepdims=True)
    i1 = jnp.min(jnp.where(le == m1, col, big), axis=-1, keepdims=True)
    le2 = jnp.where(col == i1, -jnp.inf, le)
    m2 = jnp.max(le2, axis=-1, keepdims=True)
    i2 = jnp.min(jnp.where(le2 == m2, col, big), axis=-1, keepdims=True)
    e2 = jnp.exp(m2 - m1)
    den = 1.0 + e2
    w1 = pg_top / den
    w2 = pg_top * e2 / den

    hit1 = col == i1
    hit2 = col == i2
    member = jnp.where(hit1 | hit2, 1.0, 0.0)
    ri = lax.broadcasted_iota(jnp.int32, (tm, tm), 0)
    ci = lax.broadcasted_iota(jnp.int32, (tm, tm), 1)
    earlier = jnp.where(ri > ci, 1.0, 0.0).astype(BF16)
    before = _dot(earlier, member.astype(BF16)) + carry_ref[...]
    rank1 = jnp.sum(jnp.where(hit1, before, 0.0), axis=-1, keepdims=True)
    rank2 = jnp.sum(jnp.where(hit2, before, 0.0), axis=-1, keepdims=True)
    carry_ref[...] += jnp.sum(member, axis=0, keepdims=True)
    cnt_ref[...] = jnp.broadcast_to(carry_ref[...], cnt_ref.shape)

    fields = (i1 - ROUTE_COL0, i2 - ROUTE_COL0, rank1, rank2, w1, w2)
    route = jnp.zeros_like(logits)
    for c, val in enumerate(fields):
        route = jnp.where(col == float(c), val, route)
    route_ref[...] = route
    sr = lax.broadcasted_iota(jnp.int32, (8, LANES), 0)
    sc = lax.broadcasted_iota(jnp.int32, (8, LANES), 1)
    pick = jnp.where(sr == sc, 1.0, 0.0).astype(BF16)
    routet_ref[...] = sum(_dot_nt(pick, part) for part in _split3(route))


def _merge(convg, og, sgd, x, p, alpha, tm):
    n, d = x.shape
    val_dim = og.shape[1]
    assert n % tm == 0
    row = lambda w: pl.BlockSpec((tm, w), lambda i: (i, 0))
    full = lambda shape: pl.BlockSpec(shape, lambda i: (0, 0))
    return pl.pallas_call(
        functools.partial(_merge_kernel, alpha),
        grid=(n // tm,),
        in_specs=[row(d), row(val_dim), row(d), row(d), full((val_dim, d)), full((d, d)),
                  full((1, d)), full((1, d)), full((d, LANES)), full((1, LANES))],
        out_specs=[row(d), row(LANES), pl.BlockSpec((8, tm), lambda i: (0, i)), full((8, LANES))],
        out_shape=[jax.ShapeDtypeStruct((n, d), F32), jax.ShapeDtypeStruct((n, LANES), F32),
                   jax.ShapeDtypeStruct((8, n), F32), jax.ShapeDtypeStruct((8, LANES), F32)],
        scratch_shapes=[pltpu.VMEM((1, LANES), F32)],
        compiler_params=pltpu.CompilerParams(
            dimension_semantics=("arbitrary",), vmem_limit_bytes=VMEM_LIMIT),
        name="merge_outproj",
    )(convg, og, sgd, x, p["w_o"], p["w_out"], p["ln1_g"], p["ln1_b"], p["w_router"], p["b_router"])


def _route_plan(route_t, cnt, te):
    n = route_t.shape[1]
    i32 = jnp.int32
    ri = route_t[:4].astype(i32)
    counts = cnt[0, ROUTE_COL0:ROUTE_COL0 + N_EXPERTS].astype(i32)
    ends = jnp.cumsum(counts)
    starts = ends - counts
    eids = jnp.arange(N_EXPERTS, dtype=i32)

    def lookup(table, idx):
        return jnp.sum(jnp.where(idx[None, :] == eids[:, None], table[:, None], 0), axis=0)

    dest = jnp.stack([lookup(starts, ri[0]) + ri[2], lookup(starts, ri[1]) + ri[3]])

    first_tile = starts // te
    last_tile = (ends - 1) // te
    items_e = jnp.where(counts > 0, last_tile - first_tile + 1, 0)
    item_end = jnp.cumsum(items_e)
    item_start = item_end - items_e
    total = item_end[-1]
    n_items = (2 * n) // te + N_EXPERTS - 1
    w = jnp.minimum(jnp.arange(n_items, dtype=i32), total - 1)
    item_e = jnp.sum((item_end[:, None] <= w[None, :]).astype(i32), axis=0)
    item_tile = lookup(first_tile, item_e) + w - lookup(item_start, item_e)
    lo = jnp.clip(lookup(starts, item_e) - item_tile * te, 0, te)
    hi = jnp.clip(lookup(ends, item_e) - item_tile * te, 0, te)
    return dest, (item_tile, item_e, lo, hi, total.reshape(1))


def _dispatch_kernel(d_ref, h_ref, xs_hbm, sem):
    tm = d_ref.shape[-1]

    def row_copy(r, dst):
        return pltpu.make_async_copy(h_ref.at[pl.ds(r, 1)], xs_hbm.at[pl.ds(dst, 1)], sem.at[0])

    def start(r, carry):
        row_copy(r, d_ref[0, 0, 0, r]).start()
        row_copy(r, d_ref[1, 0, 0, r]).start()
        return carry

    def wait(r, carry):
        row_copy(r, 0).wait()
        row_copy(r, 0).wait()
        return carry

    lax.fori_loop(0, tm, start, 0, unroll=8)
    lax.fori_loop(0, tm, wait, 0, unroll=8)


def _dispatch(h, dest, tm):
    n, d = h.shape
    assert n % tm == 0
    n_steps = n // tm
    return pl.pallas_call(
        _dispatch_kernel,
        grid=(n_steps,),
        in_specs=[pl.BlockSpec((2, 1, 1, tm), lambda i: (0, i, 0, 0), memory_space=pltpu.SMEM),
                  pl.BlockSpec((tm, d), lambda i: (i, 0))],
        out_specs=pl.BlockSpec(memory_space=pl.ANY),
        out_shape=jax.ShapeDtypeStruct((2 * n, d), F32),
        scratch_shapes=[pltpu.SemaphoreType.DMA((1,))],
        compiler_params=pltpu.CompilerParams(
            dimension_semantics=("arbitrary",), vmem_limit_bytes=VMEM_LIMIT),
        name="moe_dispatch",
    )(dest.reshape(2, n_steps, 1, tm), h)


def _expert_kernel(tile_ref, exp_ref, lo_ref, hi_ref, tot_ref, xs_ref, wg_ref, wu_ref, wd_ref,
                   out_ref, wgb_ref, wub_ref, wdb_ref):
    w = pl.program_id(0)
    prev = jnp.maximum(w - 1, 0)
    live = w < tot_ref[0]
    new_expert = (w == 0) | (exp_ref[w] != exp_ref[prev])
    first_of_tile = (w == 0) | (tile_ref[w] != tile_ref[prev])

    @pl.when(live & new_expert)
    def _():
        wgb_ref[...] = wg_ref[0].astype(BF16)
        wub_ref[...] = wu_ref[0].astype(BF16)
        wdb_ref[...] = wd_ref[0].astype(BF16)

    @pl.when(live)
    def _():
        x = xs_ref[...].astype(BF16)
        hg = _dot(x, wgb_ref[...])
        hu = _dot(x, wub_ref[...])
        row = lax.broadcasted_iota(jnp.int32, (x.shape[0], 1), 0)
        mine = (row >= lo_ref[w]) & (row < hi_ref[w])
        act = jnp.where(mine, _silu(hg) * hu, 0.0).astype(BF16)
        part = _dot(act, wdb_ref[...])

        @pl.when(first_of_tile)
        def _():
            out_ref[...] = part

        @pl.when(jnp.logical_not(first_of_tile))
        def _():
            out_ref[...] += part


def _experts(xs, items, p, te):
    rows, d = xs.shape
    ne, _, f = p["w_gate"].shape
    item_tile, item_e, lo, hi, total = items
    n_items = item_tile.shape[0]
    tile_map = lambda w, t_, e_, lo_, hi_, n_: (t_[w], 0)
    exp_map = lambda w, t_, e_, lo_, hi_, n_: (e_[w], 0, 0)
    grid_spec = pltpu.PrefetchScalarGridSpec(
        num_scalar_prefetch=5,
        grid=(n_items,),
        in_specs=[
            pl.BlockSpec((te, d), tile_map),
            pl.BlockSpec((1, d, f), exp_map),
            pl.BlockSpec((1, d, f), exp_map),
            pl.BlockSpec((1, f, d), exp_map),
        ],
        out_specs=pl.BlockSpec((te, d), tile_map),
        scratch_shapes=[pltpu.VMEM((d, f), BF16), pltpu.VMEM((d, f), BF16), pltpu.VMEM((f, d), BF16)],
    )
    return pl.pallas_call(
        _expert_kernel,
        grid_spec=grid_spec,
        out_shape=jax.ShapeDtypeStruct((rows, d), F32),
        compiler_params=pltpu.CompilerParams(
            dimension_semantics=("arbitrary",), vmem_limit_bytes=VMEM_LIMIT),
        name="moe_experts",
    )(item_tile, item_e, lo, hi, total, xs, p["w_gate"], p["w_up"], p["w_down"])


def _combine_kernel(alpha, d_ref, dn_ref, h_ref, route_ref, rows_hbm, g_ref, b_ref, y_ref, o_ref, sem):
    i = pl.program_id(0)
    n_i = pl.num_programs(0)
    tm = h_ref.shape[0]
    slot = i % 2

    def row_copy(src_row, slot_, k, r):
        return pltpu.make_async_copy(rows_hbm.at[pl.ds(src_row, 1)], o_ref.at[slot_, k, pl.ds(r, 1)],
                                     sem.at[slot_])

    def start(dref, slot_):
        def body(r, carry):
            row_copy(dref[0, 0, 0, r], slot_, 0, r).start()
            row_copy(dref[1, 0, 0, r], slot_, 1, r).start()
            return carry
        lax.fori_loop(0, tm, body, 0, unroll=8)

    def wait(slot_):
        def body(r, carry):
            row_copy(0, slot_, 0, r).wait()
            row_copy(0, slot_, 1, r).wait()
            return carry
        lax.fori_loop(0, tm, body, 0, unroll=8)

    @pl.when(i == 0)
    def _():
        start(d_ref, 0)

    @pl.when(i + 1 < n_i)
    def _():
        start(dn_ref, 1 - slot)

    wait(slot)
    route = route_ref[...]
    col = lax.broadcasted_iota(jnp.int32, route.shape, 1)
    w1 = jnp.sum(jnp.where(col == _R_W1, route, 0.0), axis=-1, keepdims=True)
    w2 = jnp.sum(jnp.where(col == _R_W2, route, 0.0), axis=-1, keepdims=True)
    moe = w1 * o_ref[slot, 0] + w2 * o_ref[slot, 1]
    y_ref[...] = _layer_norm(alpha * h_ref[...] + moe, g_ref[...], b_ref[...])


def _combine(h, route, rows, dest, p, alpha, tm):
    n, d = h.shape
    assert n % tm == 0
    n_i = n // tm
    dest4 = dest.reshape(2, n_i, 1, tm)
    cur = pl.BlockSpec((2, 1, 1, tm), lambda i: (0, i, 0, 0), memory_space=pltpu.SMEM)
    nxt = pl.BlockSpec((2, 1, 1, tm), lambda i: (0, jnp.minimum(i + 1, n_i - 1), 0, 0),
                       memory_space=pltpu.SMEM)
    return pl.pallas_call(
        functools.partial(_combine_kernel, alpha),
        grid=(n_i,),
        in_specs=[cur, nxt,
                  pl.BlockSpec((tm, d), lambda i: (i, 0)),
                  pl.BlockSpec((tm, LANES), lambda i: (i, 0)),
                  pl.BlockSpec(memory_space=pl.ANY),
                  pl.BlockSpec((1, d), lambda i: (0, 0)),
                  pl.BlockSpec((1, d), lambda i: (0, 0))],
        out_specs=pl.BlockSpec((tm, d), lambda i: (i, 0)),
        out_shape=jax.ShapeDtypeStruct((n, d), F32),
        scratch_shapes=[pltpu.VMEM((2, 2, tm, d), F32), pltpu.SemaphoreType.DMA((2,))],
        compiler_params=pltpu.CompilerParams(
            dimension_semantics=("arbitrary",), vmem_limit_bytes=VMEM_LIMIT),
        name="moe_combine",
    )(dest4, dest4, h, route, rows, p["ln2_g"], p["ln2_b"])


def _pack_layer(w_in, b_in, w_dw, b_dw, lnc_g, lnc_b, w_conv_out, w_short, a_log, dt_bias, o_norm_g,
                w_o, w_out, ln1_g, ln1_b, w_rg, b_rg, w_re, b_re, w_gate, w_up, w_down, ln2_g, ln2_b):
    d = w_in.shape[0]
    c_conv = w_dw.shape[1]
    qkv_dim = w_short.shape[1]
    o_qkv = 2 * c_conv
    o_ba = o_qkv + qkv_dim
    o_tail = o_ba + 2 * N_HEADS
    half = _TN // 2

    def interleave(a, b):
        r = a.shape[0]
        a = a.reshape(r, -1, 1, half)
        b = b.reshape(r, -1, 1, half)
        return jnp.concatenate([a, b], axis=2).reshape(r, -1)

    def pad_cols(a, width):
        return jnp.pad(a, ((0, 0), (0, width - a.shape[1])))

    b2 = b_in[None, :].astype(F32)
    nh = N_HEADS
    zeros_h = jnp.zeros((nh,), F32)
    head_params = jnp.stack([jnp.concatenate([zeros_h, a_log.astype(F32)]),
                             jnp.concatenate([zeros_h, dt_bias.astype(F32)])])
    w_ba = w_in[:, o_ba:o_tail]
    w_router = jnp.concatenate([w_rg, w_re], axis=1).astype(F32)
    b_router = jnp.concatenate([b_rg, b_re])[None, :].astype(F32)
    return dict(
        w_glu=interleave(w_in[:, :c_conv], w_in[:, c_conv:o_qkv]).astype(BF16),
        b_glu=interleave(b2[:, :c_conv], b2[:, c_conv:o_qkv]),
        w_qkv=w_in[:, o_qkv:o_ba].astype(BF16),
        b_qkv=b2[:, o_qkv:o_ba],
        w_ba=pad_cols(w_ba, LANES).astype(BF16),
        b_ba=pad_cols(b2[:, o_ba:o_tail], LANES),
        w_bat=w_ba.T.astype(BF16),
        b_bat=jnp.broadcast_to(b_in[o_ba:o_tail, None].astype(F32), (2 * nh, LANES)),
        w_tail=w_in[:, o_tail:].astype(BF16),
        b_tail=b2[:, o_tail:],
        p_row=pad_cols(head_params, LANES),
        p_col=pad_cols(head_params.T, LANES),
        w_dw=w_dw.astype(F32), b_dw=b_dw[None, :].astype(F32),
        lnc_g=lnc_g[None, :].astype(F32), lnc_b=lnc_b[None, :].astype(F32),
        w_conv_out=w_conv_out.astype(BF16),
        w_short=w_short.astype(F32),
        o_norm_g=o_norm_g[None, :].astype(F32),
        w_o=w_o.astype(BF16), w_out=w_out.astype(BF16),
        ln1_g=ln1_g[None, :].astype(F32), ln1_b=ln1_b[None, :].astype(F32),
        w_router=pad_cols(w_router, LANES), b_router=pad_cols(b_router, LANES),
        w_gate=w_gate.astype(F32), w_up=w_up.astype(F32), w_down=w_down.astype(F32),
        ln2_g=ln2_g[None, :].astype(F32), ln2_b=ln2_b[None, :].astype(F32),
    )


_MERGE_TM = 256
_EXPERT_TE = 256
_DISPATCH_TM = 512


def _block(x, conv_buf, short_buf, s0, p, alpha, tm, conv_tt):
    b, t, d = x.shape
    n = b * t
    x2 = x.reshape(n, d).astype(F32)
    u, qkv, bg, bgt, gsil, sgc, sgd = _inproj(x2, p, tm)
    c_conv = u.shape[1]
    u3 = u.reshape(b, t, c_conv)
    qkv3 = qkv.reshape(b, t, -1)
    convg = _conv_branch(u3, conv_buf, sgc.reshape(b, t, d), p, conv_tt)
    og, s_new = _delta_branch(qkv3, short_buf, bg, bgt, gsil.reshape(b, t, -1), s0, p)
    h, route, route_t, cnt = _merge(convg.reshape(n, d), og.reshape(n, -1), sgd, x2, p, alpha, _MERGE_TM)
    dest, items = _route_plan(route_t, cnt, _EXPERT_TE)
    xs = _dispatch(h, dest, min(_DISPATCH_TM, n))
    rows = _experts(xs, items, p, _EXPERT_TE)
    y = _combine(h, route, rows, dest, p, alpha, _MERGE_TM)
    kc = conv_buf.shape[1]
    ks = short_buf.shape[1]
    assert t >= kc and t >= ks
    return (y.reshape(b, t, d).astype(x.dtype), u3[:, t - kc:].astype(x.dtype),
            qkv3[:, t - ks:].astype(x.dtype), s_new.astype(s0.dtype))


def kernel(x_prompt, x_sample, cache_conv, cache_short, state_delta, w_in, b_in, w_dw, b_dw, lnc_g, lnc_b, w_conv_out, w_short, a_log, dt_bias, o_norm_g, w_o, w_out, ln1_g, ln1_b, w_rg, b_rg, w_re, b_re, w_gate, w_up, w_down, ln2_g, ln2_b):
    weights = (w_in, b_in, w_dw, b_dw, lnc_g, lnc_b, w_conv_out, w_short, a_log, dt_bias, o_norm_g,
               w_o, w_out, ln1_g, ln1_b, w_rg, b_rg, w_re, b_re, w_gate, w_up, w_down, ln2_g, ln2_b)
    depth = w_in.shape[0]
    alpha = (2.0 * depth) ** 0.25
    yp, ys = x_prompt, x_sample
    bp = x_prompt.shape[0]
    outs = [[] for _ in range(6)]
    for l in range(depth):
        p = _pack_layer(*(wt[l] for wt in weights))
        zc = jnp.zeros((bp,) + cache_conv.shape[2:], x_prompt.dtype)
        zs = jnp.zeros((bp,) + cache_short.shape[2:], x_prompt.dtype)
        zd = jnp.zeros((bp,) + state_delta.shape[2:], state_delta.dtype)
        yp, c, s, dl = _block(yp, zc, zs, zd, p, alpha, 1024, 256)
        outs[0].append(c), outs[1].append(s), outs[2].append(dl)
        ys, c, s, dl = _block(ys, cache_conv[l], cache_short[l], state_delta[l], p, alpha, 1024, 64)
        outs[3].append(c), outs[4].append(s), outs[5].append(dl)
    return (yp, ys) + tuple(jnp.stack(o) for o in outs)
```

```python
import functools

import jax
import jax.numpy as jnp
from jax import lax
from jax.experimental import pallas as pl
from jax.experimental.pallas import tpu as pltpu

F32 = jnp.float32
BF16 = jnp.bfloat16

CHUNK = 64
N_HEADS = 8
HEAD_DIM = 128
N_GROUPS = 4
EXP_PER_GROUP = 8
N_EXPERTS = N_GROUPS * EXP_PER_GROUP
LN_EPS = 1e-5
NORM_EPS = 1e-6
LANES = 128
ROUTE_COL0 = N_GROUPS
VMEM_LIMIT = 56 * 1024 * 1024


def _dot(a, b):
    return jnp.dot(a, b, preferred_element_type=F32)


def _dot_nt(a, b):
    return lax.dot_general(a, b, (((1,), (1,)), ((), ())), preferred_element_type=F32)


def _dot_tn(a, b):
    return lax.dot_general(a, b, (((0,), (0,)), ((), ())), preferred_element_type=F32)


def _split3(x):
    hi = x.astype(BF16)
    r1 = x - hi.astype(F32)
    mid = r1.astype(BF16)
    lo = (r1 - mid.astype(F32)).astype(BF16)
    return hi, mid, lo


def _sigmoid(x):
    return 1.0 / (1.0 + jnp.exp(-x))


def _silu(x):
    return x * _sigmoid(x)


def _softplus(x):
    return jnp.maximum(x, 0.0) + jnp.log1p(jnp.exp(-jnp.abs(x)))


def _layer_norm(x, g, b):
    mu = jnp.mean(x, axis=-1, keepdims=True)
    xc = x - mu
    var = jnp.mean(xc * xc, axis=-1, keepdims=True)
    return xc * lax.rsqrt(var + LN_EPS) * g + b


def _clamp(v, lo, hi):
    return jnp.minimum(jnp.maximum(v, lo), hi)


_TN = 512
_J_GLU, _J_QKV, _J_BA, _J_GO, _J_GC, _J_GD, _J_END = 0, 4, 10, 11, 13, 17, 21


def _wide_dot(x, w_ref):
    return jnp.concatenate([_dot(x, w_ref[b]) for b in range(w_ref.shape[0])], axis=1)


def _inproj_kernel(x_ref, wga_ref, wgb_ref, wqkv_ref, wba_ref, wbat_ref, wtail_ref,
                   bga_ref, bgb_ref, bqkv_ref, bba_ref, bbat_ref, btail_ref, prow_ref, pcol_ref,
                   u_ref, qkv_ref, bg_ref, bgt_ref, gsil_ref, sgc_ref, sgd_ref, xb_ref):
    j = pl.program_id(1)

    @pl.when(j == 0)
    def _():
        xb_ref[...] = x_ref[...].astype(BF16)

    @pl.when(j < _J_QKV)
    def _():
        xb = xb_ref[...]
        value = _dot(xb, wga_ref[0]) + bga_ref[...]
        gate = _dot(xb, wgb_ref[0]) + bgb_ref[...]
        u_ref[...] = value * _sigmoid(gate)

    @pl.when((j >= _J_QKV) & (j < _J_BA))
    def _():
        qkv_ref[...] = _wide_dot(xb_ref[...], wqkv_ref) + bqkv_ref[...]

    @pl.when(j == _J_BA)
    def _():
        xb = xb_ref[...]
        z = _dot(xb, wba_ref[...]) + bba_ref[...]
        col = lax.broadcasted_iota(jnp.int32, z.shape, 1)
        g = -jnp.exp(prow_ref[0:1, :]) * _softplus(z + prow_ref[1:2, :])
        bg_ref[...] = jnp.where(col < N_HEADS, _sigmoid(z), g)
        zt = _dot_nt(wbat_ref[...], xb) + bbat_ref[:, 0:1]
        row = lax.broadcasted_iota(jnp.int32, zt.shape, 0)
        gt = -jnp.exp(pcol_ref[:, 0:1]) * _softplus(zt + pcol_ref[:, 1:2])
        bgt_ref[...] = jnp.where(row < N_HEADS, _sigmoid(zt), gt)

    @pl.when((j >= _J_GO) & (j < _J_GC))
    def _():
        z = _wide_dot(xb_ref[...], wtail_ref) + btail_ref[...]
        gsil_ref[...] = _silu(z).astype(BF16)

    @pl.when((j >= _J_GC) & (j < _J_GD))
    def _():
        z = _wide_dot(xb_ref[...], wtail_ref) + btail_ref[...]
        sgc_ref[...] = _sigmoid(z).astype(BF16)

    @pl.when(j >= _J_GD)
    def _():
        z = _wide_dot(xb_ref[...], wtail_ref) + btail_ref[...]
        sgd_ref[...] = _sigmoid(z).astype(BF16)


def _inproj(x, pk, tm):
    n, d = x.shape
    c_conv, qkv_dim = pk["c_conv"], pk["qkv_dim"]
    val_dim = N_HEADS * HEAD_DIM
    half = _TN // 2
    n_glu = _J_QKV - _J_GLU
    assert n % tm == 0 and c_conv == n_glu * half and (2 * c_conv) % _TN == 0
    qkv_blk0 = 2 * c_conv // _TN
    tail_blk0 = (2 * c_conv + qkv_dim) // _TN
    assert (2 * c_conv + qkv_dim) % _TN == 0

    def cm(lo, hi):
        return lambda i, j: (0, _clamp(j - lo, 0, hi - lo - 1))

    def om(lo, hi):
        return lambda i, j: (i, _clamp(j - lo, 0, hi - lo - 1))

    in_specs = [
        pl.BlockSpec((tm, d), lambda i, j: (i, 0)),
        pl.BlockSpec((1, d, half), lambda i, j: (_clamp(j, 0, n_glu - 1), 0, 0)),
        pl.BlockSpec((1, d, half), lambda i, j: (n_glu + _clamp(j, 0, n_glu - 1), 0, 0)),
        pl.BlockSpec((2, d, half), lambda i, j: (qkv_blk0 + _clamp(j - _J_QKV, 0, _J_BA - _J_QKV - 1), 0, 0)),
        pl.BlockSpec((d, LANES), lambda i, j: (0, 0)),
        pl.BlockSpec((2 * N_HEADS, d), lambda i, j: (0, 0)),
        pl.BlockSpec((2, d, half), lambda i, j: (tail_blk0 + _clamp(j - _J_GO, 0, _J_END - _J_GO - 1), 0, 0)),
        pl.BlockSpec((1, half), lambda i, j: (0, _clamp(j, 0, n_glu - 1))),
        pl.BlockSpec((1, half), lambda i, j: (0, n_glu + _clamp(j, 0, n_glu - 1))),
        pl.BlockSpec((1, _TN), lambda i, j: (0, qkv_blk0 + _clamp(j - _J_QKV, 0, _J_BA - _J_QKV - 1))),
        pl.BlockSpec((1, LANES), lambda i, j: (0, 0)),
        pl.BlockSpec((2 * N_HEADS, LANES), lambda i, j: (0, 0)),
        pl.BlockSpec((1, _TN), cm(_J_GO, _J_END)),
        pl.BlockSpec((2, LANES), lambda i, j: (0, 0)),
        pl.BlockSpec((2 * N_HEADS, LANES), lambda i, j: (0, 0)),
    ]
    out_shape = [
        jax.ShapeDtypeStruct((n, c_conv), F32),
        jax.ShapeDtypeStruct((n, qkv_dim), F32),
        jax.ShapeDtypeStruct((n, LANES), F32),
        jax.ShapeDtypeStruct((2 * N_HEADS, n), F32),
        jax.ShapeDtypeStruct((n, val_dim), BF16),
        jax.ShapeDtypeStruct((n, d), BF16),
        jax.ShapeDtypeStruct((n, d), BF16),
    ]
    out_specs = [
        pl.BlockSpec((tm, _TN // 2), om(_J_GLU, _J_QKV)),
        pl.BlockSpec((tm, _TN), om(_J_QKV, _J_BA)),
        pl.BlockSpec((tm, LANES), lambda i, j: (i, 0)),
        pl.BlockSpec((2 * N_HEADS, tm), lambda i, j: (0, i)),
        pl.BlockSpec((tm, _TN), om(_J_GO, _J_GC)),
        pl.BlockSpec((tm, _TN), om(_J_GC, _J_GD)),
        pl.BlockSpec((tm, _TN), om(_J_GD, _J_END)),
    ]
    return pl.pallas_call(
        _inproj_kernel,
        grid=(n // tm, _J_END),
        in_specs=in_specs,
        out_specs=out_specs,
        out_shape=out_shape,
        scratch_shapes=[pltpu.VMEM((tm, d), BF16)],
        compiler_params=pltpu.CompilerParams(
            dimension_semantics=("arbitrary", "arbitrary"), vmem_limit_bytes=VMEM_LIMIT),
        name="inproj",
    )(x, pk["w_blk"], pk["w_blk"], pk["w_blk"], pk["w_ba"], pk["w_bat"], pk["w_blk"],
      pk["b_in"], pk["b_in"], pk["b_in"], pk["b_ba"], pk["b_bat"], pk["b_tail"], pk["p_row"], pk["p_col"])


_HALO = 32
_CONV_RB = 64
_CONV_FB = 8


def _conv_kernel(u_ref, cache_ref, wdw_ref, bdw_ref, lng_ref, lnb_ref, wco_ref, sgc_ref,
                 out_ref, xt_ref, yt_ref, cn_ref):
    t = pl.program_id(1)
    tt = u_ref.shape[1]
    c_conv = u_ref.shape[2]
    width = wdw_ref.shape[0]
    nfold = c_conv // LANES
    first = _HALO - (width - 1)

    @pl.when(t == 0)
    def _():
        for s in range(nfold):
            xt_ref[pl.ds(s, _HALO, stride=nfold), :] = cache_ref[0, :, s * LANES:(s + 1) * LANES]

    for s in range(nfold):
        xt_ref[pl.ds(_HALO * nfold + s, tt, stride=nfold), :] = u_ref[0, :, s * LANES:(s + 1) * LANES]

    def frames(ib, carry):
        f0 = ib * _CONV_FB
        acc = [None] * _CONV_FB
        for k in range(width):
            wk = wdw_ref[k]
            for j in range(_CONV_FB):
                row = pl.multiple_of((first + f0 + j + k) * nfold, nfold)
                term = wk * xt_ref[pl.ds(row, nfold), :]
                acc[j] = term if acc[j] is None else acc[j] + term
        for j in range(_CONV_FB):
            yt_ref[pl.ds(pl.multiple_of((f0 + j) * nfold, nfold), nfold), :] = acc[j]
        return carry

    lax.fori_loop(0, tt // _CONV_FB, frames, 0)

    for rb in range(tt // _CONV_RB):
        r0 = rb * _CONV_RB
        y = jnp.concatenate([yt_ref[pl.ds(r0 * nfold + s, _CONV_RB, stride=nfold), :] for s in range(nfold)],
                            axis=1)
        y = _layer_norm(y + bdw_ref[...], lng_ref[...], lnb_ref[...])
        cn_ref[r0:r0 + _CONV_RB, :] = _silu(y).astype(BF16)

    co = _dot(cn_ref[...], wco_ref[...])
    out_ref[0] = (co * sgc_ref[0].astype(F32)).astype(BF16)
    xt_ref[0:_HALO * nfold, :] = xt_ref[tt * nfold:(tt + _HALO) * nfold, :]


def _conv_branch(u, cache, sgc, p, tt):
    b, t, c_conv = u.shape
    d = sgc.shape[-1]
    width = p["w_dw"].shape[0]
    assert t % tt == 0 and tt % _CONV_RB == 0 and tt >= _HALO and width - 1 <= _HALO
    assert c_conv % (8 * LANES) == 0
    nfold = c_conv // LANES
    cache_p = jnp.pad(cache.astype(F32), ((0, 0), (_HALO - (width - 1), 0), (0, 0)))
    full2 = lambda shape: pl.BlockSpec(shape, lambda i, j: (0, 0))
    return pl.pallas_call(
        _conv_kernel,
        grid=(b, t // tt),
        in_specs=[
            pl.BlockSpec((1, tt, c_conv), lambda i, j: (i, j, 0)),
            pl.BlockSpec((1, _HALO, c_conv), lambda i, j: (i, 0, 0)),
            pl.BlockSpec((width, nfold, LANES), lambda i, j: (0, 0, 0)),
            full2((1, c_conv)), full2((1, c_conv)), full2((1, c_conv)),
            full2((c_conv, d)),
            pl.BlockSpec((1, tt, d), lambda i, j: (i, j, 0)),
        ],
        out_specs=pl.BlockSpec((1, tt, d), lambda i, j: (i, j, 0)),
        out_shape=jax.ShapeDtypeStruct((b, t, d), BF16),
        scratch_shapes=[pltpu.VMEM(((tt + _HALO) * nfold, LANES), F32),
                        pltpu.VMEM((tt * nfold, LANES), F32),
                        pltpu.VMEM((tt, c_conv), BF16)],
        compiler_params=pltpu.CompilerParams(
            dimension_semantics=("arbitrary", "arbitrary"), vmem_limit_bytes=VMEM_LIMIT),
        name="conv_branch",
    )(u, cache_p, p["w_dw"].reshape(width, nfold, LANES), p["b_dw"], p["lnc_g"], p["lnc_b"],
      p["w_conv_out"], sgc)


_SHORT_PAD = 8
_DELTA_CPS = 4


def _delta_kernel(n_steps, cps, qkv_ref, cache_ref, wsh_ref, bg_ref, bgt_ref, gsil_ref, ong_ref, s0_ref,
                  o_ref, sfin_ref, xb_ref, s_ref):
    c = pl.program_id(1)
    ck = CHUNK
    rows = cps * ck
    key_dim = N_HEADS * HEAD_DIM
    sw = wsh_ref.shape[0]

    @pl.when(c == 0)
    def _():
        s_ref[...] = s0_ref[0]
        xb_ref[0:_SHORT_PAD, :] = cache_ref[0]

    xb_ref[_SHORT_PAD:_SHORT_PAD + rows, :] = qkv_ref[0]

    def conv_cols(cc, lo):
        lanes = slice(lo, lo + HEAD_DIM)
        base = _SHORT_PAD + cc * ck
        acc = wsh_ref[sw - 1:sw, lanes] * xb_ref[base:base + ck, lanes]
        for k in range(sw - 1):
            r = base - (sw - 1) + k
            acc = acc + wsh_ref[k:k + 1, lanes] * xb_ref[r:r + ck, lanes]
        return _silu(acc)

    rowp = lax.broadcasted_iota(jnp.int32, (ck, 2 * ck), 0)
    lanep = lax.broadcasted_iota(jnp.int32, (ck, 2 * ck), 1)
    odd = lanep >= ck
    lcol = jnp.where(odd, lanep - ck, lanep)
    incl = rowp >= lcol
    strict = rowp > lcol
    eye = jnp.where(rowp == lcol, 1.0, 0.0).astype(F32)
    bd8 = (rowp // 8) == (lcol // 8)
    lvl = [((rowp // (2 * s)) == (lcol // (2 * s))) & ((rowp // s) != (lcol // s)) for s in (8, 16, 32)]
    ri = lax.broadcasted_iota(jnp.int32, (ck, ck), 0)
    ci = lax.broadcasted_iota(jnp.int32, (ck, ck), 1)
    tri_l = jnp.where(ri >= ci, 1.0, 0.0).astype(BF16)
    r2 = lax.broadcasted_iota(jnp.int32, (2 * ck, 2 * ck), 0)
    c2 = lax.broadcasted_iota(jnp.int32, (2 * ck, 2 * ck), 1)
    tri_u2 = jnp.where(((r2 >= ck) == (c2 >= ck)) & (r2 <= c2), 1.0, 0.0).astype(BF16)

    bf = lambda m: m.astype(BF16)

    def block_diag(pm):
        return bf(jnp.concatenate([jnp.where(odd, 0.0, pm), jnp.where(odd, pm, 0.0)], axis=0))

    def block_rows(top, bottom):
        z = jnp.zeros_like(top)
        return jnp.concatenate([jnp.concatenate([top, z], axis=1), jnp.concatenate([z, bottom], axis=1)], axis=0)

    bg = [bg_ref[0, cc * ck:(cc + 1) * ck, :] for cc in range(cps)]
    gc_cols = [sum(_dot(tri_l, part) for part in _split3(bg[cc])) for cc in range(cps)]
    gc_rows = [sum(_dot(part, tri_u2) for part in _split3(bgt_ref[cc])) for cc in range(cps)]

    nh2 = N_HEADS // 2
    heads = range(cps * N_HEADS)
    pairs = range(cps * nh2)
    hcc = lambda i: (i // N_HEADS, i % N_HEADS)
    ev = lambda j: (j // nh2) * N_HEADS + 2 * (j % nh2)
    q = [conv_cols(hcc(i)[0], hcc(i)[1] * HEAD_DIM) for i in heads]
    k = [conv_cols(hcc(i)[0], key_dim + hcc(i)[1] * HEAD_DIM) for i in heads]
    v = [conv_cols(hcc(i)[0], 2 * key_dim + hcc(i)[1] * HEAD_DIM) for i in heads]
    q = [x * lax.rsqrt(jnp.sum(x * x, axis=-1, keepdims=True) + NORM_EPS) * (HEAD_DIM ** -0.5) for x in q]
    k = [x * lax.rsqrt(jnp.sum(x * x, axis=-1, keepdims=True) + NORM_EPS) for x in k]
    beta = [bg[hcc(i)[0]][:, hcc(i)[1]:hcc(i)[1] + 1] for i in heads]
    gcc = [gc_cols[hcc(i)[0]][:, N_HEADS + hcc(i)[1]:N_HEADS + hcc(i)[1] + 1] for i in heads]
    kb = [k[i] * beta[i] for i in heads]
    kbf = [bf(x) for x in k]
    kk = [block_rows(kbf[ev(j)], kbf[ev(j) + 1]) for j in pairs]
    gcc2 = [jnp.where(odd, gcc[ev(j) + 1], gcc[ev(j)]) for j in pairs]
    gcr2 = [gc_rows[j // nh2][nh2 + j % nh2:nh2 + j % nh2 + 1, :] for j in pairs]
    decay = [jnp.exp(jnp.where(incl, gcc2[j] - gcr2[j], -jnp.inf)) for j in pairs]
    a = [jnp.where(strict,
                   _dot_nt(jnp.concatenate([bf(kb[ev(j)]), bf(kb[ev(j) + 1])], axis=1), kk[j]) * decay[j], 0.0)
         for j in pairs]
    qk = [_dot_nt(jnp.concatenate([bf(q[ev(j)]), bf(q[ev(j) + 1])], axis=1), kk[j]) * decay[j] for j in pairs]

    ad = [jnp.where(bd8, x, 0.0) for x in a]
    adb = [bf(x) for x in ad]
    a2 = [_dot(adb[j], block_diag(ad[j])) for j in pairs]
    a2d = [block_diag(x) for x in a2]
    a3 = [_dot(adb[j], a2d[j]) for j in pairs]
    a4 = [_dot(bf(a2[j]), a2d[j]) for j in pairs]
    t = [eye - ad[j] + a2[j] - a3[j] for j in pairs]
    t = [t[j] + _dot(bf(t[j]), block_diag(a4[j])) for j in pairs]
    for m in lvl:
        x = [_dot(bf(jnp.where(m, a[j], 0.0)), block_diag(t[j])) for j in pairs]
        t = [t[j] - _dot(bf(t[j]), block_diag(x[j])) for j in pairs]

    egc = [jnp.exp(x) for x in gcc]
    rhs = [bf(jnp.concatenate([v[i] * beta[i], kb[i] * egc[i]], axis=1)) for i in heads]
    sol2 = [_dot(bf(t[j]), block_rows(rhs[ev(j)], rhs[ev(j) + 1])) for j in pairs]
    sol = [sol2[(i // N_HEADS) * nh2 + (i % N_HEADS) // 2][:, (i % 2) * 2 * HEAD_DIM:(i % 2 + 1) * 2 * HEAD_DIM]
           for i in heads]
    g_last = [x[ck - 1:ck, :] for x in gcc]
    k_dec = [bf(k[i] * jnp.exp(g_last[i] - gcc[i])) for i in heads]
    wq_lhs = [bf(jnp.concatenate([sol[i][:, HEAD_DIM:], q[i] * egc[i]], axis=0)) for i in heads]
    qkb = [bf(x) for x in qk]

    s = [s_ref[h] for h in range(N_HEADS)]
    for cc in range(cps):
        hs = range(N_HEADS)
        u0 = cc * N_HEADS
        sb = [bf(x) for x in s]
        wq = [_dot(wq_lhs[u0 + h], sb[h]) for h in hs]
        vb = [bf(sol[u0 + h][:, :HEAD_DIM] - wq[h][:ck]) for h in hs]
        o2 = [_dot(qkb[cc * nh2 + p], block_rows(vb[2 * p], vb[2 * p + 1])) for p in range(nh2)]
        s = [s[h] * jnp.exp(g_last[u0 + h]) + _dot_tn(k_dec[u0 + h], vb[h]) for h in hs]
        for h in hs:
            o = wq[h][ck:] + o2[h // 2][:, (h % 2) * HEAD_DIM:(h % 2 + 1) * HEAD_DIM]
            on = o * lax.rsqrt(jnp.mean(o * o, axis=-1, keepdims=True) + NORM_EPS) * ong_ref[...]
            lanes = slice(h * HEAD_DIM, (h + 1) * HEAD_DIM)
            o_ref[0, cc * ck:(cc + 1) * ck, lanes] = (
                on * gsil_ref[0, cc * ck:(cc + 1) * ck, lanes].astype(F32)).astype(BF16)
    for h in range(N_HEADS):
        s_ref[h] = s[h]

    xb_ref[0:_SHORT_PAD, :] = xb_ref[rows:rows + _SHORT_PAD, :]

    @pl.when(c == n_steps - 1)
    def _():
        sfin_ref[0] = s_ref[...]


def _delta_branch(qkv, cache, bg, bgt, gsil, s0, p):
    b, t, qkv_dim = qkv.shape
    assert t % CHUNK == 0
    nc = t // CHUNK
    cps = _DELTA_CPS if nc % _DELTA_CPS == 0 else 1
    n_steps = nc // cps
    rows = cps * CHUNK
    sw = p["w_short"].shape[0]
    cache_p = jnp.pad(cache.astype(F32), ((0, 0), (_SHORT_PAD - (sw - 1), 0), (0, 0)))
    val_dim = N_HEADS * HEAD_DIM
    bgt3 = bgt.reshape(2 * N_HEADS, b * nc, CHUNK).transpose(1, 0, 2).reshape(b * nc, N_HEADS, 2 * CHUNK)
    return pl.pallas_call(
        functools.partial(_delta_kernel, n_steps, cps),
        grid=(b, n_steps),
        in_specs=[
            pl.BlockSpec((1, rows, qkv_dim), lambda i, j: (i, j, 0)),
            pl.BlockSpec((1, _SHORT_PAD, qkv_dim), lambda i, j: (i, 0, 0)),
            pl.BlockSpec((sw, qkv_dim), lambda i, j: (0, 0)),
            pl.BlockSpec((1, rows, LANES), lambda i, j: (i, j, 0)),
            pl.BlockSpec((cps, N_HEADS, 2 * CHUNK), lambda i, j: (i * n_steps + j, 0, 0)),
            pl.BlockSpec((1, rows, val_dim), lambda i, j: (i, j, 0)),
            pl.BlockSpec((1, HEAD_DIM), lambda i, j: (0, 0)),
            pl.BlockSpec((1, N_HEADS, HEAD_DIM, HEAD_DIM), lambda i, j: (i, 0, 0, 0)),
        ],
        out_specs=[
            pl.BlockSpec((1, rows, val_dim), lambda i, j: (i, j, 0)),
            pl.BlockSpec((1, N_HEADS, HEAD_DIM, HEAD_DIM), lambda i, j: (i, 0, 0, 0)),
        ],
        out_shape=[
            jax.ShapeDtypeStruct((b, t, val_dim), BF16),
            jax.ShapeDtypeStruct((b, N_HEADS, HEAD_DIM, HEAD_DIM), F32),
        ],
        scratch_shapes=[pltpu.VMEM((_SHORT_PAD + rows, qkv_dim), F32),
                        pltpu.VMEM((N_HEADS, HEAD_DIM, HEAD_DIM), F32)],
        compiler_params=pltpu.CompilerParams(
            dimension_semantics=("arbitrary", "arbitrary"), vmem_limit_bytes=VMEM_LIMIT),
        name="delta_rule",
    )(qkv, cache_p, p["w_short"], bg.reshape(b, t, LANES), bgt3, gsil, p["o_norm_g"], s0.astype(F32))


_R_E1, _R_E2, _R_RANK1, _R_RANK2, _R_W1, _R_W2 = range(6)


def _merge_kernel(alpha, convg_ref, og_ref, sgd_ref, x_ref, wo_ref, wout_ref, g_ref, b_ref,
                  wr_ref, br_ref, h_ref, route_ref, routet_ref, cnt_ref, carry_ref):
    @pl.when(pl.program_id(0) == 0)
    def _():
        carry_ref[...] = jnp.zeros_like(carry_ref)

    d_out = _dot(og_ref[...], wo_ref[...])
    merged = convg_ref[...].astype(F32) + d_out * sgd_ref[...].astype(F32)
    mix = _dot(merged.astype(BF16), wout_ref[...])
    h = _layer_norm(alpha * x_ref[...] + mix, g_ref[...], b_ref[...])
    h_ref[...] = h

    h_hi, h_mid, _ = _split3(h)
    w_hi, w_mid, _ = _split3(wr_ref[...])
    logits = _dot(h_hi, w_hi) + _dot(h_mid, w_hi) + _dot(h_hi, w_mid) + br_ref[...]
    tm = logits.shape[0]
    col = lax.broadcasted_iota(jnp.int32, logits.shape, 1).astype(F32)
    big = float(LANES)
    is_g = col < N_GROUPS
    mg = jnp.max(jnp.where(is_g, logits, -jnp.inf), axis=-1, keepdims=True)
    sg = jnp.sum(jnp.where(is_g, jnp.exp(jnp.where(is_g, logits, mg) - mg), 0.0), axis=-1, keepdims=True)
    pg_top = 1.0 / sg
    gidx = jnp.min(jnp.where(is_g & (logits == mg), col, big), axis=-1, keepdims=True)
    lo = ROUTE_COL0 + EXP_PER_GROUP * gidx
    sel = (col >= lo) & (col < lo + EXP_PER_GROUP)
    le = jnp.where(sel, logits, -jnp.inf)
    m1 = jnp.max(le, axis=-1, keepdims=True)
    i1 = jnp.min(jnp.where(le == m1, col, big), axis=-1, keepdims=True)
    le2 = jnp.where(col == i1, -jnp.inf, le)
    m2 = jnp.max(le2, axis=-1, keepdims=True)
    i2 = jnp.min(jnp.where(le2 == m2, col, big), axis=-1, keepdims=True)
    e2 = jnp.exp(m2 - m1)
    den = 1.0 + e2
    w1 = pg_top / den
    w2 = pg_top * e2 / den

    hit1 = col == i1
    hit2 = col == i2
    member = jnp.where(hit1 | hit2, 1.0, 0.0)
    ri = lax.broadcasted_iota(jnp.int32, (tm, tm), 0)
    ci = lax.broadcasted_iota(jnp.int32, (tm, tm), 1)
    earlier = jnp.where(ri > ci, 1.0, 0.0).astype(BF16)
    before = _dot(earlier, member.astype(BF16)) + carry_ref[...]
    rank1 = jnp.sum(jnp.where(hit1, before, 0.0), axis=-1, keepdims=True)
    rank2 = jnp.sum(jnp.where(hit2, before, 0.0), axis=-1, keepdims=True)
    carry_ref[...] += jnp.sum(member, axis=0, keepdims=True)
    cnt_ref[...] = jnp.broadcast_to(carry_ref[...], cnt_ref.shape)

    fields = (i1 - ROUTE_COL0, i2 - ROUTE_COL0, rank1, rank2, w1, w2)
    route = jnp.zeros_like(logits)
    for c, val in enumerate(fields):
        route = jnp.where(col == float(c), val, route)
    route_ref[...] = route
    sr = lax.broadcasted_iota(jnp.int32, (8, LANES), 0)
    sc = lax.broadcasted_iota(jnp.int32, (8, LANES), 1)
    pick = jnp.where(sr == sc, 1.0, 0.0).astype(BF16)
    routet_ref[...] = sum(_dot_nt(pick, part) for part in _split3(route))


def _merge(convg, og, sgd, x, p, alpha, tm):
    n, d = x.shape
    val_dim = og.shape[1]
    assert n % tm == 0
    row = lambda w: pl.BlockSpec((tm, w), lambda i: (i, 0))
    full = lambda shape: pl.BlockSpec(shape, lambda i: (0, 0))
    const = lambda shape: pl.BlockSpec(shape, lambda i: (0, 0), pipeline_mode=pl.Buffered(1))
    return pl.pallas_call(
        functools.partial(_merge_kernel, alpha),
        grid=(n // tm,),
        in_specs=[row(d), row(val_dim), row(d), row(d), const((val_dim, d)), const((d, d)),
                  full((1, d)), full((1, d)), const((d, LANES)), full((1, LANES))],
        out_specs=[row(d), row(LANES), pl.BlockSpec((8, tm), lambda i: (0, i)), full((8, LANES))],
        out_shape=[jax.ShapeDtypeStruct((n, d), F32), jax.ShapeDtypeStruct((n, LANES), F32),
                   jax.ShapeDtypeStruct((8, n), F32), jax.ShapeDtypeStruct((8, LANES), F32)],
        scratch_shapes=[pltpu.VMEM((1, LANES), F32)],
        compiler_params=pltpu.CompilerParams(
            dimension_semantics=("arbitrary",), vmem_limit_bytes=VMEM_LIMIT),
        name="merge_outproj",
    )(convg, og, sgd, x, p["w_o"], p["w_out"], p["ln1_g"], p["ln1_b"], p["w_router"], p["b_router"])


def _route_plan(route_t, cnt, te):
    n = route_t.shape[1]
    i32 = jnp.int32
    ri = route_t[:4].astype(i32)
    counts = cnt[0, ROUTE_COL0:ROUTE_COL0 + N_EXPERTS].astype(i32)
    ends = jnp.cumsum(counts)
    starts = ends - counts
    eids = jnp.arange(N_EXPERTS, dtype=i32)

    def lookup(table, idx):
        return jnp.sum(jnp.where(idx[None, :] == eids[:, None], table[:, None], 0), axis=0)

    dest = jnp.stack([lookup(starts, ri[0]) + ri[2], lookup(starts, ri[1]) + ri[3]])

    first_tile = starts // te
    last_tile = (ends - 1) // te
    items_e = jnp.where(counts > 0, last_tile - first_tile + 1, 0)
    item_end = jnp.cumsum(items_e)
    item_start = item_end - items_e
    total = item_end[-1]
    n_items = (2 * n) // te + N_EXPERTS - 1
    w = jnp.minimum(jnp.arange(n_items, dtype=i32), total - 1)
    item_e = jnp.sum((item_end[:, None] <= w[None, :]).astype(i32), axis=0)
    item_tile = lookup(first_tile, item_e) + w - lookup(item_start, item_e)
    lo = jnp.clip(lookup(starts, item_e) - item_tile * te, 0, te)
    hi = jnp.clip(lookup(ends, item_e) - item_tile * te, 0, te)
    return dest, (item_tile, item_e, lo, hi, total.reshape(1))


def _dispatch_kernel(d_ref, h_ref, xs_hbm, sem):
    tm = d_ref.shape[-1]

    def row_copy(r, dst):
        return pltpu.make_async_copy(h_ref.at[pl.ds(r, 1)], xs_hbm.at[pl.ds(dst, 1)], sem.at[0])

    def wait(r, carry):
        row_copy(r, 0).wait()
        row_copy(r, 0).wait()
        return carry

    for r in range(tm):
        row_copy(r, d_ref[0, 0, 0, r]).start()
        row_copy(r, d_ref[1, 0, 0, r]).start()
    lax.fori_loop(0, tm, wait, 0, unroll=8)


def _dispatch(h, dest, tm):
    n, d = h.shape
    assert n % tm == 0
    n_steps = n // tm
    return pl.pallas_call(
        _dispatch_kernel,
        grid=(n_steps,),
        in_specs=[pl.BlockSpec((2, 1, 1, tm), lambda i: (0, i, 0, 0), memory_space=pltpu.SMEM),
                  pl.BlockSpec((tm, d), lambda i: (i, 0))],
        out_specs=pl.BlockSpec(memory_space=pl.ANY),
        out_shape=jax.ShapeDtypeStruct((2 * n, d), F32),
        scratch_shapes=[pltpu.SemaphoreType.DMA((1,))],
        compiler_params=pltpu.CompilerParams(
            dimension_semantics=("arbitrary",), vmem_limit_bytes=VMEM_LIMIT),
        name="moe_dispatch",
    )(dest.reshape(2, n_steps, 1, tm), h)


def _expert_kernel(tile_ref, exp_ref, lo_ref, hi_ref, tot_ref, xs_ref, wg_ref, wu_ref, wd_ref,
                   out_ref, wgb_ref, wub_ref, wdb_ref):
    w = pl.program_id(0)
    prev = jnp.maximum(w - 1, 0)
    live = w < tot_ref[0]
    new_expert = (w == 0) | (exp_ref[w] != exp_ref[prev])
    first_of_tile = (w == 0) | (tile_ref[w] != tile_ref[prev])

    @pl.when(live & new_expert)
    def _():
        wgb_ref[...] = wg_ref[0].astype(BF16)
        wub_ref[...] = wu_ref[0].astype(BF16)
        wdb_ref[...] = wd_ref[0].astype(BF16)

    @pl.when(live)
    def _():
        x = xs_ref[...].astype(BF16)
        hg = _dot(x, wgb_ref[...])
        hu = _dot(x, wub_ref[...])
        row = lax.broadcasted_iota(jnp.int32, (x.shape[0], 1), 0)
        mine = (row >= lo_ref[w]) & (row < hi_ref[w])
        act = jnp.where(mine, _silu(hg) * hu, 0.0).astype(BF16)
        part = _dot(act, wdb_ref[...])

        @pl.when(first_of_tile)
        def _():
            out_ref[...] = part

        @pl.when(jnp.logical_not(first_of_tile))
        def _():
            out_ref[...] += part


def _experts(xs, items, p, te):
    rows, d = xs.shape
    ne, _, f = p["w_gate"].shape
    item_tile, item_e, lo, hi, total = items
    n_items = item_tile.shape[0]
    tile_map = lambda w, t_, e_, lo_, hi_, n_: (t_[w], 0)
    exp_map = lambda w, t_, e_, lo_, hi_, n_: (e_[w], 0, 0)
    grid_spec = pltpu.PrefetchScalarGridSpec(
        num_scalar_prefetch=5,
        grid=(n_items,),
        in_specs=[
            pl.BlockSpec((te, d), tile_map),
            pl.BlockSpec((1, d, f), exp_map),
            pl.BlockSpec((1, d, f), exp_map),
            pl.BlockSpec((1, f, d), exp_map),
        ],
        out_specs=pl.BlockSpec((te, d), tile_map),
        scratch_shapes=[pltpu.VMEM((d, f), BF16), pltpu.VMEM((d, f), BF16), pltpu.VMEM((f, d), BF16)],
    )
    return pl.pallas_call(
        _expert_kernel,
        grid_spec=grid_spec,
        out_shape=jax.ShapeDtypeStruct((rows, d), F32),
        compiler_params=pltpu.CompilerParams(
            dimension_semantics=("arbitrary",), vmem_limit_bytes=VMEM_LIMIT),
        name="moe_experts",
    )(item_tile, item_e, lo, hi, total, xs, p["w_gate"], p["w_up"], p["w_down"])


def _combine_kernel(alpha, d_ref, dn_ref, h_ref, route_ref, rows_hbm, g_ref, b_ref, y_ref, o_ref, sem):
    i = pl.program_id(0)
    n_i = pl.num_programs(0)
    tm = h_ref.shape[0]
    slot = i % 2

    def row_copy(src_row, slot_, k, r):
        return pltpu.make_async_copy(rows_hbm.at[pl.ds(src_row, 1)], o_ref.at[slot_, k, pl.ds(r, 1)],
                                     sem.at[slot_])

    def start(dref, slot_):
        for r in range(tm):
            row_copy(dref[0, 0, 0, r], slot_, 0, r).start()
            row_copy(dref[1, 0, 0, r], slot_, 1, r).start()

    def wait(slot_):
        def body(r, carry):
            row_copy(0, slot_, 0, r).wait()
            row_copy(0, slot_, 1, r).wait()
            return carry
        lax.fori_loop(0, tm, body, 0, unroll=8)

    @pl.when(i == 0)
    def _():
        start(d_ref, 0)

    for nxt in (0, 1):
        @pl.when((i + 1 < n_i) & (slot == 1 - nxt))
        def _():
            start(dn_ref, nxt)

    wait(slot)
    route = route_ref[...]
    col = lax.broadcasted_iota(jnp.int32, route.shape, 1)
    w1 = jnp.sum(jnp.where(col == _R_W1, route, 0.0), axis=-1, keepdims=True)
    w2 = jnp.sum(jnp.where(col == _R_W2, route, 0.0), axis=-1, keepdims=True)
    moe = w1 * o_ref[slot, 0] + w2 * o_ref[slot, 1]
    y_ref[...] = _layer_norm(alpha * h_ref[...] + moe, g_ref[...], b_ref[...])


def _combine(h, route, rows, dest, p, alpha, tm):
    n, d = h.shape
    assert n % tm == 0
    n_i = n // tm
    dest4 = dest.reshape(2, n_i, 1, tm)
    cur = pl.BlockSpec((2, 1, 1, tm), lambda i: (0, i, 0, 0), memory_space=pltpu.SMEM)
    nxt = pl.BlockSpec((2, 1, 1, tm), lambda i: (0, jnp.minimum(i + 1, n_i - 1), 0, 0),
                       memory_space=pltpu.SMEM)
    return pl.pallas_call(
        functools.partial(_combine_kernel, alpha),
        grid=(n_i,),
        in_specs=[cur, nxt,
                  pl.BlockSpec((tm, d), lambda i: (i, 0)),
                  pl.BlockSpec((tm, LANES), lambda i: (i, 0)),
                  pl.BlockSpec(memory_space=pl.ANY),
                  pl.BlockSpec((1, d), lambda i: (0, 0)),
                  pl.BlockSpec((1, d), lambda i: (0, 0))],
        out_specs=pl.BlockSpec((tm, d), lambda i: (i, 0)),
        out_shape=jax.ShapeDtypeStruct((n, d), F32),
        scratch_shapes=[pltpu.VMEM((2, 2, tm, d), F32), pltpu.SemaphoreType.DMA((2,))],
        compiler_params=pltpu.CompilerParams(
            dimension_semantics=("arbitrary",), vmem_limit_bytes=VMEM_LIMIT),
        name="moe_combine",
    )(dest4, dest4, h, route, rows, p["ln2_g"], p["ln2_b"])


def _pack_layer(w_in, b_in, w_dw, b_dw, lnc_g, lnc_b, w_conv_out, w_short, a_log, dt_bias, o_norm_g,
                w_o, w_out, ln1_g, ln1_b, w_rg, b_rg, w_re, b_re, w_gate, w_up, w_down, ln2_g, ln2_b):
    d = w_in.shape[0]
    c_conv = w_dw.shape[1]
    qkv_dim = w_short.shape[1]
    o_qkv = 2 * c_conv
    o_ba = o_qkv + qkv_dim
    o_tail = o_ba + 2 * N_HEADS

    def pad_cols(a, width):
        return jnp.pad(a, ((0, 0), (0, width - a.shape[1])))

    b2 = b_in[None, :].astype(F32)
    nh = N_HEADS
    zeros_h = jnp.zeros((nh,), F32)
    head_params = jnp.stack([jnp.concatenate([zeros_h, a_log.astype(F32)]),
                             jnp.concatenate([zeros_h, dt_bias.astype(F32)])])
    half = _TN // 2
    w_main = jnp.concatenate([w_in[:, :o_ba], w_in[:, o_tail:]], axis=1).astype(BF16)
    w_blk = w_main.reshape(d, -1, half).transpose(1, 0, 2)
    w_ba = w_in[:, o_ba:o_tail].astype(BF16)
    w_router = jnp.concatenate([w_rg, w_re], axis=1).astype(F32)
    b_router = jnp.concatenate([b_rg, b_re])[None, :].astype(F32)
    return dict(
        c_conv=c_conv, qkv_dim=qkv_dim,
        w_blk=w_blk, b_in=b2,
        w_ba=pad_cols(w_ba, LANES),
        b_ba=pad_cols(b2[:, o_ba:o_tail], LANES),
        w_bat=w_ba.T,
        b_bat=jnp.broadcast_to(b_in[o_ba:o_tail, None].astype(F32), (2 * nh, LANES)),
        b_tail=b2[:, o_tail:],
        p_row=pad_cols(head_params, LANES),
        p_col=pad_cols(head_params.T, LANES),
        w_dw=w_dw.astype(F32), b_dw=b_dw[None, :].astype(F32),
        lnc_g=lnc_g[None, :].astype(F32), lnc_b=lnc_b[None, :].astype(F32),
        w_conv_out=w_conv_out.astype(BF16),
        w_short=w_short.astype(F32),
        o_norm_g=o_norm_g[None, :].astype(F32),
        w_o=w_o.astype(BF16), w_out=w_out.astype(BF16),
        ln1_g=ln1_g[None, :].astype(F32), ln1_b=ln1_b[None, :].astype(F32),
        w_router=pad_cols(w_router, LANES), b_router=pad_cols(b_router, LANES),
        w_gate=w_gate.astype(F32), w_up=w_up.astype(F32), w_down=w_down.astype(F32),
        ln2_g=ln2_g[None, :].astype(F32), ln2_b=ln2_b[None, :].astype(F32),
    )


_MERGE_TM = 512
_COMBINE_TM = 256
_EXPERT_TE = 256
_DISPATCH_TM = 256


def _block(x, conv_buf, short_buf, s0, p, alpha, tm, conv_tt):
    b, t, d = x.shape
    n = b * t
    x2 = x.reshape(n, d).astype(F32)
    u, qkv, bg, bgt, gsil, sgc, sgd = _inproj(x2, p, tm)
    c_conv = u.shape[1]
    u3 = u.reshape(b, t, c_conv)
    qkv3 = qkv.reshape(b, t, -1)
    convg = _conv_branch(u3, conv_buf, sgc.reshape(b, t, d), p, conv_tt)
    og, s_new = _delta_branch(qkv3, short_buf, bg, bgt, gsil.reshape(b, t, -1), s0, p)
    h, route, route_t, cnt = _merge(convg.reshape(n, d), og.reshape(n, -1), sgd, x2, p, alpha, min(_MERGE_TM, n))
    dest, items = _route_plan(route_t, cnt, _EXPERT_TE)
    xs = _dispatch(h, dest, min(_DISPATCH_TM, n))
    rows = _experts(xs, items, p, _EXPERT_TE)
    y = _combine(h, route, rows, dest, p, alpha, min(_COMBINE_TM, n))
    kc = conv_buf.shape[1]
    ks = short_buf.shape[1]
    assert t >= kc and t >= ks
    return (y.reshape(b, t, d).astype(x.dtype), u3[:, t - kc:].astype(x.dtype),
            qkv3[:, t - ks:].astype(x.dtype), s_new.astype(s0.dtype))


def kernel(x_prompt, x_sample, cache_conv, cache_short, state_delta, w_in, b_in, w_dw, b_dw, lnc_g, lnc_b, w_conv_out, w_short, a_log, dt_bias, o_norm_g, w_o, w_out, ln1_g, ln1_b, w_rg, b_rg, w_re, b_re, w_gate, w_up, w_down, ln2_g, ln2_b):
    weights = (w_in, b_in, w_dw, b_dw, lnc_g, lnc_b, w_conv_out, w_short, a_log, dt_bias, o_norm_g,
               w_o, w_out, ln1_g, ln1_b, w_rg, b_rg, w_re, b_re, w_gate, w_up, w_down, ln2_g, ln2_b)
    depth = w_in.shape[0]
    alpha = (2.0 * depth) ** 0.25
    yp, ys = x_prompt, x_sample
    bp = x_prompt.shape[0]
    outs = [[] for _ in range(6)]
    for l in range(depth):
        p = _pack_layer(*(wt[l] for wt in weights))
        zc = jnp.zeros((bp,) + cache_conv.shape[2:], x_prompt.dtype)
        zs = jnp.zeros((bp,) + cache_short.shape[2:], x_prompt.dtype)
        zd = jnp.zeros((bp,) + state_delta.shape[2:], state_delta.dtype)
        yp, c, s, dl = _block(yp, zc, zs, zd, p, alpha, 1024, 256)
        outs[0].append(c), outs[1].append(s), outs[2].append(dl)
        ys, c, s, dl = _block(ys, cache_conv[l], cache_short[l], state_delta[l], p, alpha, 1024, 64)
        outs[3].append(c), outs[4].append(s), outs[5].append(dl)
    return (yp, ys) + tuple(jnp.stack(o) for o in outs)
```

```python
import functools

import jax
import jax.numpy as jnp
from jax import lax
from jax.experimental import pallas as pl
from jax.experimental.pallas import tpu as pltpu

F32 = jnp.float32
BF16 = jnp.bfloat16

CHUNK = 64
N_HEADS = 8
HEAD_DIM = 128
N_GROUPS = 4
EXP_PER_GROUP = 8
N_EXPERTS = N_GROUPS * EXP_PER_GROUP
LN_EPS = 1e-5
NORM_EPS = 1e-6
LANES = 128
ROUTE_COL0 = N_GROUPS
VMEM_LIMIT = 56 * 1024 * 1024


def _dot(a, b):
    return jnp.dot(a, b, preferred_element_type=F32)


def _dot_nt(a, b):
    return lax.dot_general(a, b, (((1,), (1,)), ((), ())), preferred_element_type=F32)


def _dot_tn(a, b):
    return lax.dot_general(a, b, (((0,), (0,)), ((), ())), preferred_element_type=F32)


def _split3(x):
    hi = x.astype(BF16)
    r1 = x - hi.astype(F32)
    mid = r1.astype(BF16)
    lo = (r1 - mid.astype(F32)).astype(BF16)
    return hi, mid, lo


def _pack_pair(lo, hi):
    lo_bits = lax.bitcast_convert_type(lo.astype(BF16).astype(F32), jnp.uint32)
    hi_bits = lax.bitcast_convert_type(hi.astype(BF16).astype(F32), jnp.uint32)
    return (hi_bits & jnp.uint32(0xFFFF0000)) | (lo_bits >> 16)


def _unpack_pair(packed):
    lo = lax.bitcast_convert_type(packed << 16, F32)
    hi = lax.bitcast_convert_type(packed & jnp.uint32(0xFFFF0000), F32)
    return lo, hi


def _sigmoid(x):
    return 1.0 / (1.0 + jnp.exp(-x))


def _silu(x):
    return x * _sigmoid(x)


def _softplus(x):
    return jnp.maximum(x, 0.0) + jnp.log1p(jnp.exp(-jnp.abs(x)))


def _layer_norm(x, g, b):
    mu = jnp.mean(x, axis=-1, keepdims=True)
    xc = x - mu
    var = jnp.mean(xc * xc, axis=-1, keepdims=True)
    return xc * lax.rsqrt(var + LN_EPS) * g + b


def _clamp(v, lo, hi):
    return jnp.minimum(jnp.maximum(v, lo), hi)


_TN = 512
_J_GLU, _J_QKV, _J_BA, _J_GO, _J_GC, _J_GD, _J_END = 0, 4, 10, 11, 13, 17, 21


def _wide_dot(x, w_ref):
    return jnp.concatenate([_dot(x, w_ref[b]) for b in range(w_ref.shape[0])], axis=1)


def _inproj_kernel(x_ref, wga_ref, wgb_ref, wqkv_ref, wba_ref, wbat_ref, wtail_ref,
                   bga_ref, bgb_ref, bqkv_ref, bba_ref, bbat_ref, btail_ref, prow_ref, pcol_ref,
                   u_ref, qkv_ref, bg_ref, bgt_ref, gsil_ref, sgc_ref, sgd_ref, xb_ref):
    j = pl.program_id(1)

    @pl.when(j == 0)
    def _():
        xb_ref[...] = x_ref[...].astype(BF16)

    @pl.when(j < _J_QKV)
    def _():
        xb = xb_ref[...]
        value = _dot(xb, wga_ref[0]) + bga_ref[...]
        gate = _dot(xb, wgb_ref[0]) + bgb_ref[...]
        u_ref[...] = value * _sigmoid(gate)

    @pl.when((j >= _J_QKV) & (j < _J_BA))
    def _():
        qkv_ref[...] = _wide_dot(xb_ref[...], wqkv_ref) + bqkv_ref[...]

    @pl.when(j == _J_BA)
    def _():
        xb = xb_ref[...]
        z = _dot(xb, wba_ref[...]) + bba_ref[...]
        col = lax.broadcasted_iota(jnp.int32, z.shape, 1)
        g = -jnp.exp(prow_ref[0:1, :]) * _softplus(z + prow_ref[1:2, :])
        bg_ref[...] = jnp.where(col < N_HEADS, _sigmoid(z), g)
        zt = _dot_nt(wbat_ref[...], xb) + bbat_ref[:, 0:1]
        row = lax.broadcasted_iota(jnp.int32, zt.shape, 0)
        gt = -jnp.exp(pcol_ref[:, 0:1]) * _softplus(zt + pcol_ref[:, 1:2])
        bgt_ref[...] = jnp.where(row < N_HEADS, _sigmoid(zt), gt)

    @pl.when((j >= _J_GO) & (j < _J_GC))
    def _():
        z = _wide_dot(xb_ref[...], wtail_ref) + btail_ref[...]
        gsil_ref[...] = _silu(z).astype(BF16)

    @pl.when((j >= _J_GC) & (j < _J_GD))
    def _():
        z = _wide_dot(xb_ref[...], wtail_ref) + btail_ref[...]
        sgc_ref[...] = _sigmoid(z).astype(BF16)

    @pl.when(j >= _J_GD)
    def _():
        z = _wide_dot(xb_ref[...], wtail_ref) + btail_ref[...]
        sgd_ref[...] = _sigmoid(z).astype(BF16)


def _inproj(x, pk, tm):
    n, d = x.shape
    c_conv, qkv_dim = pk["c_conv"], pk["qkv_dim"]
    val_dim = N_HEADS * HEAD_DIM
    half = _TN // 2
    n_glu = _J_QKV - _J_GLU
    assert n % tm == 0 and c_conv == n_glu * half and (2 * c_conv) % _TN == 0
    qkv_blk0 = 2 * c_conv // _TN
    tail_blk0 = (2 * c_conv + qkv_dim) // _TN
    assert (2 * c_conv + qkv_dim) % _TN == 0

    def cm(lo, hi):
        return lambda i, j: (0, _clamp(j - lo, 0, hi - lo - 1))

    def om(lo, hi):
        return lambda i, j: (i, _clamp(j - lo, 0, hi - lo - 1))

    in_specs = [
        pl.BlockSpec((tm, d), lambda i, j: (i, 0)),
        pl.BlockSpec((1, d, half), lambda i, j: (_clamp(j, 0, n_glu - 1), 0, 0)),
        pl.BlockSpec((1, d, half), lambda i, j: (n_glu + _clamp(j, 0, n_glu - 1), 0, 0)),
        pl.BlockSpec((2, d, half), lambda i, j: (qkv_blk0 + _clamp(j - _J_QKV, 0, _J_BA - _J_QKV - 1), 0, 0)),
        pl.BlockSpec((d, LANES), lambda i, j: (0, 0)),
        pl.BlockSpec((2 * N_HEADS, d), lambda i, j: (0, 0)),
        pl.BlockSpec((2, d, half), lambda i, j: (tail_blk0 + _clamp(j - _J_GO, 0, _J_END - _J_GO - 1), 0, 0)),
        pl.BlockSpec((1, half), lambda i, j: (0, _clamp(j, 0, n_glu - 1))),
        pl.BlockSpec((1, half), lambda i, j: (0, n_glu + _clamp(j, 0, n_glu - 1))),
        pl.BlockSpec((1, _TN), lambda i, j: (0, qkv_blk0 + _clamp(j - _J_QKV, 0, _J_BA - _J_QKV - 1))),
        pl.BlockSpec((1, LANES), lambda i, j: (0, 0)),
        pl.BlockSpec((2 * N_HEADS, LANES), lambda i, j: (0, 0)),
        pl.BlockSpec((1, _TN), cm(_J_GO, _J_END)),
        pl.BlockSpec((2, LANES), lambda i, j: (0, 0)),
        pl.BlockSpec((2 * N_HEADS, LANES), lambda i, j: (0, 0)),
    ]
    out_shape = [
        jax.ShapeDtypeStruct((n, c_conv), F32),
        jax.ShapeDtypeStruct((n, qkv_dim), F32),
        jax.ShapeDtypeStruct((n, LANES), F32),
        jax.ShapeDtypeStruct((2 * N_HEADS, n), F32),
        jax.ShapeDtypeStruct((n, val_dim), BF16),
        jax.ShapeDtypeStruct((n, d), BF16),
        jax.ShapeDtypeStruct((n, d), BF16),
    ]
    out_specs = [
        pl.BlockSpec((tm, _TN // 2), om(_J_GLU, _J_QKV)),
        pl.BlockSpec((tm, _TN), om(_J_QKV, _J_BA)),
        pl.BlockSpec((tm, LANES), lambda i, j: (i, 0)),
        pl.BlockSpec((2 * N_HEADS, tm), lambda i, j: (0, i)),
        pl.BlockSpec((tm, _TN), om(_J_GO, _J_GC)),
        pl.BlockSpec((tm, _TN), om(_J_GC, _J_GD)),
        pl.BlockSpec((tm, _TN), om(_J_GD, _J_END)),
    ]
    return pl.pallas_call(
        _inproj_kernel,
        grid=(n // tm, _J_END),
        in_specs=in_specs,
        out_specs=out_specs,
        out_shape=out_shape,
        scratch_shapes=[pltpu.VMEM((tm, d), BF16)],
        compiler_params=pltpu.CompilerParams(
            dimension_semantics=("arbitrary", "arbitrary"), vmem_limit_bytes=VMEM_LIMIT),
        name="inproj",
    )(x, pk["w_blk"], pk["w_blk"], pk["w_blk"], pk["w_ba"], pk["w_bat"], pk["w_blk"],
      pk["b_in"], pk["b_in"], pk["b_in"], pk["b_ba"], pk["b_bat"], pk["b_tail"], pk["p_row"], pk["p_col"])


_HALO = 32
_CONV_RB = 64
_CONV_FB = 8


def _conv_kernel(u_ref, cache_ref, wdw_ref, bdw_ref, lng_ref, lnb_ref, wco_ref, sgc_ref,
                 out_ref, xt_ref, yt_ref, cn_ref):
    t = pl.program_id(1)
    tt = u_ref.shape[1]
    c_conv = u_ref.shape[2]
    width = wdw_ref.shape[0]
    nfold = c_conv // LANES
    first = _HALO - (width - 1)

    @pl.when(t == 0)
    def _():
        for s in range(nfold):
            xt_ref[pl.ds(s, _HALO, stride=nfold), :] = cache_ref[0, :, s * LANES:(s + 1) * LANES]

    for s in range(nfold):
        xt_ref[pl.ds(_HALO * nfold + s, tt, stride=nfold), :] = u_ref[0, :, s * LANES:(s + 1) * LANES]

    def frames(ib, carry):
        f0 = ib * _CONV_FB
        acc = [None] * _CONV_FB
        for k in range(width):
            wk = wdw_ref[k]
            for j in range(_CONV_FB):
                row = pl.multiple_of((first + f0 + j + k) * nfold, nfold)
                term = wk * xt_ref[pl.ds(row, nfold), :]
                acc[j] = term if acc[j] is None else acc[j] + term
        for j in range(_CONV_FB):
            yt_ref[pl.ds(pl.multiple_of((f0 + j) * nfold, nfold), nfold), :] = acc[j]
        return carry

    lax.fori_loop(0, tt // _CONV_FB, frames, 0)

    for rb in range(tt // _CONV_RB):
        r0 = rb * _CONV_RB
        y = jnp.concatenate([yt_ref[pl.ds(r0 * nfold + s, _CONV_RB, stride=nfold), :] for s in range(nfold)],
                            axis=1)
        y = _layer_norm(y + bdw_ref[...], lng_ref[...], lnb_ref[...])
        cn_ref[r0:r0 + _CONV_RB, :] = _silu(y).astype(BF16)

    co = _dot(cn_ref[...], wco_ref[...])
    out_ref[0] = (co * sgc_ref[0].astype(F32)).astype(BF16)
    xt_ref[0:_HALO * nfold, :] = xt_ref[tt * nfold:(tt + _HALO) * nfold, :]


def _conv_branch(u, cache, sgc, p, tt):
    b, t, c_conv = u.shape
    d = sgc.shape[-1]
    width = p["w_dw"].shape[0]
    assert t % tt == 0 and tt % _CONV_RB == 0 and tt >= _HALO and width - 1 <= _HALO
    assert c_conv % (8 * LANES) == 0
    nfold = c_conv // LANES
    cache_p = jnp.pad(cache.astype(F32), ((0, 0), (_HALO - (width - 1), 0), (0, 0)))
    full2 = lambda shape: pl.BlockSpec(shape, lambda i, j: (0, 0))
    return pl.pallas_call(
        _conv_kernel,
        grid=(b, t // tt),
        in_specs=[
            pl.BlockSpec((1, tt, c_conv), lambda i, j: (i, j, 0)),
            pl.BlockSpec((1, _HALO, c_conv), lambda i, j: (i, 0, 0)),
            pl.BlockSpec((width, nfold, LANES), lambda i, j: (0, 0, 0)),
            full2((1, c_conv)), full2((1, c_conv)), full2((1, c_conv)),
            full2((c_conv, d)),
            pl.BlockSpec((1, tt, d), lambda i, j: (i, j, 0)),
        ],
        out_specs=pl.BlockSpec((1, tt, d), lambda i, j: (i, j, 0)),
        out_shape=jax.ShapeDtypeStruct((b, t, d), BF16),
        scratch_shapes=[pltpu.VMEM(((tt + _HALO) * nfold, LANES), F32),
                        pltpu.VMEM((tt * nfold, LANES), F32),
                        pltpu.VMEM((tt, c_conv), BF16)],
        compiler_params=pltpu.CompilerParams(
            dimension_semantics=("arbitrary", "arbitrary"), vmem_limit_bytes=VMEM_LIMIT),
        name="conv_branch",
    )(u, cache_p, p["w_dw"].reshape(width, nfold, LANES), p["b_dw"], p["lnc_g"], p["lnc_b"],
      p["w_conv_out"], sgc)


_SHORT_PAD = 8
_DELTA_CPS = 4


def _delta_kernel(n_steps, cps, qkv_ref, cache_ref, wsh_ref, bg_ref, bgt_ref, gsil_ref, ong_ref, s0_ref,
                  o_ref, sfin_ref, xb_ref, s_ref):
    c = pl.program_id(1)
    ck = CHUNK
    rows = cps * ck
    key_dim = N_HEADS * HEAD_DIM
    sw = wsh_ref.shape[0]

    @pl.when(c == 0)
    def _():
        s_ref[...] = s0_ref[0]
        xb_ref[0:_SHORT_PAD, :] = cache_ref[0]

    xb_ref[_SHORT_PAD:_SHORT_PAD + rows, :] = qkv_ref[0]

    def conv_cols(cc, lo):
        lanes = slice(lo, lo + HEAD_DIM)
        base = _SHORT_PAD + cc * ck
        acc = wsh_ref[sw - 1:sw, lanes] * xb_ref[base:base + ck, lanes]
        for k in range(sw - 1):
            r = base - (sw - 1) + k
            acc = acc + wsh_ref[k:k + 1, lanes] * xb_ref[r:r + ck, lanes]
        return _silu(acc)

    rowp = lax.broadcasted_iota(jnp.int32, (ck, 2 * ck), 0)
    lanep = lax.broadcasted_iota(jnp.int32, (ck, 2 * ck), 1)
    odd = lanep >= ck
    lcol = jnp.where(odd, lanep - ck, lanep)
    incl = rowp >= lcol
    strict = rowp > lcol
    eye = jnp.where(rowp == lcol, 1.0, 0.0).astype(F32)
    bd8 = (rowp // 8) == (lcol // 8)
    lvl = [((rowp // (2 * s)) == (lcol // (2 * s))) & ((rowp // s) != (lcol // s)) for s in (8, 16, 32)]
    ri = lax.broadcasted_iota(jnp.int32, (ck, ck), 0)
    ci = lax.broadcasted_iota(jnp.int32, (ck, ck), 1)
    tri_l = jnp.where(ri >= ci, 1.0, 0.0).astype(BF16)
    r2 = lax.broadcasted_iota(jnp.int32, (2 * ck, 2 * ck), 0)
    c2 = lax.broadcasted_iota(jnp.int32, (2 * ck, 2 * ck), 1)
    tri_u2 = jnp.where(((r2 >= ck) == (c2 >= ck)) & (r2 <= c2), 1.0, 0.0).astype(BF16)

    bf = lambda m: m.astype(BF16)

    def block_diag(pm):
        return bf(jnp.concatenate([jnp.where(odd, 0.0, pm), jnp.where(odd, pm, 0.0)], axis=0))

    def block_rows(top, bottom):
        z = jnp.zeros_like(top)
        return jnp.concatenate([jnp.concatenate([top, z], axis=1), jnp.concatenate([z, bottom], axis=1)], axis=0)

    bg = [bg_ref[0, cc * ck:(cc + 1) * ck, :] for cc in range(cps)]
    gc_cols = [sum(_dot(tri_l, part) for part in _split3(bg[cc])) for cc in range(cps)]
    gc_rows = [sum(_dot(part, tri_u2) for part in _split3(bgt_ref[cc])) for cc in range(cps)]

    nh2 = N_HEADS // 2
    heads = range(cps * N_HEADS)
    pairs = range(cps * nh2)
    hcc = lambda i: (i // N_HEADS, i % N_HEADS)
    ev = lambda j: (j // nh2) * N_HEADS + 2 * (j % nh2)
    q = [conv_cols(hcc(i)[0], hcc(i)[1] * HEAD_DIM) for i in heads]
    k = [conv_cols(hcc(i)[0], key_dim + hcc(i)[1] * HEAD_DIM) for i in heads]
    v = [conv_cols(hcc(i)[0], 2 * key_dim + hcc(i)[1] * HEAD_DIM) for i in heads]
    q = [x * lax.rsqrt(jnp.sum(x * x, axis=-1, keepdims=True) + NORM_EPS) * (HEAD_DIM ** -0.5) for x in q]
    k = [x * lax.rsqrt(jnp.sum(x * x, axis=-1, keepdims=True) + NORM_EPS) for x in k]
    beta = [bg[hcc(i)[0]][:, hcc(i)[1]:hcc(i)[1] + 1] for i in heads]
    gcc = [gc_cols[hcc(i)[0]][:, N_HEADS + hcc(i)[1]:N_HEADS + hcc(i)[1] + 1] for i in heads]
    kb = [k[i] * beta[i] for i in heads]
    kbf = [bf(x) for x in k]
    kk = [block_rows(kbf[ev(j)], kbf[ev(j) + 1]) for j in pairs]
    gcc2 = [jnp.where(odd, gcc[ev(j) + 1], gcc[ev(j)]) for j in pairs]
    gcr2 = [gc_rows[j // nh2][nh2 + j % nh2:nh2 + j % nh2 + 1, :] for j in pairs]
    decay = [jnp.exp(jnp.where(incl, gcc2[j] - gcr2[j], -jnp.inf)) for j in pairs]
    a = [jnp.where(strict,
                   _dot_nt(jnp.concatenate([bf(kb[ev(j)]), bf(kb[ev(j) + 1])], axis=1), kk[j]) * decay[j], 0.0)
         for j in pairs]
    qk = [_dot_nt(jnp.concatenate([bf(q[ev(j)]), bf(q[ev(j) + 1])], axis=1), kk[j]) * decay[j] for j in pairs]

    ad = [jnp.where(bd8, x, 0.0) for x in a]
    adb = [bf(x) for x in ad]
    a2 = [_dot(adb[j], block_diag(ad[j])) for j in pairs]
    a2d = [block_diag(x) for x in a2]
    a3 = [_dot(adb[j], a2d[j]) for j in pairs]
    a4 = [_dot(bf(a2[j]), a2d[j]) for j in pairs]
    t = [eye - ad[j] + a2[j] - a3[j] for j in pairs]
    t = [t[j] + _dot(bf(t[j]), block_diag(a4[j])) for j in pairs]
    for m in lvl:
        x = [_dot(bf(jnp.where(m, a[j], 0.0)), block_diag(t[j])) for j in pairs]
        t = [t[j] - _dot(bf(t[j]), block_diag(x[j])) for j in pairs]

    egc = [jnp.exp(x) for x in gcc]
    rhs = [bf(jnp.concatenate([v[i] * beta[i], kb[i] * egc[i]], axis=1)) for i in heads]
    sol2 = [_dot(bf(t[j]), block_rows(rhs[ev(j)], rhs[ev(j) + 1])) for j in pairs]
    sol = [sol2[(i // N_HEADS) * nh2 + (i % N_HEADS) // 2][:, (i % 2) * 2 * HEAD_DIM:(i % 2 + 1) * 2 * HEAD_DIM]
           for i in heads]
    g_last = [x[ck - 1:ck, :] for x in gcc]
    k_dec = [bf(k[i] * jnp.exp(g_last[i] - gcc[i])) for i in heads]
    wq_lhs = [bf(jnp.concatenate([sol[i][:, HEAD_DIM:], q[i] * egc[i]], axis=0)) for i in heads]
    qkb = [bf(x) for x in qk]

    s = [s_ref[h] for h in range(N_HEADS)]
    for cc in range(cps):
        hs = range(N_HEADS)
        u0 = cc * N_HEADS
        sb = [bf(x) for x in s]
        wq = [_dot(wq_lhs[u0 + h], sb[h]) for h in hs]
        vb = [bf(sol[u0 + h][:, :HEAD_DIM] - wq[h][:ck]) for h in hs]
        o2 = [_dot(qkb[cc * nh2 + p], block_rows(vb[2 * p], vb[2 * p + 1])) for p in range(nh2)]
        s = [s[h] * jnp.exp(g_last[u0 + h]) + _dot_tn(k_dec[u0 + h], vb[h]) for h in hs]
        for h in hs:
            o = wq[h][ck:] + o2[h // 2][:, (h % 2) * HEAD_DIM:(h % 2 + 1) * HEAD_DIM]
            on = o * lax.rsqrt(jnp.mean(o * o, axis=-1, keepdims=True) + NORM_EPS) * ong_ref[...]
            lanes = slice(h * HEAD_DIM, (h + 1) * HEAD_DIM)
            o_ref[0, cc * ck:(cc + 1) * ck, lanes] = (
                on * gsil_ref[0, cc * ck:(cc + 1) * ck, lanes].astype(F32)).astype(BF16)
    for h in range(N_HEADS):
        s_ref[h] = s[h]

    xb_ref[0:_SHORT_PAD, :] = xb_ref[rows:rows + _SHORT_PAD, :]

    @pl.when(c == n_steps - 1)
    def _():
        sfin_ref[0] = s_ref[...]


def _delta_branch(qkv, cache, bg, bgt, gsil, s0, p):
    b, t, qkv_dim = qkv.shape
    assert t % CHUNK == 0
    nc = t // CHUNK
    cps = _DELTA_CPS if nc % _DELTA_CPS == 0 else 1
    n_steps = nc // cps
    rows = cps * CHUNK
    sw = p["w_short"].shape[0]
    cache_p = jnp.pad(cache.astype(F32), ((0, 0), (_SHORT_PAD - (sw - 1), 0), (0, 0)))
    val_dim = N_HEADS * HEAD_DIM
    bgt3 = bgt.reshape(2 * N_HEADS, b * nc, CHUNK).transpose(1, 0, 2).reshape(b * nc, N_HEADS, 2 * CHUNK)
    return pl.pallas_call(
        functools.partial(_delta_kernel, n_steps, cps),
        grid=(b, n_steps),
        in_specs=[
            pl.BlockSpec((1, rows, qkv_dim), lambda i, j: (i, j, 0)),
            pl.BlockSpec((1, _SHORT_PAD, qkv_dim), lambda i, j: (i, 0, 0)),
            pl.BlockSpec((sw, qkv_dim), lambda i, j: (0, 0)),
            pl.BlockSpec((1, rows, LANES), lambda i, j: (i, j, 0)),
            pl.BlockSpec((cps, N_HEADS, 2 * CHUNK), lambda i, j: (i * n_steps + j, 0, 0)),
            pl.BlockSpec((1, rows, val_dim), lambda i, j: (i, j, 0)),
            pl.BlockSpec((1, HEAD_DIM), lambda i, j: (0, 0)),
            pl.BlockSpec((1, N_HEADS, HEAD_DIM, HEAD_DIM), lambda i, j: (i, 0, 0, 0)),
        ],
        out_specs=[
            pl.BlockSpec((1, rows, val_dim), lambda i, j: (i, j, 0)),
            pl.BlockSpec((1, N_HEADS, HEAD_DIM, HEAD_DIM), lambda i, j: (i, 0, 0, 0)),
        ],
        out_shape=[
            jax.ShapeDtypeStruct((b, t, val_dim), BF16),
            jax.ShapeDtypeStruct((b, N_HEADS, HEAD_DIM, HEAD_DIM), F32),
        ],
        scratch_shapes=[pltpu.VMEM((_SHORT_PAD + rows, qkv_dim), F32),
                        pltpu.VMEM((N_HEADS, HEAD_DIM, HEAD_DIM), F32)],
        compiler_params=pltpu.CompilerParams(
            dimension_semantics=("arbitrary", "arbitrary"), vmem_limit_bytes=VMEM_LIMIT),
        name="delta_rule",
    )(qkv, cache_p, p["w_short"], bg.reshape(b, t, LANES), bgt3, gsil, p["o_norm_g"], s0.astype(F32))


_R_E1, _R_E2, _R_RANK1, _R_RANK2, _R_W1, _R_W2 = range(6)


def _merge_kernel(alpha, convg_ref, og_ref, sgd_ref, x_ref, wo_ref, wout_ref, g_ref, b_ref,
                  wr_ref, br_ref, h_ref, hp_ref, route_ref, routet_ref, cnt_ref, carry_ref):
    @pl.when(pl.program_id(0) == 0)
    def _():
        carry_ref[...] = jnp.zeros_like(carry_ref)

    d_out = _dot(og_ref[...], wo_ref[...])
    merged = convg_ref[...].astype(F32) + d_out * sgd_ref[...].astype(F32)
    mix = _dot(merged.astype(BF16), wout_ref[...])
    h = _layer_norm(alpha * x_ref[...] + mix, g_ref[...], b_ref[...])
    h_ref[...] = h
    half_d = h.shape[1] // 2
    hp_ref[...] = _pack_pair(h[:, :half_d], h[:, half_d:])

    h_hi, h_mid, _ = _split3(h)
    w_hi, w_mid, _ = _split3(wr_ref[...])
    logits = _dot(h_hi, w_hi) + _dot(h_mid, w_hi) + _dot(h_hi, w_mid) + br_ref[...]
    tm = logits.shape[0]
    col = lax.broadcasted_iota(jnp.int32, logits.shape, 1).astype(F32)
    big = float(LANES)
    is_g = col < N_GROUPS
    mg = jnp.max(jnp.where(is_g, logits, -jnp.inf), axis=-1, keepdims=True)
    sg = jnp.sum(jnp.where(is_g, jnp.exp(jnp.where(is_g, logits, mg) - mg), 0.0), axis=-1, keepdims=True)
    pg_top = 1.0 / sg
    gidx = jnp.min(jnp.where(is_g & (logits == mg), col, big), axis=-1, keepdims=True)
    lo = ROUTE_COL0 + EXP_PER_GROUP * gidx
    sel = (col >= lo) & (col < lo + EXP_PER_GROUP)
    le = jnp.where(sel, logits, -jnp.inf)
    m1 = jnp.max(le, axis=-1, keepdims=True)
    i1 = jnp.min(jnp.where(le == m1, col, big), axis=-1, keepdims=True)
    le2 = jnp.where(col == i1, -jnp.inf, le)
    m2 = jnp.max(le2, axis=-1, keepdims=True)
    i2 = jnp.min(jnp.where(le2 == m2, col, big), axis=-1, keepdims=True)
    e2 = jnp.exp(m2 - m1)
    den = 1.0 + e2
    w1 = pg_top / den
    w2 = pg_top * e2 / den

    hit1 = col == i1
    hit2 = col == i2
    member = jnp.where(hit1 | hit2, 1.0, 0.0)
    ri = lax.broadcasted_iota(jnp.int32, (tm, tm), 0)
    ci = lax.broadcasted_iota(jnp.int32, (tm, tm), 1)
    earlier = jnp.where(ri > ci, 1.0, 0.0).astype(BF16)
    before = _dot(earlier, member.astype(BF16)) + carry_ref[...]
    rank1 = jnp.sum(jnp.where(hit1, before, 0.0), axis=-1, keepdims=True)
    rank2 = jnp.sum(jnp.where(hit2, before, 0.0), axis=-1, keepdims=True)
    carry_ref[...] += jnp.sum(member, axis=0, keepdims=True)
    cnt_ref[...] = jnp.broadcast_to(carry_ref[...], cnt_ref.shape)

    fields = (i1 - ROUTE_COL0, i2 - ROUTE_COL0, rank1, rank2, w1, w2)
    route = jnp.zeros_like(logits)
    for c, val in enumerate(fields):
        route = jnp.where(col == float(c), val, route)
    route_ref[...] = route
    sr = lax.broadcasted_iota(jnp.int32, (8, LANES), 0)
    sc = lax.broadcasted_iota(jnp.int32, (8, LANES), 1)
    pick = jnp.where(sr == sc, 1.0, 0.0).astype(BF16)
    routet_ref[...] = sum(_dot_nt(pick, part) for part in _split3(route))


def _merge(convg, og, sgd, x, p, alpha, tm):
    n, d = x.shape
    val_dim = og.shape[1]
    assert n % tm == 0
    row = lambda w: pl.BlockSpec((tm, w), lambda i: (i, 0))
    full = lambda shape: pl.BlockSpec(shape, lambda i: (0, 0))
    const = lambda shape: pl.BlockSpec(shape, lambda i: (0, 0), pipeline_mode=pl.Buffered(1))
    return pl.pallas_call(
        functools.partial(_merge_kernel, alpha),
        grid=(n // tm,),
        in_specs=[row(d), row(val_dim), row(d), row(d), const((val_dim, d)), const((d, d)),
                  full((1, d)), full((1, d)), const((d, LANES)), full((1, LANES))],
        out_specs=[row(d), row(d // 2), row(LANES), pl.BlockSpec((8, tm), lambda i: (0, i)), full((8, LANES))],
        out_shape=[jax.ShapeDtypeStruct((n, d), F32), jax.ShapeDtypeStruct((n, d // 2), jnp.uint32),
                   jax.ShapeDtypeStruct((n, LANES), F32),
                   jax.ShapeDtypeStruct((8, n), F32), jax.ShapeDtypeStruct((8, LANES), F32)],
        scratch_shapes=[pltpu.VMEM((1, LANES), F32)],
        compiler_params=pltpu.CompilerParams(
            dimension_semantics=("arbitrary",), vmem_limit_bytes=VMEM_LIMIT),
        name="merge_outproj",
    )(convg, og, sgd, x, p["w_o"], p["w_out"], p["ln1_g"], p["ln1_b"], p["w_router"], p["b_router"])


def _route_plan(route_t, cnt, te):
    n = route_t.shape[1]
    i32 = jnp.int32
    ri = route_t[:4].astype(i32)
    counts = cnt[0, ROUTE_COL0:ROUTE_COL0 + N_EXPERTS].astype(i32)
    ends = jnp.cumsum(counts)
    starts = ends - counts
    eids = jnp.arange(N_EXPERTS, dtype=i32)

    def lookup(table, idx):
        return jnp.sum(jnp.where(idx[None, :] == eids[:, None], table[:, None], 0), axis=0)

    dest = jnp.stack([lookup(starts, ri[0]) + ri[2], lookup(starts, ri[1]) + ri[3]])

    first_tile = starts // te
    last_tile = (ends - 1) // te
    items_e = jnp.where(counts > 0, last_tile - first_tile + 1, 0)
    item_end = jnp.cumsum(items_e)
    item_start = item_end - items_e
    total = item_end[-1]
    n_items = (2 * n) // te + N_EXPERTS - 1
    w = jnp.minimum(jnp.arange(n_items, dtype=i32), total - 1)
    item_e = jnp.sum((item_end[:, None] <= w[None, :]).astype(i32), axis=0)
    item_tile = lookup(first_tile, item_e) + w - lookup(item_start, item_e)
    lo = jnp.clip(lookup(starts, item_e) - item_tile * te, 0, te)
    hi = jnp.clip(lookup(ends, item_e) - item_tile * te, 0, te)
    return dest, (item_tile, item_e, lo, hi, total.reshape(1))


def _dispatch_kernel(d_ref, h_ref, xs_hbm, sem):
    tm = d_ref.shape[-1]

    def row_copy(r, dst):
        return pltpu.make_async_copy(h_ref.at[pl.ds(r, 1)], xs_hbm.at[pl.ds(dst, 1)], sem.at[0])

    def wait(r, carry):
        row_copy(r, 0).wait()
        row_copy(r, 0).wait()
        return carry

    for r in range(tm):
        row_copy(r, d_ref[0, 0, 0, r]).start()
        row_copy(r, d_ref[1, 0, 0, r]).start()
    lax.fori_loop(0, tm, wait, 0, unroll=8)


def _dispatch(h, dest, tm):
    n, d = h.shape
    assert n % tm == 0
    n_steps = n // tm
    return pl.pallas_call(
        _dispatch_kernel,
        grid=(n_steps,),
        in_specs=[pl.BlockSpec((2, 1, 1, tm), lambda i: (0, i, 0, 0), memory_space=pltpu.SMEM),
                  pl.BlockSpec((tm, d), lambda i: (i, 0))],
        out_specs=pl.BlockSpec(memory_space=pl.ANY),
        out_shape=jax.ShapeDtypeStruct((2 * n, d), h.dtype),
        scratch_shapes=[pltpu.SemaphoreType.DMA((1,))],
        compiler_params=pltpu.CompilerParams(
            dimension_semantics=("arbitrary",), vmem_limit_bytes=VMEM_LIMIT),
        name="moe_dispatch",
    )(dest.reshape(2, n_steps, 1, tm), h)


def _expert_kernel(tile_ref, exp_ref, lo_ref, hi_ref, tot_ref, xs_ref, wg_ref, wu_ref, wd_ref,
                   out_ref, wgb_ref, wub_ref, wdb_ref):
    w = pl.program_id(0)
    prev = jnp.maximum(w - 1, 0)
    live = w < tot_ref[0]
    new_expert = (w == 0) | (exp_ref[w] != exp_ref[prev])
    first_of_tile = (w == 0) | (tile_ref[w] != tile_ref[prev])

    @pl.when(live & new_expert)
    def _():
        wgb_ref[...] = wg_ref[0].astype(BF16)
        wub_ref[...] = wu_ref[0].astype(BF16)
        wdb_ref[...] = wd_ref[0].astype(BF16)

    @pl.when(live)
    def _():
        x_lo, x_hi = _unpack_pair(xs_ref[...])
        x = jnp.concatenate([x_lo.astype(BF16), x_hi.astype(BF16)], axis=1)
        hg = _dot(x, wgb_ref[...])
        hu = _dot(x, wub_ref[...])
        row = lax.broadcasted_iota(jnp.int32, (x.shape[0], 1), 0)
        mine = (row >= lo_ref[w]) & (row < hi_ref[w])
        act = jnp.where(mine, _silu(hg) * hu, 0.0).astype(BF16)
        part = _dot(act, wdb_ref[...])
        half_d = part.shape[1] // 2
        packed = _pack_pair(part[:, :half_d], part[:, half_d:])

        @pl.when(first_of_tile)
        def _():
            out_ref[...] = packed

        @pl.when(jnp.logical_not(first_of_tile))
        def _():
            out_ref[...] = jnp.where(mine, packed, out_ref[...])


def _experts(xs, items, p, te):
    rows, dp = xs.shape
    ne, d, f = p["w_gate"].shape
    assert d == 2 * dp
    item_tile, item_e, lo, hi, total = items
    n_items = item_tile.shape[0]
    tile_map = lambda w, t_, e_, lo_, hi_, n_: (t_[w], 0)
    exp_map = lambda w, t_, e_, lo_, hi_, n_: (e_[w], 0, 0)
    grid_spec = pltpu.PrefetchScalarGridSpec(
        num_scalar_prefetch=5,
        grid=(n_items,),
        in_specs=[
            pl.BlockSpec((te, dp), tile_map),
            pl.BlockSpec((1, d, f), exp_map),
            pl.BlockSpec((1, d, f), exp_map),
            pl.BlockSpec((1, f, d), exp_map),
        ],
        out_specs=pl.BlockSpec((te, dp), tile_map),
        scratch_shapes=[pltpu.VMEM((d, f), BF16), pltpu.VMEM((d, f), BF16), pltpu.VMEM((f, d), BF16)],
    )
    return pl.pallas_call(
        _expert_kernel,
        grid_spec=grid_spec,
        out_shape=jax.ShapeDtypeStruct((rows, dp), jnp.uint32),
        compiler_params=pltpu.CompilerParams(
            dimension_semantics=("arbitrary",), vmem_limit_bytes=VMEM_LIMIT),
        name="moe_experts",
    )(item_tile, item_e, lo, hi, total, xs, p["w_gate"], p["w_up"], p["w_down"])


def _combine_kernel(alpha, d_ref, dn_ref, h_ref, route_ref, rows_hbm, g_ref, b_ref, y_ref, o_ref, sem):
    i = pl.program_id(0)
    n_i = pl.num_programs(0)
    tm = h_ref.shape[0]
    slot = i % 2

    def row_copy(src_row, slot_, k, r):
        return pltpu.make_async_copy(rows_hbm.at[pl.ds(src_row, 1)], o_ref.at[slot_, k, pl.ds(r, 1)],
                                     sem.at[slot_])

    def start(dref, slot_):
        for r in range(tm):
            row_copy(dref[0, 0, 0, r], slot_, 0, r).start()
            row_copy(dref[1, 0, 0, r], slot_, 1, r).start()

    def wait(slot_):
        def body(r, carry):
            row_copy(0, slot_, 0, r).wait()
            row_copy(0, slot_, 1, r).wait()
            return carry
        lax.fori_loop(0, tm, body, 0, unroll=8)

    @pl.when(i == 0)
    def _():
        start(d_ref, 0)

    for nxt in (0, 1):
        @pl.when((i + 1 < n_i) & (slot == 1 - nxt))
        def _():
            start(dn_ref, nxt)

    wait(slot)
    route = route_ref[...]
    col = lax.broadcasted_iota(jnp.int32, route.shape, 1)
    w1 = jnp.sum(jnp.where(col == _R_W1, route, 0.0), axis=-1, keepdims=True)
    w2 = jnp.sum(jnp.where(col == _R_W2, route, 0.0), axis=-1, keepdims=True)
    a_lo, a_hi = _unpack_pair(o_ref[slot, 0])
    b_lo, b_hi = _unpack_pair(o_ref[slot, 1])
    moe = jnp.concatenate([w1 * a_lo + w2 * b_lo, w1 * a_hi + w2 * b_hi], axis=1)
    y_ref[...] = _layer_norm(alpha * h_ref[...] + moe, g_ref[...], b_ref[...])


def _combine(h, route, rows, dest, p, alpha, tm):
    n, d = h.shape
    assert n % tm == 0
    n_i = n // tm
    dest4 = dest.reshape(2, n_i, 1, tm)
    cur = pl.BlockSpec((2, 1, 1, tm), lambda i: (0, i, 0, 0), memory_space=pltpu.SMEM)
    nxt = pl.BlockSpec((2, 1, 1, tm), lambda i: (0, jnp.minimum(i + 1, n_i - 1), 0, 0),
                       memory_space=pltpu.SMEM)
    return pl.pallas_call(
        functools.partial(_combine_kernel, alpha),
        grid=(n_i,),
        in_specs=[cur, nxt,
                  pl.BlockSpec((tm, d), lambda i: (i, 0)),
                  pl.BlockSpec((tm, LANES), lambda i: (i, 0)),
                  pl.BlockSpec(memory_space=pl.ANY),
                  pl.BlockSpec((1, d), lambda i: (0, 0)),
                  pl.BlockSpec((1, d), lambda i: (0, 0))],
        out_specs=pl.BlockSpec((tm, d), lambda i: (i, 0)),
        out_shape=jax.ShapeDtypeStruct((n, d), F32),
        scratch_shapes=[pltpu.VMEM((2, 2, tm, d // 2), jnp.uint32), pltpu.SemaphoreType.DMA((2,))],
        compiler_params=pltpu.CompilerParams(
            dimension_semantics=("arbitrary",), vmem_limit_bytes=VMEM_LIMIT),
        name="moe_combine",
    )(dest4, dest4, h, route, rows, p["ln2_g"], p["ln2_b"])


def _pack_layer(w_in, b_in, w_dw, b_dw, lnc_g, lnc_b, w_conv_out, w_short, a_log, dt_bias, o_norm_g,
                w_o, w_out, ln1_g, ln1_b, w_rg, b_rg, w_re, b_re, w_gate, w_up, w_down, ln2_g, ln2_b):
    d = w_in.shape[0]
    c_conv = w_dw.shape[1]
    qkv_dim = w_short.shape[1]
    o_qkv = 2 * c_conv
    o_ba = o_qkv + qkv_dim
    o_tail = o_ba + 2 * N_HEADS

    def pad_cols(a, width):
        return jnp.pad(a, ((0, 0), (0, width - a.shape[1])))

    b2 = b_in[None, :].astype(F32)
    nh = N_HEADS
    zeros_h = jnp.zeros((nh,), F32)
    head_params = jnp.stack([jnp.concatenate([zeros_h, a_log.astype(F32)]),
                             jnp.concatenate([zeros_h, dt_bias.astype(F32)])])
    half = _TN // 2
    w_main = jnp.concatenate([w_in[:, :o_ba], w_in[:, o_tail:]], axis=1).astype(BF16)
    w_blk = w_main.reshape(d, -1, half).transpose(1, 0, 2)
    w_ba = w_in[:, o_ba:o_tail].astype(BF16)
    w_router = jnp.concatenate([w_rg, w_re], axis=1).astype(F32)
    b_router = jnp.concatenate([b_rg, b_re])[None, :].astype(F32)
    return dict(
        c_conv=c_conv, qkv_dim=qkv_dim,
        w_blk=w_blk, b_in=b2,
        w_ba=pad_cols(w_ba, LANES),
        b_ba=pad_cols(b2[:, o_ba:o_tail], LANES),
        w_bat=w_ba.T,
        b_bat=jnp.broadcast_to(b_in[o_ba:o_tail, None].astype(F32), (2 * nh, LANES)),
        b_tail=b2[:, o_tail:],
        p_row=pad_cols(head_params, LANES),
        p_col=pad_cols(head_params.T, LANES),
        w_dw=w_dw.astype(F32), b_dw=b_dw[None, :].astype(F32),
        lnc_g=lnc_g[None, :].astype(F32), lnc_b=lnc_b[None, :].astype(F32),
        w_conv_out=w_conv_out.astype(BF16),
        w_short=w_short.astype(F32),
        o_norm_g=o_norm_g[None, :].astype(F32),
        w_o=w_o.astype(BF16), w_out=w_out.astype(BF16),
        ln1_g=ln1_g[None, :].astype(F32), ln1_b=ln1_b[None, :].astype(F32),
        w_router=pad_cols(w_router, LANES), b_router=pad_cols(b_router, LANES),
        w_gate=w_gate.astype(F32), w_up=w_up.astype(F32), w_down=w_down.astype(F32),
        ln2_g=ln2_g[None, :].astype(F32), ln2_b=ln2_b[None, :].astype(F32),
    )


_MERGE_TM = 512
_COMBINE_TM = 256
_EXPERT_TE = 256
_DISPATCH_TM = 256


def _block(x, conv_buf, short_buf, s0, p, alpha, tm, conv_tt):
    b, t, d = x.shape
    n = b * t
    x2 = x.reshape(n, d).astype(F32)
    u, qkv, bg, bgt, gsil, sgc, sgd = _inproj(x2, p, tm)
    c_conv = u.shape[1]
    u3 = u.reshape(b, t, c_conv)
    qkv3 = qkv.reshape(b, t, -1)
    convg = _conv_branch(u3, conv_buf, sgc.reshape(b, t, d), p, conv_tt)
    og, s_new = _delta_branch(qkv3, short_buf, bg, bgt, gsil.reshape(b, t, -1), s0, p)
    h, hp, route, route_t, cnt = _merge(convg.reshape(n, d), og.reshape(n, -1), sgd, x2, p, alpha, min(_MERGE_TM, n))
    dest, items = _route_plan(route_t, cnt, _EXPERT_TE)
    xs = _dispatch(hp, dest, min(_DISPATCH_TM, n))
    rows = _experts(xs, items, p, _EXPERT_TE)
    y = _combine(h, route, rows, dest, p, alpha, min(_COMBINE_TM, n))
    kc = conv_buf.shape[1]
    ks = short_buf.shape[1]
    assert t >= kc and t >= ks
    return (y.reshape(b, t, d).astype(x.dtype), u3[:, t - kc:].astype(x.dtype),
            qkv3[:, t - ks:].astype(x.dtype), s_new.astype(s0.dtype))


def kernel(x_prompt, x_sample, cache_conv, cache_short, state_delta, w_in, b_in, w_dw, b_dw, lnc_g, lnc_b, w_conv_out, w_short, a_log, dt_bias, o_norm_g, w_o, w_out, ln1_g, ln1_b, w_rg, b_rg, w_re, b_re, w_gate, w_up, w_down, ln2_g, ln2_b):
    weights = (w_in, b_in, w_dw, b_dw, lnc_g, lnc_b, w_conv_out, w_short, a_log, dt_bias, o_norm_g,
               w_o, w_out, ln1_g, ln1_b, w_rg, b_rg, w_re, b_re, w_gate, w_up, w_down, ln2_g, ln2_b)
    depth = w_in.shape[0]
    alpha = (2.0 * depth) ** 0.25
    yp, ys = x_prompt, x_sample
    bp = x_prompt.shape[0]
    outs = [[] for _ in range(6)]
    for l in range(depth):
        p = _pack_layer(*(wt[l] for wt in weights))
        zc = jnp.zeros((bp,) + cache_conv.shape[2:], x_prompt.dtype)
        zs = jnp.zeros((bp,) + cache_short.shape[2:], x_prompt.dtype)
        zd = jnp.zeros((bp,) + state_delta.shape[2:], state_delta.dtype)
        yp, c, s, dl = _block(yp, zc, zs, zd, p, alpha, 1024, 256)
        outs[0].append(c), outs[1].append(s), outs[2].append(dl)
        ys, c, s, dl = _block(ys, cache_conv[l], cache_short[l], state_delta[l], p, alpha, 1024, 64)
        outs[3].append(c), outs[4].append(s), outs[5].append(dl)
    return (yp, ys) + tuple(jnp.stack(o) for o in outs)
```

```python
import functools

import jax
import jax.numpy as jnp
from jax import lax
from jax.experimental import pallas as pl
from jax.experimental.pallas import tpu as pltpu

F32 = jnp.float32
BF16 = jnp.bfloat16

CHUNK = 64
N_HEADS = 8
HEAD_DIM = 128
N_GROUPS = 4
EXP_PER_GROUP = 8
N_EXPERTS = N_GROUPS * EXP_PER_GROUP
LN_EPS = 1e-5
NORM_EPS = 1e-6
LANES = 128
ROUTE_COL0 = N_GROUPS
VMEM_LIMIT = 56 * 1024 * 1024


def _dot(a, b):
    return jnp.dot(a, b, preferred_element_type=F32)


def _dot_nt(a, b):
    return lax.dot_general(a, b, (((1,), (1,)), ((), ())), preferred_element_type=F32)


def _dot_tn(a, b):
    return lax.dot_general(a, b, (((0,), (0,)), ((), ())), preferred_element_type=F32)


def _split3(x):
    hi = x.astype(BF16)
    r1 = x - hi.astype(F32)
    mid = r1.astype(BF16)
    lo = (r1 - mid.astype(F32)).astype(BF16)
    return hi, mid, lo


def _pack_pair(lo, hi):
    lo_bits = lax.bitcast_convert_type(lo.astype(BF16).astype(F32), jnp.uint32)
    hi_bits = lax.bitcast_convert_type(hi.astype(BF16).astype(F32), jnp.uint32)
    return (hi_bits & jnp.uint32(0xFFFF0000)) | (lo_bits >> 16)


def _unpack_pair(packed):
    lo = lax.bitcast_convert_type(packed << 16, F32)
    hi = lax.bitcast_convert_type(packed & jnp.uint32(0xFFFF0000), F32)
    return lo, hi


def _sigmoid(x):
    return 1.0 / (1.0 + jnp.exp(-x))


def _silu(x):
    return x * _sigmoid(x)


def _softplus(x):
    return jnp.maximum(x, 0.0) + jnp.log1p(jnp.exp(-jnp.abs(x)))


def _layer_norm(x, g, b):
    mu = jnp.mean(x, axis=-1, keepdims=True)
    xc = x - mu
    var = jnp.mean(xc * xc, axis=-1, keepdims=True)
    return xc * lax.rsqrt(var + LN_EPS) * g + b


def _clamp(v, lo, hi):
    return jnp.minimum(jnp.maximum(v, lo), hi)


_TN = 512
_J_GLU, _J_QKV, _J_BA, _J_GO, _J_GC, _J_GD, _J_END = 0, 4, 10, 11, 13, 17, 21


def _wide_dot(x, w_ref):
    return jnp.concatenate([_dot(x, w_ref[b]) for b in range(w_ref.shape[0])], axis=1)


def _inproj_kernel(x_ref, wga_ref, wgb_ref, wqkv_ref, wba_ref, wbat_ref, wtail_ref,
                   bga_ref, bgb_ref, bqkv_ref, bba_ref, bbat_ref, btail_ref, prow_ref, pcol_ref,
                   u_ref, qkv_ref, bg_ref, bgt_ref, gsil_ref, sgc_ref, sgd_ref, xb_ref):
    j = pl.program_id(1)

    @pl.when(j == 0)
    def _():
        xb_ref[...] = x_ref[...].astype(BF16)

    @pl.when(j < _J_QKV)
    def _():
        xb = xb_ref[...]
        value = _dot(xb, wga_ref[0]) + bga_ref[...]
        gate = _dot(xb, wgb_ref[0]) + bgb_ref[...]
        u_ref[...] = value * _sigmoid(gate)

    @pl.when((j >= _J_QKV) & (j < _J_BA))
    def _():
        qkv_ref[...] = _wide_dot(xb_ref[...], wqkv_ref) + bqkv_ref[...]

    @pl.when(j == _J_BA)
    def _():
        xb = xb_ref[...]
        z = _dot(xb, wba_ref[...]) + bba_ref[...]
        col = lax.broadcasted_iota(jnp.int32, z.shape, 1)
        g = -jnp.exp(prow_ref[0:1, :]) * _softplus(z + prow_ref[1:2, :])
        bg_ref[...] = jnp.where(col < N_HEADS, _sigmoid(z), g)
        zt = _dot_nt(wbat_ref[...], xb) + bbat_ref[:, 0:1]
        row = lax.broadcasted_iota(jnp.int32, zt.shape, 0)
        gt = -jnp.exp(pcol_ref[:, 0:1]) * _softplus(zt + pcol_ref[:, 1:2])
        bgt_ref[...] = jnp.where(row < N_HEADS, _sigmoid(zt), gt)

    @pl.when((j >= _J_GO) & (j < _J_GC))
    def _():
        z = _wide_dot(xb_ref[...], wtail_ref) + btail_ref[...]
        gsil_ref[...] = _silu(z).astype(BF16)

    @pl.when((j >= _J_GC) & (j < _J_GD))
    def _():
        z = _wide_dot(xb_ref[...], wtail_ref) + btail_ref[...]
        sgc_ref[...] = _sigmoid(z).astype(BF16)

    @pl.when(j >= _J_GD)
    def _():
        z = _wide_dot(xb_ref[...], wtail_ref) + btail_ref[...]
        sgd_ref[...] = _sigmoid(z).astype(BF16)


def _inproj(x, pk, tm):
    n, d = x.shape
    c_conv, qkv_dim = pk["c_conv"], pk["qkv_dim"]
    val_dim = N_HEADS * HEAD_DIM
    half = _TN // 2
    n_glu = _J_QKV - _J_GLU
    assert n % tm == 0 and c_conv == n_glu * half and (2 * c_conv) % _TN == 0
    qkv_blk0 = 2 * c_conv // _TN
    tail_blk0 = (2 * c_conv + qkv_dim) // _TN
    assert (2 * c_conv + qkv_dim) % _TN == 0

    def cm(lo, hi):
        return lambda i, j: (0, _clamp(j - lo, 0, hi - lo - 1))

    def om(lo, hi):
        return lambda i, j: (i, _clamp(j - lo, 0, hi - lo - 1))

    in_specs = [
        pl.BlockSpec((tm, d), lambda i, j: (i, 0)),
        pl.BlockSpec((1, d, half), lambda i, j: (_clamp(j, 0, n_glu - 1), 0, 0)),
        pl.BlockSpec((1, d, half), lambda i, j: (n_glu + _clamp(j, 0, n_glu - 1), 0, 0)),
        pl.BlockSpec((2, d, half), lambda i, j: (qkv_blk0 + _clamp(j - _J_QKV, 0, _J_BA - _J_QKV - 1), 0, 0)),
        pl.BlockSpec((d, LANES), lambda i, j: (0, 0)),
        pl.BlockSpec((2 * N_HEADS, d), lambda i, j: (0, 0)),
        pl.BlockSpec((2, d, half), lambda i, j: (tail_blk0 + _clamp(j - _J_GO, 0, _J_END - _J_GO - 1), 0, 0)),
        pl.BlockSpec((1, half), lambda i, j: (0, _clamp(j, 0, n_glu - 1))),
        pl.BlockSpec((1, half), lambda i, j: (0, n_glu + _clamp(j, 0, n_glu - 1))),
        pl.BlockSpec((1, _TN), lambda i, j: (0, qkv_blk0 + _clamp(j - _J_QKV, 0, _J_BA - _J_QKV - 1))),
        pl.BlockSpec((1, LANES), lambda i, j: (0, 0)),
        pl.BlockSpec((2 * N_HEADS, LANES), lambda i, j: (0, 0)),
        pl.BlockSpec((1, _TN), cm(_J_GO, _J_END)),
        pl.BlockSpec((2, LANES), lambda i, j: (0, 0)),
        pl.BlockSpec((2 * N_HEADS, LANES), lambda i, j: (0, 0)),
    ]
    out_shape = [
        jax.ShapeDtypeStruct((n, c_conv), F32),
        jax.ShapeDtypeStruct((n, qkv_dim), F32),
        jax.ShapeDtypeStruct((n, LANES), F32),
        jax.ShapeDtypeStruct((2 * N_HEADS, n), F32),
        jax.ShapeDtypeStruct((n, val_dim), BF16),
        jax.ShapeDtypeStruct((n, d), BF16),
        jax.ShapeDtypeStruct((n, d), BF16),
    ]
    out_specs = [
        pl.BlockSpec((tm, _TN // 2), om(_J_GLU, _J_QKV)),
        pl.BlockSpec((tm, _TN), om(_J_QKV, _J_BA)),
        pl.BlockSpec((tm, LANES), lambda i, j: (i, 0)),
        pl.BlockSpec((2 * N_HEADS, tm), lambda i, j: (0, i)),
        pl.BlockSpec((tm, _TN), om(_J_GO, _J_GC)),
        pl.BlockSpec((tm, _TN), om(_J_GC, _J_GD)),
        pl.BlockSpec((tm, _TN), om(_J_GD, _J_END)),
    ]
    return pl.pallas_call(
        _inproj_kernel,
        grid=(n // tm, _J_END),
        in_specs=in_specs,
        out_specs=out_specs,
        out_shape=out_shape,
        scratch_shapes=[pltpu.VMEM((tm, d), BF16)],
        compiler_params=pltpu.CompilerParams(
            dimension_semantics=("arbitrary", "arbitrary"), vmem_limit_bytes=VMEM_LIMIT),
        name="inproj",
    )(x, pk["w_blk"], pk["w_blk"], pk["w_blk"], pk["w_ba"], pk["w_bat"], pk["w_blk"],
      pk["b_in"], pk["b_in"], pk["b_in"], pk["b_ba"], pk["b_bat"], pk["b_tail"], pk["p_row"], pk["p_col"])


_HALO = 32
_CONV_RB = 64
_CONV_FB = 8


def _conv_kernel(u_ref, cache_ref, wdw_ref, bdw_ref, lng_ref, lnb_ref, wco_ref, sgc_ref,
                 out_ref, xt_ref, yt_ref, cn_ref):
    t = pl.program_id(1)
    tt = u_ref.shape[1]
    c_conv = u_ref.shape[2]
    width = wdw_ref.shape[0]
    nfold = c_conv // LANES
    first = _HALO - (width - 1)

    @pl.when(t == 0)
    def _():
        for s in range(nfold):
            xt_ref[pl.ds(s, _HALO, stride=nfold), :] = cache_ref[0, :, s * LANES:(s + 1) * LANES]

    for s in range(nfold):
        xt_ref[pl.ds(_HALO * nfold + s, tt, stride=nfold), :] = u_ref[0, :, s * LANES:(s + 1) * LANES]

    def frames(ib, carry):
        f0 = ib * _CONV_FB
        acc = [None] * _CONV_FB
        for k in range(width):
            wk = wdw_ref[k]
            for j in range(_CONV_FB):
                row = pl.multiple_of((first + f0 + j + k) * nfold, nfold)
                term = wk * xt_ref[pl.ds(row, nfold), :]
                acc[j] = term if acc[j] is None else acc[j] + term
        for j in range(_CONV_FB):
            yt_ref[pl.ds(pl.multiple_of((f0 + j) * nfold, nfold), nfold), :] = acc[j]
        return carry

    lax.fori_loop(0, tt // _CONV_FB, frames, 0)

    for rb in range(tt // _CONV_RB):
        r0 = rb * _CONV_RB
        y = jnp.concatenate([yt_ref[pl.ds(r0 * nfold + s, _CONV_RB, stride=nfold), :] for s in range(nfold)],
                            axis=1)
        y = _layer_norm(y + bdw_ref[...], lng_ref[...], lnb_ref[...])
        cn_ref[r0:r0 + _CONV_RB, :] = _silu(y).astype(BF16)

    co = _dot(cn_ref[...], wco_ref[...])
    out_ref[0] = (co * sgc_ref[0].astype(F32)).astype(BF16)
    xt_ref[0:_HALO * nfold, :] = xt_ref[tt * nfold:(tt + _HALO) * nfold, :]


def _conv_branch(u, cache, sgc, p, tt):
    b, t, c_conv = u.shape
    d = sgc.shape[-1]
    width = p["w_dw"].shape[0]
    assert t % tt == 0 and tt % _CONV_RB == 0 and tt >= _HALO and width - 1 <= _HALO
    assert c_conv % (8 * LANES) == 0
    nfold = c_conv // LANES
    cache_p = jnp.pad(cache.astype(F32), ((0, 0), (_HALO - (width - 1), 0), (0, 0)))
    full2 = lambda shape: pl.BlockSpec(shape, lambda i, j: (0, 0))
    return pl.pallas_call(
        _conv_kernel,
        grid=(b, t // tt),
        in_specs=[
            pl.BlockSpec((1, tt, c_conv), lambda i, j: (i, j, 0)),
            pl.BlockSpec((1, _HALO, c_conv), lambda i, j: (i, 0, 0)),
            pl.BlockSpec((width, nfold, LANES), lambda i, j: (0, 0, 0)),
            full2((1, c_conv)), full2((1, c_conv)), full2((1, c_conv)),
            full2((c_conv, d)),
            pl.BlockSpec((1, tt, d), lambda i, j: (i, j, 0)),
        ],
        out_specs=pl.BlockSpec((1, tt, d), lambda i, j: (i, j, 0)),
        out_shape=jax.ShapeDtypeStruct((b, t, d), BF16),
        scratch_shapes=[pltpu.VMEM(((tt + _HALO) * nfold, LANES), F32),
                        pltpu.VMEM((tt * nfold, LANES), F32),
                        pltpu.VMEM((tt, c_conv), BF16)],
        compiler_params=pltpu.CompilerParams(
            dimension_semantics=("arbitrary", "arbitrary"), vmem_limit_bytes=VMEM_LIMIT),
        name="conv_branch",
    )(u, cache_p, p["w_dw"].reshape(width, nfold, LANES), p["b_dw"], p["lnc_g"], p["lnc_b"],
      p["w_conv_out"], sgc)


_SHORT_PAD = 8
_DELTA_CPS = 4


def _delta_kernel(n_steps, cps, qkv_ref, cache_ref, wsh_ref, bg_ref, bgt_ref, gsil_ref, ong_ref, s0_ref,
                  o_ref, sfin_ref, xb_ref, s_ref):
    c = pl.program_id(1)
    ck = CHUNK
    rows = cps * ck
    key_dim = N_HEADS * HEAD_DIM
    sw = wsh_ref.shape[0]

    @pl.when(c == 0)
    def _():
        s_ref[...] = s0_ref[0]
        xb_ref[0:_SHORT_PAD, :] = cache_ref[0]

    xb_ref[_SHORT_PAD:_SHORT_PAD + rows, :] = qkv_ref[0]

    def conv_cols(cc, lo):
        lanes = slice(lo, lo + HEAD_DIM)
        base = _SHORT_PAD + cc * ck
        acc = wsh_ref[sw - 1:sw, lanes] * xb_ref[base:base + ck, lanes]
        for k in range(sw - 1):
            r = base - (sw - 1) + k
            acc = acc + wsh_ref[k:k + 1, lanes] * xb_ref[r:r + ck, lanes]
        return _silu(acc)

    rowp = lax.broadcasted_iota(jnp.int32, (ck, 2 * ck), 0)
    lanep = lax.broadcasted_iota(jnp.int32, (ck, 2 * ck), 1)
    odd = lanep >= ck
    lcol = jnp.where(odd, lanep - ck, lanep)
    incl = rowp >= lcol
    strict = rowp > lcol
    eye = jnp.where(rowp == lcol, 1.0, 0.0).astype(F32)
    bd8 = (rowp // 8) == (lcol // 8)
    lvl = [((rowp // (2 * s)) == (lcol // (2 * s))) & ((rowp // s) != (lcol // s)) for s in (8, 16, 32)]
    ri = lax.broadcasted_iota(jnp.int32, (ck, ck), 0)
    ci = lax.broadcasted_iota(jnp.int32, (ck, ck), 1)
    tri_l = jnp.where(ri >= ci, 1.0, 0.0).astype(BF16)
    r2 = lax.broadcasted_iota(jnp.int32, (2 * ck, 2 * ck), 0)
    c2 = lax.broadcasted_iota(jnp.int32, (2 * ck, 2 * ck), 1)
    tri_u2 = jnp.where(((r2 >= ck) == (c2 >= ck)) & (r2 <= c2), 1.0, 0.0).astype(BF16)

    bf = lambda m: m.astype(BF16)

    def block_diag(pm):
        return bf(jnp.concatenate([jnp.where(odd, 0.0, pm), jnp.where(odd, pm, 0.0)], axis=0))

    def block_rows(top, bottom):
        z = jnp.zeros_like(top)
        return jnp.concatenate([jnp.concatenate([top, z], axis=1), jnp.concatenate([z, bottom], axis=1)], axis=0)

    bg = [bg_ref[0, cc * ck:(cc + 1) * ck, :] for cc in range(cps)]
    gc_cols = [sum(_dot(tri_l, part) for part in _split3(bg[cc])) for cc in range(cps)]
    gc_rows = [sum(_dot(part, tri_u2) for part in _split3(bgt_ref[cc])) for cc in range(cps)]

    nh2 = N_HEADS // 2
    heads = range(cps * N_HEADS)
    pairs = range(cps * nh2)
    hcc = lambda i: (i // N_HEADS, i % N_HEADS)
    ev = lambda j: (j // nh2) * N_HEADS + 2 * (j % nh2)
    q = [conv_cols(hcc(i)[0], hcc(i)[1] * HEAD_DIM) for i in heads]
    k = [conv_cols(hcc(i)[0], key_dim + hcc(i)[1] * HEAD_DIM) for i in heads]
    v = [conv_cols(hcc(i)[0], 2 * key_dim + hcc(i)[1] * HEAD_DIM) for i in heads]
    q = [x * lax.rsqrt(jnp.sum(x * x, axis=-1, keepdims=True) + NORM_EPS) * (HEAD_DIM ** -0.5) for x in q]
    k = [x * lax.rsqrt(jnp.sum(x * x, axis=-1, keepdims=True) + NORM_EPS) for x in k]
    beta = [bg[hcc(i)[0]][:, hcc(i)[1]:hcc(i)[1] + 1] for i in heads]
    gcc = [gc_cols[hcc(i)[0]][:, N_HEADS + hcc(i)[1]:N_HEADS + hcc(i)[1] + 1] for i in heads]
    kb = [k[i] * beta[i] for i in heads]
    kbf = [bf(x) for x in k]
    kk = [block_rows(kbf[ev(j)], kbf[ev(j) + 1]) for j in pairs]
    gcc2 = [jnp.where(odd, gcc[ev(j) + 1], gcc[ev(j)]) for j in pairs]
    gcr2 = [gc_rows[j // nh2][nh2 + j % nh2:nh2 + j % nh2 + 1, :] for j in pairs]
    decay = [jnp.exp(jnp.where(incl, gcc2[j] - gcr2[j], -jnp.inf)) for j in pairs]
    a = [jnp.where(strict,
                   _dot_nt(jnp.concatenate([bf(kb[ev(j)]), bf(kb[ev(j) + 1])], axis=1), kk[j]) * decay[j], 0.0)
         for j in pairs]
    qk = [_dot_nt(jnp.concatenate([bf(q[ev(j)]), bf(q[ev(j) + 1])], axis=1), kk[j]) * decay[j] for j in pairs]

    ad = [jnp.where(bd8, x, 0.0) for x in a]
    adb = [bf(x) for x in ad]
    a2 = [_dot(adb[j], block_diag(ad[j])) for j in pairs]
    a2d = [block_diag(x) for x in a2]
    a3 = [_dot(adb[j], a2d[j]) for j in pairs]
    a4 = [_dot(bf(a2[j]), a2d[j]) for j in pairs]
    t = [eye - ad[j] + a2[j] - a3[j] for j in pairs]
    t = [t[j] + _dot(bf(t[j]), block_diag(a4[j])) for j in pairs]
    for m in lvl:
        x = [_dot(bf(jnp.where(m, a[j], 0.0)), block_diag(t[j])) for j in pairs]
        t = [t[j] - _dot(bf(t[j]), block_diag(x[j])) for j in pairs]

    egc = [jnp.exp(x) for x in gcc]
    rhs = [bf(jnp.concatenate([v[i] * beta[i], kb[i] * egc[i]], axis=1)) for i in heads]
    sol2 = [_dot(bf(t[j]), block_rows(rhs[ev(j)], rhs[ev(j) + 1])) for j in pairs]
    sol = [sol2[(i // N_HEADS) * nh2 + (i % N_HEADS) // 2][:, (i % 2) * 2 * HEAD_DIM:(i % 2 + 1) * 2 * HEAD_DIM]
           for i in heads]
    g_last = [x[ck - 1:ck, :] for x in gcc]
    k_dec = [bf(k[i] * jnp.exp(g_last[i] - gcc[i])) for i in heads]
    wq_lhs = [bf(jnp.concatenate([sol[i][:, HEAD_DIM:], q[i] * egc[i]], axis=0)) for i in heads]
    qkb = [bf(x) for x in qk]

    s = [s_ref[h] for h in range(N_HEADS)]
    for cc in range(cps):
        hs = range(N_HEADS)
        u0 = cc * N_HEADS
        sb = [bf(x) for x in s]
        wq = [_dot(wq_lhs[u0 + h], sb[h]) for h in hs]
        vb = [bf(sol[u0 + h][:, :HEAD_DIM] - wq[h][:ck]) for h in hs]
        o2 = [_dot(qkb[cc * nh2 + p], block_rows(vb[2 * p], vb[2 * p + 1])) for p in range(nh2)]
        s = [s[h] * jnp.exp(g_last[u0 + h]) + _dot_tn(k_dec[u0 + h], vb[h]) for h in hs]
        for h in hs:
            o = wq[h][ck:] + o2[h // 2][:, (h % 2) * HEAD_DIM:(h % 2 + 1) * HEAD_DIM]
            on = o * lax.rsqrt(jnp.mean(o * o, axis=-1, keepdims=True) + NORM_EPS) * ong_ref[...]
            lanes = slice(h * HEAD_DIM, (h + 1) * HEAD_DIM)
            o_ref[0, cc * ck:(cc + 1) * ck, lanes] = (
                on * gsil_ref[0, cc * ck:(cc + 1) * ck, lanes].astype(F32)).astype(BF16)
    for h in range(N_HEADS):
        s_ref[h] = s[h]

    xb_ref[0:_SHORT_PAD, :] = xb_ref[rows:rows + _SHORT_PAD, :]

    @pl.when(c == n_steps - 1)
    def _():
        sfin_ref[0] = s_ref[...]


def _delta_branch(qkv, cache, bg, bgt, gsil, s0, p):
    b, t, qkv_dim = qkv.shape
    assert t % CHUNK == 0
    nc = t // CHUNK
    cps = _DELTA_CPS if nc % _DELTA_CPS == 0 else 1
    n_steps = nc // cps
    rows = cps * CHUNK
    sw = p["w_short"].shape[0]
    cache_p = jnp.pad(cache.astype(F32), ((0, 0), (_SHORT_PAD - (sw - 1), 0), (0, 0)))
    val_dim = N_HEADS * HEAD_DIM
    bgt3 = bgt.reshape(2 * N_HEADS, b * nc, CHUNK).transpose(1, 0, 2).reshape(b * nc, N_HEADS, 2 * CHUNK)
    return pl.pallas_call(
        functools.partial(_delta_kernel, n_steps, cps),
        grid=(b, n_steps),
        in_specs=[
            pl.BlockSpec((1, rows, qkv_dim), lambda i, j: (i, j, 0)),
            pl.BlockSpec((1, _SHORT_PAD, qkv_dim), lambda i, j: (i, 0, 0)),
            pl.BlockSpec((sw, qkv_dim), lambda i, j: (0, 0)),
            pl.BlockSpec((1, rows, LANES), lambda i, j: (i, j, 0)),
            pl.BlockSpec((cps, N_HEADS, 2 * CHUNK), lambda i, j: (i * n_steps + j, 0, 0)),
            pl.BlockSpec((1, rows, val_dim), lambda i, j: (i, j, 0)),
            pl.BlockSpec((1, HEAD_DIM), lambda i, j: (0, 0)),
            pl.BlockSpec((1, N_HEADS, HEAD_DIM, HEAD_DIM), lambda i, j: (i, 0, 0, 0)),
        ],
        out_specs=[
            pl.BlockSpec((1, rows, val_dim), lambda i, j: (i, j, 0)),
            pl.BlockSpec((1, N_HEADS, HEAD_DIM, HEAD_DIM), lambda i, j: (i, 0, 0, 0)),
        ],
        out_shape=[
            jax.ShapeDtypeStruct((b, t, val_dim), BF16),
            jax.ShapeDtypeStruct((b, N_HEADS, HEAD_DIM, HEAD_DIM), F32),
        ],
        scratch_shapes=[pltpu.VMEM((_SHORT_PAD + rows, qkv_dim), F32),
                        pltpu.VMEM((N_HEADS, HEAD_DIM, HEAD_DIM), F32)],
        compiler_params=pltpu.CompilerParams(
            dimension_semantics=("arbitrary", "arbitrary"), vmem_limit_bytes=VMEM_LIMIT),
        name="delta_rule",
    )(qkv, cache_p, p["w_short"], bg.reshape(b, t, LANES), bgt3, gsil, p["o_norm_g"], s0.astype(F32))


_R_E1, _R_E2, _R_RANK1, _R_RANK2, _R_W1, _R_W2 = range(6)


def _merge_kernel(alpha, convg_ref, og_ref, sgd_ref, x_ref, wo_ref, wout_ref, g_ref, b_ref,
                  wr_ref, br_ref, h_ref, hp_ref, route_ref, routet_ref, cnt_ref, carry_ref):
    @pl.when(pl.program_id(0) == 0)
    def _():
        carry_ref[...] = jnp.zeros_like(carry_ref)

    d_out = _dot(og_ref[...], wo_ref[...])
    merged = convg_ref[...].astype(F32) + d_out * sgd_ref[...].astype(F32)
    mix = _dot(merged.astype(BF16), wout_ref[...])
    h = _layer_norm(alpha * x_ref[...] + mix, g_ref[...], b_ref[...])
    h_ref[...] = h
    half_d = h.shape[1] // 2
    hp_ref[...] = _pack_pair(h[:, :half_d], h[:, half_d:])

    h_hi, h_mid, _ = _split3(h)
    w_hi, w_mid, _ = _split3(wr_ref[...])
    logits = _dot(h_hi, w_hi) + _dot(h_mid, w_hi) + _dot(h_hi, w_mid) + br_ref[...]
    tm = logits.shape[0]
    col = lax.broadcasted_iota(jnp.int32, logits.shape, 1).astype(F32)
    big = float(LANES)
    is_g = col < N_GROUPS
    mg = jnp.max(jnp.where(is_g, logits, -jnp.inf), axis=-1, keepdims=True)
    sg = jnp.sum(jnp.where(is_g, jnp.exp(jnp.where(is_g, logits, mg) - mg), 0.0), axis=-1, keepdims=True)
    pg_top = 1.0 / sg
    gidx = jnp.min(jnp.where(is_g & (logits == mg), col, big), axis=-1, keepdims=True)
    lo = ROUTE_COL0 + EXP_PER_GROUP * gidx
    sel = (col >= lo) & (col < lo + EXP_PER_GROUP)
    le = jnp.where(sel, logits, -jnp.inf)
    m1 = jnp.max(le, axis=-1, keepdims=True)
    i1 = jnp.min(jnp.where(le == m1, col, big), axis=-1, keepdims=True)
    le2 = jnp.where(col == i1, -jnp.inf, le)
    m2 = jnp.max(le2, axis=-1, keepdims=True)
    i2 = jnp.min(jnp.where(le2 == m2, col, big), axis=-1, keepdims=True)
    e2 = jnp.exp(m2 - m1)
    den = 1.0 + e2
    w1 = pg_top / den
    w2 = pg_top * e2 / den

    hit1 = col == i1
    hit2 = col == i2
    member = jnp.where(hit1 | hit2, 1.0, 0.0)
    ri = lax.broadcasted_iota(jnp.int32, (tm, tm), 0)
    ci = lax.broadcasted_iota(jnp.int32, (tm, tm), 1)
    earlier = jnp.where(ri > ci, 1.0, 0.0).astype(BF16)
    before = _dot(earlier, member.astype(BF16)) + carry_ref[...]
    rank1 = jnp.sum(jnp.where(hit1, before, 0.0), axis=-1, keepdims=True)
    rank2 = jnp.sum(jnp.where(hit2, before, 0.0), axis=-1, keepdims=True)
    carry_ref[...] += jnp.sum(member, axis=0, keepdims=True)
    cnt_ref[...] = jnp.broadcast_to(carry_ref[...], cnt_ref.shape)

    fields = (i1 - ROUTE_COL0, i2 - ROUTE_COL0, rank1, rank2, w1, w2)
    route = jnp.zeros_like(logits)
    for c, val in enumerate(fields):
        route = jnp.where(col == float(c), val, route)
    route_ref[...] = route
    sr = lax.broadcasted_iota(jnp.int32, (8, LANES), 0)
    sc = lax.broadcasted_iota(jnp.int32, (8, LANES), 1)
    pick = jnp.where(sr == sc, 1.0, 0.0).astype(BF16)
    routet_ref[...] = sum(_dot_nt(pick, part) for part in _split3(route))


def _merge(convg, og, sgd, x, p, alpha, tm):
    n, d = x.shape
    val_dim = og.shape[1]
    assert n % tm == 0
    row = lambda w: pl.BlockSpec((tm, w), lambda i: (i, 0))
    full = lambda shape: pl.BlockSpec(shape, lambda i: (0, 0))
    const = lambda shape: pl.BlockSpec(shape, lambda i: (0, 0), pipeline_mode=pl.Buffered(1))
    return pl.pallas_call(
        functools.partial(_merge_kernel, alpha),
        grid=(n // tm,),
        in_specs=[row(d), row(val_dim), row(d), row(d), const((val_dim, d)), const((d, d)),
                  full((1, d)), full((1, d)), const((d, LANES)), full((1, LANES))],
        out_specs=[row(d), row(d // 2), row(LANES), pl.BlockSpec((8, tm), lambda i: (0, i)), full((8, LANES))],
        out_shape=[jax.ShapeDtypeStruct((n, d), F32), jax.ShapeDtypeStruct((n, d // 2), jnp.uint32),
                   jax.ShapeDtypeStruct((n, LANES), F32),
                   jax.ShapeDtypeStruct((8, n), F32), jax.ShapeDtypeStruct((8, LANES), F32)],
        scratch_shapes=[pltpu.VMEM((1, LANES), F32)],
        compiler_params=pltpu.CompilerParams(
            dimension_semantics=("arbitrary",), vmem_limit_bytes=VMEM_LIMIT),
        name="merge_outproj",
    )(convg, og, sgd, x, p["w_o"], p["w_out"], p["ln1_g"], p["ln1_b"], p["w_router"], p["b_router"])


def _dest_kernel(starts_ref, rt_ref, dest_ref):
    rt = rt_ref[...]
    base = jnp.zeros_like(rt)
    for e in range(N_EXPERTS):
        base = jnp.where(rt == float(e), starts_ref[e].astype(F32), base)
    d = (base[0:2] + rt[2:4]).astype(jnp.int32)
    dest_ref[...] = jnp.concatenate([d, jnp.zeros((rt.shape[0] - 2, rt.shape[1]), jnp.int32)], axis=0)


def _dest_rows(route_t, starts, tn):
    rows, n = route_t.shape
    assert n % tn == 0
    grid_spec = pltpu.PrefetchScalarGridSpec(
        num_scalar_prefetch=1,
        grid=(n // tn,),
        in_specs=[pl.BlockSpec((rows, tn), lambda i, st: (0, i))],
        out_specs=pl.BlockSpec((rows, tn), lambda i, st: (0, i)),
    )
    out = pl.pallas_call(
        _dest_kernel,
        grid_spec=grid_spec,
        out_shape=jax.ShapeDtypeStruct((rows, n), jnp.int32),
        compiler_params=pltpu.CompilerParams(dimension_semantics=("arbitrary",)),
        name="moe_dest_rows",
    )(starts, route_t)
    return out[:2]


def _route_plan(route_t, cnt, te):
    n = route_t.shape[1]
    i32 = jnp.int32
    counts = cnt[0, ROUTE_COL0:ROUTE_COL0 + N_EXPERTS].astype(i32)
    ends = jnp.cumsum(counts)
    starts = ends - counts
    eids = jnp.arange(N_EXPERTS, dtype=i32)

    def lookup(table, idx):
        return jnp.sum(jnp.where(idx[None, :] == eids[:, None], table[:, None], 0), axis=0)

    dest = _dest_rows(route_t, starts, min(_DEST_TN, n))

    first_tile = starts // te
    last_tile = (ends - 1) // te
    items_e = jnp.where(counts > 0, last_tile - first_tile + 1, 0)
    item_end = jnp.cumsum(items_e)
    item_start = item_end - items_e
    total = item_end[-1]
    n_items = (2 * n) // te + N_EXPERTS - 1
    w = jnp.minimum(jnp.arange(n_items, dtype=i32), total - 1)
    item_e = jnp.sum((item_end[:, None] <= w[None, :]).astype(i32), axis=0)
    item_tile = lookup(first_tile, item_e) + w - lookup(item_start, item_e)
    lo = jnp.clip(lookup(starts, item_e) - item_tile * te, 0, te)
    hi = jnp.clip(lookup(ends, item_e) - item_tile * te, 0, te)
    return dest, (item_tile, item_e, lo, hi, total.reshape(1))


def _dispatch_kernel(d_ref, h_ref, xs_hbm, sem):
    tm = d_ref.shape[-1]

    def row_copy(r, dst):
        return pltpu.make_async_copy(h_ref.at[pl.ds(r, 1)], xs_hbm.at[pl.ds(dst, 1)], sem.at[0])

    def wait(r, carry):
        row_copy(r, 0).wait()
        row_copy(r, 0).wait()
        return carry

    for r in range(tm):
        row_copy(r, d_ref[0, 0, 0, r]).start(priority=0)
        row_copy(r, d_ref[1, 0, 0, r]).start(priority=1)
    lax.fori_loop(0, tm, wait, 0, unroll=8)


def _dispatch(h, dest, tm):
    n, d = h.shape
    assert n % tm == 0
    n_steps = n // tm
    return pl.pallas_call(
        _dispatch_kernel,
        grid=(n_steps,),
        in_specs=[pl.BlockSpec((2, 1, 1, tm), lambda i: (0, i, 0, 0), memory_space=pltpu.SMEM),
                  pl.BlockSpec((tm, d), lambda i: (i, 0))],
        out_specs=pl.BlockSpec(memory_space=pl.ANY),
        out_shape=jax.ShapeDtypeStruct((2 * n, d), h.dtype),
        scratch_shapes=[pltpu.SemaphoreType.DMA((1,))],
        compiler_params=pltpu.CompilerParams(
            dimension_semantics=("arbitrary",), vmem_limit_bytes=VMEM_LIMIT),
        name="moe_dispatch",
    )(dest.reshape(2, n_steps, 1, tm), h)


def _expert_kernel(tile_ref, exp_ref, lo_ref, hi_ref, tot_ref, xs_ref, wg_ref, wu_ref, wd_ref,
                   out_ref, wgb_ref, wub_ref, wdb_ref):
    w = pl.program_id(0)
    prev = jnp.maximum(w - 1, 0)
    live = w < tot_ref[0]
    new_expert = (w == 0) | (exp_ref[w] != exp_ref[prev])
    first_of_tile = (w == 0) | (tile_ref[w] != tile_ref[prev])

    @pl.when(live & new_expert)
    def _():
        wgb_ref[...] = wg_ref[0].astype(BF16)
        wub_ref[...] = wu_ref[0].astype(BF16)
        wdb_ref[...] = wd_ref[0].astype(BF16)

    @pl.when(live)
    def _():
        x_lo, x_hi = _unpack_pair(xs_ref[...])
        x = jnp.concatenate([x_lo.astype(BF16), x_hi.astype(BF16)], axis=1)
        hg = _dot(x, wgb_ref[...])
        hu = _dot(x, wub_ref[...])
        row = lax.broadcasted_iota(jnp.int32, (x.shape[0], 1), 0)
        mine = (row >= lo_ref[w]) & (row < hi_ref[w])
        act = jnp.where(mine, _silu(hg) * hu, 0.0).astype(BF16)
        part = _dot(act, wdb_ref[...])
        half_d = part.shape[1] // 2
        packed = _pack_pair(part[:, :half_d], part[:, half_d:])

        @pl.when(first_of_tile)
        def _():
            out_ref[...] = packed

        @pl.when(jnp.logical_not(first_of_tile))
        def _():
            out_ref[...] = jnp.where(mine, packed, out_ref[...])


def _experts(xs, items, p, te):
    rows, dp = xs.shape
    ne, d, f = p["w_gate"].shape
    assert d == 2 * dp
    item_tile, item_e, lo, hi, total = items
    n_items = item_tile.shape[0]
    tile_map = lambda w, t_, e_, lo_, hi_, n_: (t_[w], 0)
    exp_map = lambda w, t_, e_, lo_, hi_, n_: (e_[w], 0, 0)
    grid_spec = pltpu.PrefetchScalarGridSpec(
        num_scalar_prefetch=5,
        grid=(n_items,),
        in_specs=[
            pl.BlockSpec((te, dp), tile_map),
            pl.BlockSpec((1, d, f), exp_map),
            pl.BlockSpec((1, d, f), exp_map),
            pl.BlockSpec((1, f, d), exp_map),
        ],
        out_specs=pl.BlockSpec((te, dp), tile_map),
        scratch_shapes=[pltpu.VMEM((d, f), BF16), pltpu.VMEM((d, f), BF16), pltpu.VMEM((f, d), BF16)],
    )
    return pl.pallas_call(
        _expert_kernel,
        grid_spec=grid_spec,
        out_shape=jax.ShapeDtypeStruct((rows, dp), jnp.uint32),
        compiler_params=pltpu.CompilerParams(
            dimension_semantics=("arbitrary",), vmem_limit_bytes=VMEM_LIMIT),
        name="moe_experts",
    )(item_tile, item_e, lo, hi, total, xs, p["w_gate"], p["w_up"], p["w_down"])


def _combine_kernel(alpha, d_ref, dn_ref, h_ref, route_ref, rows_hbm, g_ref, b_ref, y_ref, o_ref, sem):
    i = pl.program_id(0)
    n_i = pl.num_programs(0)
    tm = h_ref.shape[0]
    slot = i % 2

    def row_copy(src_row, slot_, k, r):
        return pltpu.make_async_copy(rows_hbm.at[pl.ds(src_row, 1)], o_ref.at[slot_, k, pl.ds(r, 1)],
                                     sem.at[slot_])

    def start(dref, slot_):
        for r in range(tm):
            row_copy(dref[0, 0, 0, r], slot_, 0, r).start(priority=0)
            row_copy(dref[1, 0, 0, r], slot_, 1, r).start(priority=1)

    def wait(slot_):
        def body(r, carry):
            row_copy(0, slot_, 0, r).wait()
            row_copy(0, slot_, 1, r).wait()
            return carry
        lax.fori_loop(0, tm, body, 0, unroll=8)

    @pl.when(i == 0)
    def _():
        start(d_ref, 0)

    for nxt in (0, 1):
        @pl.when((i + 1 < n_i) & (slot == 1 - nxt))
        def _():
            start(dn_ref, nxt)

    wait(slot)
    route = route_ref[...]
    col = lax.broadcasted_iota(jnp.int32, route.shape, 1)
    w1 = jnp.sum(jnp.where(col == _R_W1, route, 0.0), axis=-1, keepdims=True)
    w2 = jnp.sum(jnp.where(col == _R_W2, route, 0.0), axis=-1, keepdims=True)
    a_lo, a_hi = _unpack_pair(o_ref[slot, 0])
    b_lo, b_hi = _unpack_pair(o_ref[slot, 1])
    moe = jnp.concatenate([w1 * a_lo + w2 * b_lo, w1 * a_hi + w2 * b_hi], axis=1)
    y_ref[...] = _layer_norm(alpha * h_ref[...] + moe, g_ref[...], b_ref[...])


def _combine(h, route, rows, dest, p, alpha, tm):
    n, d = h.shape
    assert n % tm == 0
    n_i = n // tm
    dest4 = dest.reshape(2, n_i, 1, tm)
    cur = pl.BlockSpec((2, 1, 1, tm), lambda i: (0, i, 0, 0), memory_space=pltpu.SMEM)
    nxt = pl.BlockSpec((2, 1, 1, tm), lambda i: (0, jnp.minimum(i + 1, n_i - 1), 0, 0),
                       memory_space=pltpu.SMEM)
    return pl.pallas_call(
        functools.partial(_combine_kernel, alpha),
        grid=(n_i,),
        in_specs=[cur, nxt,
                  pl.BlockSpec((tm, d), lambda i: (i, 0)),
                  pl.BlockSpec((tm, LANES), lambda i: (i, 0)),
                  pl.BlockSpec(memory_space=pl.ANY),
                  pl.BlockSpec((1, d), lambda i: (0, 0)),
                  pl.BlockSpec((1, d), lambda i: (0, 0))],
        out_specs=pl.BlockSpec((tm, d), lambda i: (i, 0)),
        out_shape=jax.ShapeDtypeStruct((n, d), F32),
        scratch_shapes=[pltpu.VMEM((2, 2, tm, d // 2), jnp.uint32), pltpu.SemaphoreType.DMA((2,))],
        compiler_params=pltpu.CompilerParams(
            dimension_semantics=("arbitrary",), vmem_limit_bytes=VMEM_LIMIT),
        name="moe_combine",
    )(dest4, dest4, h, route, rows, p["ln2_g"], p["ln2_b"])


def _pack_layer(w_in, b_in, w_dw, b_dw, lnc_g, lnc_b, w_conv_out, w_short, a_log, dt_bias, o_norm_g,
                w_o, w_out, ln1_g, ln1_b, w_rg, b_rg, w_re, b_re, w_gate, w_up, w_down, ln2_g, ln2_b):
    d = w_in.shape[0]
    c_conv = w_dw.shape[1]
    qkv_dim = w_short.shape[1]
    o_qkv = 2 * c_conv
    o_ba = o_qkv + qkv_dim
    o_tail = o_ba + 2 * N_HEADS

    def pad_cols(a, width):
        return jnp.pad(a, ((0, 0), (0, width - a.shape[1])))

    b2 = b_in[None, :].astype(F32)
    nh = N_HEADS
    zeros_h = jnp.zeros((nh,), F32)
    head_params = jnp.stack([jnp.concatenate([zeros_h, a_log.astype(F32)]),
                             jnp.concatenate([zeros_h, dt_bias.astype(F32)])])
    half = _TN // 2
    w_main = jnp.concatenate([w_in[:, :o_ba], w_in[:, o_tail:]], axis=1).astype(BF16)
    w_blk = w_main.reshape(d, -1, half).transpose(1, 0, 2)
    w_ba = w_in[:, o_ba:o_tail].astype(BF16)
    w_router = jnp.concatenate([w_rg, w_re], axis=1).astype(F32)
    b_router = jnp.concatenate([b_rg, b_re])[None, :].astype(F32)
    return dict(
        c_conv=c_conv, qkv_dim=qkv_dim,
        w_blk=w_blk, b_in=b2,
        w_ba=pad_cols(w_ba, LANES),
        b_ba=pad_cols(b2[:, o_ba:o_tail], LANES),
        w_bat=w_ba.T,
        b_bat=jnp.broadcast_to(b_in[o_ba:o_tail, None].astype(F32), (2 * nh, LANES)),
        b_tail=b2[:, o_tail:],
        p_row=pad_cols(head_params, LANES),
        p_col=pad_cols(head_params.T, LANES),
        w_dw=w_dw.astype(F32), b_dw=b_dw[None, :].astype(F32),
        lnc_g=lnc_g[None, :].astype(F32), lnc_b=lnc_b[None, :].astype(F32),
        w_conv_out=w_conv_out.astype(BF16),
        w_short=w_short.astype(F32),
        o_norm_g=o_norm_g[None, :].astype(F32),
        w_o=w_o.astype(BF16), w_out=w_out.astype(BF16),
        ln1_g=ln1_g[None, :].astype(F32), ln1_b=ln1_b[None, :].astype(F32),
        w_router=pad_cols(w_router, LANES), b_router=pad_cols(b_router, LANES),
        w_gate=w_gate.astype(F32), w_up=w_up.astype(F32), w_down=w_down.astype(F32),
        ln2_g=ln2_g[None, :].astype(F32), ln2_b=ln2_b[None, :].astype(F32),
    )


_MERGE_TM = 512
_COMBINE_TM = 256
_DEST_TN = 2048
_EXPERT_TE = 256
_DISPATCH_TM = 256


def _block(x, conv_buf, short_buf, s0, p, alpha, tm, conv_tt):
    b, t, d = x.shape
    n = b * t
    x2 = x.reshape(n, d).astype(F32)
    u, qkv, bg, bgt, gsil, sgc, sgd = _inproj(x2, p, tm)
    c_conv = u.shape[1]
    u3 = u.reshape(b, t, c_conv)
    qkv3 = qkv.reshape(b, t, -1)
    convg = _conv_branch(u3, conv_buf, sgc.reshape(b, t, d), p, conv_tt)
    og, s_new = _delta_branch(qkv3, short_buf, bg, bgt, gsil.reshape(b, t, -1), s0, p)
    h, hp, route, route_t, cnt = _merge(convg.reshape(n, d), og.reshape(n, -1), sgd, x2, p, alpha, min(_MERGE_TM, n))
    dest, items = _route_plan(route_t, cnt, _EXPERT_TE)
    xs = _dispatch(hp, dest, min(_DISPATCH_TM, n))
    rows = _experts(xs, items, p, _EXPERT_TE)
    y = _combine(h, route, rows, dest, p, alpha, min(_COMBINE_TM, n))
    kc = conv_buf.shape[1]
    ks = short_buf.shape[1]
    assert t >= kc and t >= ks
    return (y.reshape(b, t, d).astype(x.dtype), u3[:, t - kc:].astype(x.dtype),
            qkv3[:, t - ks:].astype(x.dtype), s_new.astype(s0.dtype))


def kernel(x_prompt, x_sample, cache_conv, cache_short, state_delta, w_in, b_in, w_dw, b_dw, lnc_g, lnc_b, w_conv_out, w_short, a_log, dt_bias, o_norm_g, w_o, w_out, ln1_g, ln1_b, w_rg, b_rg, w_re, b_re, w_gate, w_up, w_down, ln2_g, ln2_b):
    weights = (w_in, b_in, w_dw, b_dw, lnc_g, lnc_b, w_conv_out, w_short, a_log, dt_bias, o_norm_g,
               w_o, w_out, ln1_g, ln1_b, w_rg, b_rg, w_re, b_re, w_gate, w_up, w_down, ln2_g, ln2_b)
    depth = w_in.shape[0]
    alpha = (2.0 * depth) ** 0.25
    yp, ys = x_prompt, x_sample
    bp = x_prompt.shape[0]
    outs = [[] for _ in range(6)]
    for l in range(depth):
        p = _pack_layer(*(wt[l] for wt in weights))
        zc = jnp.zeros((bp,) + cache_conv.shape[2:], x_prompt.dtype)
        zs = jnp.zeros((bp,) + cache_short.shape[2:], x_prompt.dtype)
        zd = jnp.zeros((bp,) + state_delta.shape[2:], state_delta.dtype)
        yp, c, s, dl = _block(yp, zc, zs, zd, p, alpha, 1024, 256)
        outs[0].append(c), outs[1].append(s), outs[2].append(dl)
        ys, c, s, dl = _block(ys, cache_conv[l], cache_short[l], state_delta[l], p, alpha, 1024, 64)
        outs[3].append(c), outs[4].append(s), outs[5].append(dl)
    return (yp, ys) + tuple(jnp.stack(o) for o in outs)
```

```python
import functools

import jax
import jax.numpy as jnp
from jax import lax
from jax.experimental import pallas as pl
from jax.experimental.pallas import tpu as pltpu

F32 = jnp.float32
BF16 = jnp.bfloat16

CHUNK = 64
N_HEADS = 8
HEAD_DIM = 128
N_GROUPS = 4
EXP_PER_GROUP = 8
N_EXPERTS = N_GROUPS * EXP_PER_GROUP
LN_EPS = 1e-5
NORM_EPS = 1e-6
LANES = 128
ROUTE_COL0 = N_GROUPS
VMEM_LIMIT = 56 * 1024 * 1024


def _dot(a, b):
    return jnp.dot(a, b, preferred_element_type=F32)


def _dot_nt(a, b):
    return lax.dot_general(a, b, (((1,), (1,)), ((), ())), preferred_element_type=F32)


def _dot_tn(a, b):
    return lax.dot_general(a, b, (((0,), (0,)), ((), ())), preferred_element_type=F32)


def _split3(x):
    hi = x.astype(BF16)
    r1 = x - hi.astype(F32)
    mid = r1.astype(BF16)
    lo = (r1 - mid.astype(F32)).astype(BF16)
    return hi, mid, lo


def _pack_pair(lo, hi):
    lo_bits = lax.bitcast_convert_type(lo.astype(BF16).astype(F32), jnp.uint32)
    hi_bits = lax.bitcast_convert_type(hi.astype(BF16).astype(F32), jnp.uint32)
    return (hi_bits & jnp.uint32(0xFFFF0000)) | (lo_bits >> 16)


def _unpack_pair(packed):
    lo = lax.bitcast_convert_type(packed << 16, F32)
    hi = lax.bitcast_convert_type(packed & jnp.uint32(0xFFFF0000), F32)
    return lo, hi


def _sigmoid(x):
    return 1.0 / (1.0 + jnp.exp(-x))


def _silu(x):
    return x * _sigmoid(x)


def _softplus(x):
    return jnp.maximum(x, 0.0) + jnp.log1p(jnp.exp(-jnp.abs(x)))


def _layer_norm(x, g, b):
    mu = jnp.mean(x, axis=-1, keepdims=True)
    xc = x - mu
    var = jnp.mean(xc * xc, axis=-1, keepdims=True)
    return xc * lax.rsqrt(var + LN_EPS) * g + b


def _clamp(v, lo, hi):
    return jnp.minimum(jnp.maximum(v, lo), hi)


_TN = 512
_J_GLU, _J_QKV, _J_BA, _J_GO, _J_GC, _J_GD, _J_END = 0, 4, 10, 11, 13, 17, 21


def _wide_dot(x, w_ref):
    return jnp.concatenate([_dot(x, w_ref[b]) for b in range(w_ref.shape[0])], axis=1)


def _inproj_kernel(x_ref, wga_ref, wgb_ref, wqkv_ref, wba_ref, wbat_ref, wtail_ref,
                   bga_ref, bgb_ref, bqkv_ref, bba_ref, bbat_ref, btail_ref, prow_ref, pcol_ref,
                   u_ref, qkv_ref, bg_ref, bgt_ref, gsil_ref, sgc_ref, sgd_ref, xb_ref):
    j = pl.program_id(1)

    @pl.when(j == 0)
    def _():
        xb_ref[...] = x_ref[...].astype(BF16)

    @pl.when(j < _J_QKV)
    def _():
        xb = xb_ref[...]
        value = _dot(xb, wga_ref[0]) + bga_ref[...]
        gate = _dot(xb, wgb_ref[0]) + bgb_ref[...]
        u_ref[...] = value * _sigmoid(gate)

    @pl.when((j >= _J_QKV) & (j < _J_BA))
    def _():
        qkv_ref[...] = _wide_dot(xb_ref[...], wqkv_ref) + bqkv_ref[...]

    @pl.when(j == _J_BA)
    def _():
        xb = xb_ref[...]
        z = _dot(xb, wba_ref[...]) + bba_ref[...]
        col = lax.broadcasted_iota(jnp.int32, z.shape, 1)
        g = -jnp.exp(prow_ref[0:1, :]) * _softplus(z + prow_ref[1:2, :])
        bg_ref[...] = jnp.where(col < N_HEADS, _sigmoid(z), g)
        zt = _dot_nt(wbat_ref[...], xb) + bbat_ref[:, 0:1]
        row = lax.broadcasted_iota(jnp.int32, zt.shape, 0)
        gt = -jnp.exp(pcol_ref[:, 0:1]) * _softplus(zt + pcol_ref[:, 1:2])
        bgt_ref[...] = jnp.where(row < N_HEADS, _sigmoid(zt), gt)

    @pl.when((j >= _J_GO) & (j < _J_GC))
    def _():
        z = _wide_dot(xb_ref[...], wtail_ref) + btail_ref[...]
        gsil_ref[...] = _silu(z).astype(BF16)

    @pl.when((j >= _J_GC) & (j < _J_GD))
    def _():
        z = _wide_dot(xb_ref[...], wtail_ref) + btail_ref[...]
        sgc_ref[...] = _sigmoid(z).astype(BF16)

    @pl.when(j >= _J_GD)
    def _():
        z = _wide_dot(xb_ref[...], wtail_ref) + btail_ref[...]
        sgd_ref[...] = _sigmoid(z).astype(BF16)


def _inproj(x, pk, tm):
    n, d = x.shape
    c_conv, qkv_dim = pk["c_conv"], pk["qkv_dim"]
    val_dim = N_HEADS * HEAD_DIM
    half = _TN // 2
    n_glu = _J_QKV - _J_GLU
    assert n % tm == 0 and c_conv == n_glu * half and (2 * c_conv) % _TN == 0
    qkv_blk0 = 2 * c_conv // _TN
    tail_blk0 = (2 * c_conv + qkv_dim) // _TN
    assert (2 * c_conv + qkv_dim) % _TN == 0

    def cm(lo, hi):
        return lambda i, j: (0, _clamp(j - lo, 0, hi - lo - 1))

    def om(lo, hi):
        return lambda i, j: (i, _clamp(j - lo, 0, hi - lo - 1))

    in_specs = [
        pl.BlockSpec((tm, d), lambda i, j: (i, 0)),
        pl.BlockSpec((1, d, half), lambda i, j: (_clamp(j, 0, n_glu - 1), 0, 0)),
        pl.BlockSpec((1, d, half), lambda i, j: (n_glu + _clamp(j, 0, n_glu - 1), 0, 0)),
        pl.BlockSpec((2, d, half), lambda i, j: (qkv_blk0 + _clamp(j - _J_QKV, 0, _J_BA - _J_QKV - 1), 0, 0)),
        pl.BlockSpec((d, LANES), lambda i, j: (0, 0)),
        pl.BlockSpec((2 * N_HEADS, d), lambda i, j: (0, 0)),
        pl.BlockSpec((2, d, half), lambda i, j: (tail_blk0 + _clamp(j - _J_GO, 0, _J_END - _J_GO - 1), 0, 0)),
        pl.BlockSpec((1, half), lambda i, j: (0, _clamp(j, 0, n_glu - 1))),
        pl.BlockSpec((1, half), lambda i, j: (0, n_glu + _clamp(j, 0, n_glu - 1))),
        pl.BlockSpec((1, _TN), lambda i, j: (0, qkv_blk0 + _clamp(j - _J_QKV, 0, _J_BA - _J_QKV - 1))),
        pl.BlockSpec((1, LANES), lambda i, j: (0, 0)),
        pl.BlockSpec((2 * N_HEADS, LANES), lambda i, j: (0, 0)),
        pl.BlockSpec((1, _TN), cm(_J_GO, _J_END)),
        pl.BlockSpec((2, LANES), lambda i, j: (0, 0)),
        pl.BlockSpec((2 * N_HEADS, LANES), lambda i, j: (0, 0)),
    ]
    out_shape = [
        jax.ShapeDtypeStruct((n, c_conv), F32),
        jax.ShapeDtypeStruct((n, qkv_dim), F32),
        jax.ShapeDtypeStruct((n, LANES), F32),
        jax.ShapeDtypeStruct((2 * N_HEADS, n), F32),
        jax.ShapeDtypeStruct((n, val_dim), BF16),
        jax.ShapeDtypeStruct((n, d), BF16),
        jax.ShapeDtypeStruct((n, d), BF16),
    ]
    out_specs = [
        pl.BlockSpec((tm, _TN // 2), om(_J_GLU, _J_QKV)),
        pl.BlockSpec((tm, _TN), om(_J_QKV, _J_BA)),
        pl.BlockSpec((tm, LANES), lambda i, j: (i, 0)),
        pl.BlockSpec((2 * N_HEADS, tm), lambda i, j: (0, i)),
        pl.BlockSpec((tm, _TN), om(_J_GO, _J_GC)),
        pl.BlockSpec((tm, _TN), om(_J_GC, _J_GD)),
        pl.BlockSpec((tm, _TN), om(_J_GD, _J_END)),
    ]
    return pl.pallas_call(
        _inproj_kernel,
        grid=(n // tm, _J_END),
        in_specs=in_specs,
        out_specs=out_specs,
        out_shape=out_shape,
        scratch_shapes=[pltpu.VMEM((tm, d), BF16)],
        compiler_params=pltpu.CompilerParams(
            dimension_semantics=("arbitrary", "arbitrary"), vmem_limit_bytes=VMEM_LIMIT),
        name="inproj",
    )(x, pk["w_blk"], pk["w_blk"], pk["w_blk"], pk["w_ba"], pk["w_bat"], pk["w_blk"],
      pk["b_in"], pk["b_in"], pk["b_in"], pk["b_ba"], pk["b_bat"], pk["b_tail"], pk["p_row"], pk["p_col"])


_HALO = 32
_CONV_RB = 64
_CONV_FB = 8
_CONV_SEQS = 4


def _conv_kernel(independent, u_ref, cache_ref, wdw_ref, bdw_ref, lng_ref, lnb_ref, wco_ref, sgc_ref,
                 out_ref, xt_ref, yt_ref, cn_ref):
    t = pl.program_id(1)
    nseq, tt, c_conv = u_ref.shape
    width = wdw_ref.shape[0]
    nfold = c_conv // LANES
    first = _HALO - (width - 1)

    def load_history(q):
        for s in range(nfold):
            xt_ref[pl.ds(s, _HALO, stride=nfold), :] = cache_ref[q, :, s * LANES:(s + 1) * LANES]

    def frames(ib, carry):
        f0 = ib * _CONV_FB
        acc = [None] * _CONV_FB
        for k in range(width):
            wk = wdw_ref[k]
            for j in range(_CONV_FB):
                row = pl.multiple_of((first + f0 + j + k) * nfold, nfold)
                term = wk * xt_ref[pl.ds(row, nfold), :]
                acc[j] = term if acc[j] is None else acc[j] + term
        for j in range(_CONV_FB):
            yt_ref[pl.ds(pl.multiple_of((f0 + j) * nfold, nfold), nfold), :] = acc[j]
        return carry

    if not independent:
        pl.when(t == 0)(lambda: load_history(0))

    for q in range(nseq):
        if independent:
            load_history(q)
        for s in range(nfold):
            xt_ref[pl.ds(_HALO * nfold + s, tt, stride=nfold), :] = u_ref[q, :, s * LANES:(s + 1) * LANES]
        lax.fori_loop(0, tt // _CONV_FB, frames, 0)
        for rb in range(tt // _CONV_RB):
            r0 = rb * _CONV_RB
            y = jnp.concatenate(
                [yt_ref[pl.ds(r0 * nfold + s, _CONV_RB, stride=nfold), :] for s in range(nfold)], axis=1)
            y = _layer_norm(y + bdw_ref[...], lng_ref[...], lnb_ref[...])
            cn_ref[q * tt + r0:q * tt + r0 + _CONV_RB, :] = _silu(y).astype(BF16)

    co = _dot(cn_ref[...], wco_ref[...])
    for q in range(nseq):
        out_ref[q] = (co[q * tt:(q + 1) * tt] * sgc_ref[q].astype(F32)).astype(BF16)
    if not independent:
        xt_ref[0:_HALO * nfold, :] = xt_ref[tt * nfold:(tt + _HALO) * nfold, :]


def _conv_branch(u, cache, sgc, p, tt):
    b, t, c_conv = u.shape
    d = sgc.shape[-1]
    width = p["w_dw"].shape[0]
    assert t % tt == 0 and tt % _CONV_RB == 0 and tt >= _HALO and width - 1 <= _HALO
    assert c_conv % (8 * LANES) == 0
    nfold = c_conv // LANES
    cache_p = jnp.pad(cache.astype(F32), ((0, 0), (_HALO - (width - 1), 0), (0, 0)))
    independent = t == tt
    nseq = _CONV_SEQS if independent and b % _CONV_SEQS == 0 else 1
    full2 = lambda shape: pl.BlockSpec(shape, lambda i, j: (0, 0))
    return pl.pallas_call(
        functools.partial(_conv_kernel, independent),
        grid=(b // nseq, t // tt),
        in_specs=[
            pl.BlockSpec((nseq, tt, c_conv), lambda i, j: (i, j, 0)),
            pl.BlockSpec((nseq, _HALO, c_conv), lambda i, j: (i, 0, 0)),
            pl.BlockSpec((width, nfold, LANES), lambda i, j: (0, 0, 0)),
            full2((1, c_conv)), full2((1, c_conv)), full2((1, c_conv)),
            full2((c_conv, d)),
            pl.BlockSpec((nseq, tt, d), lambda i, j: (i, j, 0)),
        ],
        out_specs=pl.BlockSpec((nseq, tt, d), lambda i, j: (i, j, 0)),
        out_shape=jax.ShapeDtypeStruct((b, t, d), BF16),
        scratch_shapes=[pltpu.VMEM(((tt + _HALO) * nfold, LANES), F32),
                        pltpu.VMEM((tt * nfold, LANES), F32),
                        pltpu.VMEM((nseq * tt, c_conv), BF16)],
        compiler_params=pltpu.CompilerParams(
            dimension_semantics=("arbitrary", "arbitrary"), vmem_limit_bytes=VMEM_LIMIT),
        name="conv_branch",
    )(u, cache_p, p["w_dw"].reshape(width, nfold, LANES), p["b_dw"], p["lnc_g"], p["lnc_b"],
      p["w_conv_out"], sgc)


_SHORT_PAD = 8
_DELTA_CPS = 4


def _delta_kernel(n_steps, cps, independent, qkv_ref, cache_ref, wsh_ref, bg_ref, bgt_ref, gsil_ref, ong_ref,
                  s0_ref, o_ref, sfin_ref, xb_ref, s_ref):
    c = pl.program_id(1)
    ck = CHUNK
    key_dim = N_HEADS * HEAD_DIM
    sw = wsh_ref.shape[0]
    unit_rows = _SHORT_PAD + ck
    ub = lambda cc: cc * unit_rows

    if independent:
        for cc in range(cps):
            xb_ref[ub(cc):ub(cc) + _SHORT_PAD, :] = cache_ref[cc]
            xb_ref[ub(cc) + _SHORT_PAD:ub(cc) + unit_rows, :] = qkv_ref[cc]
    else:
        @pl.when(c == 0)
        def _():
            s_ref[...] = s0_ref[0]
            xb_ref[0:_SHORT_PAD, :] = cache_ref[0]

        for cc in range(cps):
            xb_ref[ub(cc) + _SHORT_PAD:ub(cc) + unit_rows, :] = qkv_ref[0, cc * ck:(cc + 1) * ck, :]
            if cc > 0:
                xb_ref[ub(cc):ub(cc) + _SHORT_PAD, :] = qkv_ref[0, cc * ck - _SHORT_PAD:cc * ck, :]

    def conv_cols(cc, lo):
        lanes = slice(lo, lo + HEAD_DIM)
        base = ub(cc) + _SHORT_PAD
        acc = wsh_ref[sw - 1:sw, lanes] * xb_ref[base:base + ck, lanes]
        for k in range(sw - 1):
            r = base - (sw - 1) + k
            acc = acc + wsh_ref[k:k + 1, lanes] * xb_ref[r:r + ck, lanes]
        return _silu(acc)

    rowp = lax.broadcasted_iota(jnp.int32, (ck, 2 * ck), 0)
    lanep = lax.broadcasted_iota(jnp.int32, (ck, 2 * ck), 1)
    odd = lanep >= ck
    lcol = jnp.where(odd, lanep - ck, lanep)
    incl = rowp >= lcol
    strict = rowp > lcol
    eye = jnp.where(rowp == lcol, 1.0, 0.0).astype(F32)
    bd8 = (rowp // 8) == (lcol // 8)
    lvl = [((rowp // (2 * s)) == (lcol // (2 * s))) & ((rowp // s) != (lcol // s)) for s in (8, 16, 32)]
    ri = lax.broadcasted_iota(jnp.int32, (ck, ck), 0)
    ci = lax.broadcasted_iota(jnp.int32, (ck, ck), 1)
    tri_l = jnp.where(ri >= ci, 1.0, 0.0).astype(BF16)
    r2 = lax.broadcasted_iota(jnp.int32, (2 * ck, 2 * ck), 0)
    c2 = lax.broadcasted_iota(jnp.int32, (2 * ck, 2 * ck), 1)
    tri_u2 = jnp.where(((r2 >= ck) == (c2 >= ck)) & (r2 <= c2), 1.0, 0.0).astype(BF16)

    bf = lambda m: m.astype(BF16)

    def block_diag(pm):
        return bf(jnp.concatenate([jnp.where(odd, 0.0, pm), jnp.where(odd, pm, 0.0)], axis=0))

    def block_rows(top, bottom):
        z = jnp.zeros_like(top)
        return jnp.concatenate([jnp.concatenate([top, z], axis=1), jnp.concatenate([z, bottom], axis=1)], axis=0)

    bg = [bg_ref[cc] if independent else bg_ref[0, cc * ck:(cc + 1) * ck, :] for cc in range(cps)]
    gc_cols = [sum(_dot(tri_l, part) for part in _split3(bg[cc])) for cc in range(cps)]
    gc_rows = [sum(_dot(part, tri_u2) for part in _split3(bgt_ref[cc])) for cc in range(cps)]

    nh2 = N_HEADS // 2
    heads = range(cps * N_HEADS)
    pairs = range(cps * nh2)
    hcc = lambda i: (i // N_HEADS, i % N_HEADS)
    ev = lambda j: (j // nh2) * N_HEADS + 2 * (j % nh2)
    q = [conv_cols(hcc(i)[0], hcc(i)[1] * HEAD_DIM) for i in heads]
    k = [conv_cols(hcc(i)[0], key_dim + hcc(i)[1] * HEAD_DIM) for i in heads]
    v = [conv_cols(hcc(i)[0], 2 * key_dim + hcc(i)[1] * HEAD_DIM) for i in heads]
    q = [x * lax.rsqrt(jnp.sum(x * x, axis=-1, keepdims=True) + NORM_EPS) * (HEAD_DIM ** -0.5) for x in q]
    k = [x * lax.rsqrt(jnp.sum(x * x, axis=-1, keepdims=True) + NORM_EPS) for x in k]
    beta = [bg[hcc(i)[0]][:, hcc(i)[1]:hcc(i)[1] + 1] for i in heads]
    gcc = [gc_cols[hcc(i)[0]][:, N_HEADS + hcc(i)[1]:N_HEADS + hcc(i)[1] + 1] for i in heads]
    kb = [k[i] * beta[i] for i in heads]
    kbf = [bf(x) for x in k]
    kk = [block_rows(kbf[ev(j)], kbf[ev(j) + 1]) for j in pairs]
    gcc2 = [jnp.where(odd, gcc[ev(j) + 1], gcc[ev(j)]) for j in pairs]
    gcr2 = [gc_rows[j // nh2][nh2 + j % nh2:nh2 + j % nh2 + 1, :] for j in pairs]
    decay = [jnp.exp(jnp.where(incl, gcc2[j] - gcr2[j], -jnp.inf)) for j in pairs]
    a = [jnp.where(strict,
                   _dot_nt(jnp.concatenate([bf(kb[ev(j)]), bf(kb[ev(j) + 1])], axis=1), kk[j]) * decay[j], 0.0)
         for j in pairs]
    qk = [_dot_nt(jnp.concatenate([bf(q[ev(j)]), bf(q[ev(j) + 1])], axis=1), kk[j]) * decay[j] for j in pairs]

    ad = [jnp.where(bd8, x, 0.0) for x in a]
    adb = [bf(x) for x in ad]
    a2 = [_dot(adb[j], block_diag(ad[j])) for j in pairs]
    a2d = [block_diag(x) for x in a2]
    a3 = [_dot(adb[j], a2d[j]) for j in pairs]
    a4 = [_dot(bf(a2[j]), a2d[j]) for j in pairs]
    t = [eye - ad[j] + a2[j] - a3[j] for j in pairs]
    t = [t[j] + _dot(bf(t[j]), block_diag(a4[j])) for j in pairs]
    for m in lvl:
        x = [_dot(bf(jnp.where(m, a[j], 0.0)), block_diag(t[j])) for j in pairs]
        t = [t[j] - _dot(bf(t[j]), block_diag(x[j])) for j in pairs]

    egc = [jnp.exp(x) for x in gcc]
    rhs = [bf(jnp.concatenate([v[i] * beta[i], kb[i] * egc[i]], axis=1)) for i in heads]
    sol2 = [_dot(bf(t[j]), block_rows(rhs[ev(j)], rhs[ev(j) + 1])) for j in pairs]
    sol = [sol2[(i // N_HEADS) * nh2 + (i % N_HEADS) // 2][:, (i % 2) * 2 * HEAD_DIM:(i % 2 + 1) * 2 * HEAD_DIM]
           for i in heads]
    g_last = [x[ck - 1:ck, :] for x in gcc]
    k_dec = [bf(k[i] * jnp.exp(g_last[i] - gcc[i])) for i in heads]
    wq_lhs = [bf(jnp.concatenate([sol[i][:, HEAD_DIM:], q[i] * egc[i]], axis=0)) for i in heads]
    qkb = [bf(x) for x in qk]

    def o_rows(ref, cc, lanes):
        return ref.at[cc, :, lanes] if independent else ref.at[0, cc * ck:(cc + 1) * ck, lanes]

    groups = [list(range(cps))] if independent else [[cc] for cc in range(cps)]
    carried = None if independent else [s_ref[h] for h in range(N_HEADS)]
    for group in groups:
        units = [(cc, h) for cc in group for h in range(N_HEADS)]
        s_in = [s0_ref[cc, h] if independent else carried[h] for cc, h in units]
        sb = [bf(x) for x in s_in]
        wq = [_dot(wq_lhs[cc * N_HEADS + h], sb[i]) for i, (cc, h) in enumerate(units)]
        vb = [bf(sol[cc * N_HEADS + h][:, :HEAD_DIM] - wq[i][:ck]) for i, (cc, h) in enumerate(units)]
        o2 = [_dot(qkb[cc * nh2 + p], block_rows(vb[g * N_HEADS + 2 * p], vb[g * N_HEADS + 2 * p + 1]))
              for g, cc in enumerate(group) for p in range(nh2)]
        s_out = [s_in[i] * jnp.exp(g_last[cc * N_HEADS + h]) + _dot_tn(k_dec[cc * N_HEADS + h], vb[i])
                 for i, (cc, h) in enumerate(units)]
        for i, (cc, h) in enumerate(units):
            g = i // N_HEADS
            o = wq[i][ck:] + o2[g * nh2 + h // 2][:, (h % 2) * HEAD_DIM:(h % 2 + 1) * HEAD_DIM]
            on = o * lax.rsqrt(jnp.mean(o * o, axis=-1, keepdims=True) + NORM_EPS) * ong_ref[...]
            lanes = slice(h * HEAD_DIM, (h + 1) * HEAD_DIM)
            o_rows(o_ref, cc, lanes)[...] = (on * o_rows(gsil_ref, cc, lanes)[...].astype(F32)).astype(BF16)
            if independent:
                sfin_ref[cc, h] = s_out[i]
        if not independent:
            carried = s_out

    if not independent:
        for h in range(N_HEADS):
            s_ref[h] = carried[h]
        xb_ref[0:_SHORT_PAD, :] = qkv_ref[0, cps * ck - _SHORT_PAD:cps * ck, :]

        @pl.when(c == n_steps - 1)
        def _():
            sfin_ref[0] = s_ref[...]


def _delta_branch(qkv, cache, bg, bgt, gsil, s0, p):
    b, t, qkv_dim = qkv.shape
    assert t % CHUNK == 0
    nc = t // CHUNK
    sw = p["w_short"].shape[0]
    cache_p = jnp.pad(cache.astype(F32), ((0, 0), (_SHORT_PAD - (sw - 1), 0), (0, 0)))
    val_dim = N_HEADS * HEAD_DIM
    bgt3 = bgt.reshape(2 * N_HEADS, b * nc, CHUNK).transpose(1, 0, 2).reshape(b * nc, N_HEADS, 2 * CHUNK)
    independent = nc == 1
    if independent:
        cps = _DELTA_CPS if b % _DELTA_CPS == 0 else 1
        grid = (b // cps, 1)
        seq = lambda i, j: (i, 0, 0)
        blk = lambda width: pl.BlockSpec((cps, CHUNK, width), seq)
        cache_spec = pl.BlockSpec((cps, _SHORT_PAD, qkv_dim), seq)
        bgt_spec = pl.BlockSpec((cps, N_HEADS, 2 * CHUNK), seq)
        state_spec = pl.BlockSpec((cps, N_HEADS, HEAD_DIM, HEAD_DIM), lambda i, j: (i, 0, 0, 0))
        n_steps = 1
    else:
        cps = _DELTA_CPS if nc % _DELTA_CPS == 0 else 1
        n_steps = nc // cps
        grid = (b, n_steps)
        blk = lambda width: pl.BlockSpec((1, cps * CHUNK, width), lambda i, j: (i, j, 0))
        cache_spec = pl.BlockSpec((1, _SHORT_PAD, qkv_dim), lambda i, j: (i, 0, 0))
        bgt_spec = pl.BlockSpec((cps, N_HEADS, 2 * CHUNK), lambda i, j: (i * n_steps + j, 0, 0))
        state_spec = pl.BlockSpec((1, N_HEADS, HEAD_DIM, HEAD_DIM), lambda i, j: (i, 0, 0, 0))
    return pl.pallas_call(
        functools.partial(_delta_kernel, n_steps, cps, independent),
        grid=grid,
        in_specs=[
            blk(qkv_dim),
            cache_spec,
            pl.BlockSpec((sw, qkv_dim), lambda i, j: (0, 0)),
            blk(LANES),
            bgt_spec,
            blk(val_dim),
            pl.BlockSpec((1, HEAD_DIM), lambda i, j: (0, 0)),
            state_spec,
        ],
        out_specs=[blk(val_dim), state_spec],
        out_shape=[
            jax.ShapeDtypeStruct((b, t, val_dim), BF16),
            jax.ShapeDtypeStruct((b, N_HEADS, HEAD_DIM, HEAD_DIM), F32),
        ],
        scratch_shapes=[pltpu.VMEM((cps * (_SHORT_PAD + CHUNK), qkv_dim), F32),
                        pltpu.VMEM((N_HEADS, HEAD_DIM, HEAD_DIM), F32)],
        compiler_params=pltpu.CompilerParams(
            dimension_semantics=("arbitrary", "arbitrary"), vmem_limit_bytes=VMEM_LIMIT),
        name="delta_rule",
    )(qkv, cache_p, p["w_short"], bg.reshape(b, t, LANES), bgt3, gsil, p["o_norm_g"], s0.astype(F32))


_R_E1, _R_E2, _R_RANK1, _R_RANK2, _R_W1, _R_W2 = range(6)


def _merge_kernel(alpha, convg_ref, og_ref, sgd_ref, x_ref, wo_ref, wout_ref, g_ref, b_ref,
                  wr_ref, br_ref, h_ref, hp_ref, route_ref, routet_ref, cnt_ref, carry_ref):
    @pl.when(pl.program_id(0) == 0)
    def _():
        carry_ref[...] = jnp.zeros_like(carry_ref)

    d_out = _dot(og_ref[...], wo_ref[...])
    merged = convg_ref[...].astype(F32) + d_out * sgd_ref[...].astype(F32)
    mix = _dot(merged.astype(BF16), wout_ref[...])
    h = _layer_norm(alpha * x_ref[...] + mix, g_ref[...], b_ref[...])
    h_ref[...] = h
    half_d = h.shape[1] // 2
    hp_ref[...] = _pack_pair(h[:, :half_d], h[:, half_d:])

    h_hi, h_mid, _ = _split3(h)
    w_hi, w_mid, _ = _split3(wr_ref[...])
    logits = _dot(h_hi, w_hi) + _dot(h_mid, w_hi) + _dot(h_hi, w_mid) + br_ref[...]
    tm = logits.shape[0]
    col = lax.broadcasted_iota(jnp.int32, logits.shape, 1).astype(F32)
    big = float(LANES)
    is_g = col < N_GROUPS
    mg = jnp.max(jnp.where(is_g, logits, -jnp.inf), axis=-1, keepdims=True)
    sg = jnp.sum(jnp.where(is_g, jnp.exp(jnp.where(is_g, logits, mg) - mg), 0.0), axis=-1, keepdims=True)
    pg_top = 1.0 / sg
    gidx = jnp.min(jnp.where(is_g & (logits == mg), col, big), axis=-1, keepdims=True)
    lo = ROUTE_COL0 + EXP_PER_GROUP * gidx
    sel = (col >= lo) & (col < lo + EXP_PER_GROUP)
    le = jnp.where(sel, logits, -jnp.inf)
    m1 = jnp.max(le, axis=-1, keepdims=True)
    i1 = jnp.min(jnp.where(le == m1, col, big), axis=-1, keepdims=True)
    le2 = jnp.where(col == i1, -jnp.inf, le)
    m2 = jnp.max(le2, axis=-1, keepdims=True)
    i2 = jnp.min(jnp.where(le2 == m2, col, big), axis=-1, keepdims=True)
    e2 = jnp.exp(m2 - m1)
    den = 1.0 + e2
    w1 = pg_top / den
    w2 = pg_top * e2 / den

    hit1 = col == i1
    hit2 = col == i2
    member = jnp.where(hit1 | hit2, 1.0, 0.0)
    ri = lax.broadcasted_iota(jnp.int32, (tm, tm), 0)
    ci = lax.broadcasted_iota(jnp.int32, (tm, tm), 1)
    earlier = jnp.where(ri > ci, 1.0, 0.0).astype(BF16)
    before = _dot(earlier, member.astype(BF16)) + carry_ref[...]
    rank1 = jnp.sum(jnp.where(hit1, before, 0.0), axis=-1, keepdims=True)
    rank2 = jnp.sum(jnp.where(hit2, before, 0.0), axis=-1, keepdims=True)
    carry_ref[...] += jnp.sum(member, axis=0, keepdims=True)
    cnt_ref[...] = jnp.broadcast_to(carry_ref[...], cnt_ref.shape)

    fields = (i1 - ROUTE_COL0, i2 - ROUTE_COL0, rank1, rank2, w1, w2)
    route = jnp.zeros_like(logits)
    for c, val in enumerate(fields):
        route = jnp.where(col == float(c), val, route)
    route_ref[...] = route
    sr = lax.broadcasted_iota(jnp.int32, (8, LANES), 0)
    sc = lax.broadcasted_iota(jnp.int32, (8, LANES), 1)
    pick = jnp.where(sr == sc, 1.0, 0.0).astype(BF16)
    routet_ref[...] = sum(_dot_nt(pick, part) for part in _split3(route))


def _merge(convg, og, sgd, x, p, alpha, tm):
    n, d = x.shape
    val_dim = og.shape[1]
    assert n % tm == 0
    row = lambda w: pl.BlockSpec((tm, w), lambda i: (i, 0))
    full = lambda shape: pl.BlockSpec(shape, lambda i: (0, 0))
    const = lambda shape: pl.BlockSpec(shape, lambda i: (0, 0), pipeline_mode=pl.Buffered(1))
    return pl.pallas_call(
        functools.partial(_merge_kernel, alpha),
        grid=(n // tm,),
        in_specs=[row(d), row(val_dim), row(d), row(d), const((val_dim, d)), const((d, d)),
                  full((1, d)), full((1, d)), const((d, LANES)), full((1, LANES))],
        out_specs=[row(d), row(d // 2), row(LANES), pl.BlockSpec((8, tm), lambda i: (0, i)), full((8, LANES))],
        out_shape=[jax.ShapeDtypeStruct((n, d), F32), jax.ShapeDtypeStruct((n, d // 2), jnp.uint32),
                   jax.ShapeDtypeStruct((n, LANES), F32),
                   jax.ShapeDtypeStruct((8, n), F32), jax.ShapeDtypeStruct((8, LANES), F32)],
        scratch_shapes=[pltpu.VMEM((1, LANES), F32)],
        compiler_params=pltpu.CompilerParams(
            dimension_semantics=("arbitrary",), vmem_limit_bytes=VMEM_LIMIT),
        name="merge_outproj",
    )(convg, og, sgd, x, p["w_o"], p["w_out"], p["ln1_g"], p["ln1_b"], p["w_router"], p["b_router"])


def _dest_kernel(starts_ref, rt_ref, dest_ref):
    rt = rt_ref[...]
    base = jnp.zeros_like(rt)
    for e in range(N_EXPERTS):
        base = jnp.where(rt == float(e), starts_ref[e].astype(F32), base)
    d = (base[0:2] + rt[2:4]).astype(jnp.int32)
    dest_ref[...] = jnp.concatenate([d, jnp.zeros((rt.shape[0] - 2, rt.shape[1]), jnp.int32)], axis=0)


def _dest_rows(route_t, starts, tn):
    rows, n = route_t.shape
    assert n % tn == 0
    grid_spec = pltpu.PrefetchScalarGridSpec(
        num_scalar_prefetch=1,
        grid=(n // tn,),
        in_specs=[pl.BlockSpec((rows, tn), lambda i, st: (0, i))],
        out_specs=pl.BlockSpec((rows, tn), lambda i, st: (0, i)),
    )
    out = pl.pallas_call(
        _dest_kernel,
        grid_spec=grid_spec,
        out_shape=jax.ShapeDtypeStruct((rows, n), jnp.int32),
        compiler_params=pltpu.CompilerParams(dimension_semantics=("arbitrary",)),
        name="moe_dest_rows",
    )(starts, route_t)
    return out[:2]


def _route_plan(route_t, cnt, te):
    n = route_t.shape[1]
    i32 = jnp.int32
    counts = cnt[0, ROUTE_COL0:ROUTE_COL0 + N_EXPERTS].astype(i32)
    ends = jnp.cumsum(counts)
    starts = ends - counts
    eids = jnp.arange(N_EXPERTS, dtype=i32)

    def lookup(table, idx):
        return jnp.sum(jnp.where(idx[None, :] == eids[:, None], table[:, None], 0), axis=0)

    dest = _dest_rows(route_t, starts, min(_DEST_TN, n))

    first_tile = starts // te
    last_tile = (ends - 1) // te
    items_e = jnp.where(counts > 0, last_tile - first_tile + 1, 0)
    item_end = jnp.cumsum(items_e)
    item_start = item_end - items_e
    total = item_end[-1]
    n_items = (2 * n) // te + N_EXPERTS - 1
    w = jnp.minimum(jnp.arange(n_items, dtype=i32), total - 1)
    item_e = jnp.sum((item_end[:, None] <= w[None, :]).astype(i32), axis=0)
    item_tile = lookup(first_tile, item_e) + w - lookup(item_start, item_e)
    lo = jnp.clip(lookup(starts, item_e) - item_tile * te, 0, te)
    hi = jnp.clip(lookup(ends, item_e) - item_tile * te, 0, te)
    return dest, (item_tile, item_e, lo, hi, total.reshape(1))


def _dispatch_kernel(d_ref, h_ref, xs_hbm, sem):
    tm = d_ref.shape[-1]

    def row_copy(r, dst):
        return pltpu.make_async_copy(h_ref.at[pl.ds(r, 1)], xs_hbm.at[pl.ds(dst, 1)], sem.at[0])

    def wait(r, carry):
        row_copy(r, 0).wait()
        row_copy(r, 0).wait()
        return carry

    for r in range(tm):
        row_copy(r, d_ref[0, 0, 0, r]).start(priority=0)
        row_copy(r, d_ref[1, 0, 0, r]).start(priority=1)
    lax.fori_loop(0, tm, wait, 0, unroll=8)


def _dispatch(h, dest, tm):
    n, d = h.shape
    assert n % tm == 0
    n_steps = n // tm
    return pl.pallas_call(
        _dispatch_kernel,
        grid=(n_steps,),
        in_specs=[pl.BlockSpec((2, 1, 1, tm), lambda i: (0, i, 0, 0), memory_space=pltpu.SMEM),
                  pl.BlockSpec((tm, d), lambda i: (i, 0))],
        out_specs=pl.BlockSpec(memory_space=pl.ANY),
        out_shape=jax.ShapeDtypeStruct((2 * n, d), h.dtype),
        scratch_shapes=[pltpu.SemaphoreType.DMA((1,))],
        compiler_params=pltpu.CompilerParams(
            dimension_semantics=("arbitrary",), vmem_limit_bytes=VMEM_LIMIT),
        name="moe_dispatch",
    )(dest.reshape(2, n_steps, 1, tm), h)


def _expert_kernel(tile_ref, exp_ref, lo_ref, hi_ref, tot_ref, xs_ref, wg_ref, wu_ref, wd_ref,
                   out_ref, wgb_ref, wub_ref, wdb_ref):
    w = pl.program_id(0)
    prev = jnp.maximum(w - 1, 0)
    live = w < tot_ref[0]
    new_expert = (w == 0) | (exp_ref[w] != exp_ref[prev])
    first_of_tile = (w == 0) | (tile_ref[w] != tile_ref[prev])

    @pl.when(live & new_expert)
    def _():
        wgb_ref[...] = wg_ref[0].astype(BF16)
        wub_ref[...] = wu_ref[0].astype(BF16)
        wdb_ref[...] = wd_ref[0].astype(BF16)

    @pl.when(live)
    def _():
        x_lo, x_hi = _unpack_pair(xs_ref[...])
        x = jnp.concatenate([x_lo.astype(BF16), x_hi.astype(BF16)], axis=1)
        hg = _dot(x, wgb_ref[...])
        hu = _dot(x, wub_ref[...])
        row = lax.broadcasted_iota(jnp.int32, (x.shape[0], 1), 0)
        mine = (row >= lo_ref[w]) & (row < hi_ref[w])
        act = jnp.where(mine, _silu(hg) * hu, 0.0).astype(BF16)
        part = _dot(act, wdb_ref[...])
        half_d = part.shape[1] // 2
        packed = _pack_pair(part[:, :half_d], part[:, half_d:])

        @pl.when(first_of_tile)
        def _():
            out_ref[...] = packed

        @pl.when(jnp.logical_not(first_of_tile))
        def _():
            out_ref[...] = jnp.where(mine, packed, out_ref[...])


def _experts(xs, items, p, te):
    rows, dp = xs.shape
    ne, d, f = p["w_gate"].shape
    assert d == 2 * dp
    item_tile, item_e, lo, hi, total = items
    n_items = item_tile.shape[0]
    tile_map = lambda w, t_, e_, lo_, hi_, n_: (t_[w], 0)
    exp_map = lambda w, t_, e_, lo_, hi_, n_: (e_[w], 0, 0)
    grid_spec = pltpu.PrefetchScalarGridSpec(
        num_scalar_prefetch=5,
        grid=(n_items,),
        in_specs=[
            pl.BlockSpec((te, dp), tile_map),
            pl.BlockSpec((1, d, f), exp_map),
            pl.BlockSpec((1, d, f), exp_map),
            pl.BlockSpec((1, f, d), exp_map),
        ],
        out_specs=pl.BlockSpec((te, dp), tile_map),
        scratch_shapes=[pltpu.VMEM((d, f), BF16), pltpu.VMEM((d, f), BF16), pltpu.VMEM((f, d), BF16)],
    )
    return pl.pallas_call(
        _expert_kernel,
        grid_spec=grid_spec,
        out_shape=jax.ShapeDtypeStruct((rows, dp), jnp.uint32),
        compiler_params=pltpu.CompilerParams(
            dimension_semantics=("arbitrary",), vmem_limit_bytes=VMEM_LIMIT),
        name="moe_experts",
    )(item_tile, item_e, lo, hi, total, xs, p["w_gate"], p["w_up"], p["w_down"])


def _combine_kernel(alpha, d_ref, dn_ref, h_ref, route_ref, rows_hbm, g_ref, b_ref, y_ref, o_ref, sem):
    i = pl.program_id(0)
    n_i = pl.num_programs(0)
    tm = h_ref.shape[0]
    slot = i % 2

    def row_copy(src_row, slot_, k, r):
        return pltpu.make_async_copy(rows_hbm.at[pl.ds(src_row, 1)], o_ref.at[slot_, k, pl.ds(r, 1)],
                                     sem.at[slot_])

    def start(dref, slot_):
        for r in range(tm):
            row_copy(dref[0, 0, 0, r], slot_, 0, r).start(priority=0)
            row_copy(dref[1, 0, 0, r], slot_, 1, r).start(priority=1)

    def wait(slot_):
        def body(r, carry):
            row_copy(0, slot_, 0, r).wait()
            row_copy(0, slot_, 1, r).wait()
            return carry
        lax.fori_loop(0, tm, body, 0, unroll=8)

    @pl.when(i == 0)
    def _():
        start(d_ref, 0)

    for nxt in (0, 1):
        @pl.when((i + 1 < n_i) & (slot == 1 - nxt))
        def _():
            start(dn_ref, nxt)

    wait(slot)
    route = route_ref[...]
    col = lax.broadcasted_iota(jnp.int32, route.shape, 1)
    w1 = jnp.sum(jnp.where(col == _R_W1, route, 0.0), axis=-1, keepdims=True)
    w2 = jnp.sum(jnp.where(col == _R_W2, route, 0.0), axis=-1, keepdims=True)
    a_lo, a_hi = _unpack_pair(o_ref[slot, 0])
    b_lo, b_hi = _unpack_pair(o_ref[slot, 1])
    moe = jnp.concatenate([w1 * a_lo + w2 * b_lo, w1 * a_hi + w2 * b_hi], axis=1)
    y_ref[...] = _layer_norm(alpha * h_ref[...] + moe, g_ref[...], b_ref[...])


def _combine(h, route, rows, dest, p, alpha, tm):
    n, d = h.shape
    assert n % tm == 0
    n_i = n // tm
    dest4 = dest.reshape(2, n_i, 1, tm)
    cur = pl.BlockSpec((2, 1, 1, tm), lambda i: (0, i, 0, 0), memory_space=pltpu.SMEM)
    nxt = pl.BlockSpec((2, 1, 1, tm), lambda i: (0, jnp.minimum(i + 1, n_i - 1), 0, 0),
                       memory_space=pltpu.SMEM)
    return pl.pallas_call(
        functools.partial(_combine_kernel, alpha),
        grid=(n_i,),
        in_specs=[cur, nxt,
                  pl.BlockSpec((tm, d), lambda i: (i, 0)),
                  pl.BlockSpec((tm, LANES), lambda i: (i, 0)),
                  pl.BlockSpec(memory_space=pl.ANY),
                  pl.BlockSpec((1, d), lambda i: (0, 0)),
                  pl.BlockSpec((1, d), lambda i: (0, 0))],
        out_specs=pl.BlockSpec((tm, d), lambda i: (i, 0)),
        out_shape=jax.ShapeDtypeStruct((n, d), F32),
        scratch_shapes=[pltpu.VMEM((2, 2, tm, d // 2), jnp.uint32), pltpu.SemaphoreType.DMA((2,))],
        compiler_params=pltpu.CompilerParams(
            dimension_semantics=("arbitrary",), vmem_limit_bytes=VMEM_LIMIT),
        name="moe_combine",
    )(dest4, dest4, h, route, rows, p["ln2_g"], p["ln2_b"])


def _pack_layer(w_in, b_in, w_dw, b_dw, lnc_g, lnc_b, w_conv_out, w_short, a_log, dt_bias, o_norm_g,
                w_o, w_out, ln1_g, ln1_b, w_rg, b_rg, w_re, b_re, w_gate, w_up, w_down, ln2_g, ln2_b):
    d = w_in.shape[0]
    c_conv = w_dw.shape[1]
    qkv_dim = w_short.shape[1]
    o_qkv = 2 * c_conv
    o_ba = o_qkv + qkv_dim
    o_tail = o_ba + 2 * N_HEADS

    def pad_cols(a, width):
        return jnp.pad(a, ((0, 0), (0, width - a.shape[1])))

    b2 = b_in[None, :].astype(F32)
    nh = N_HEADS
    zeros_h = jnp.zeros((nh,), F32)
    head_params = jnp.stack([jnp.concatenate([zeros_h, a_log.astype(F32)]),
                             jnp.concatenate([zeros_h, dt_bias.astype(F32)])])
    half = _TN // 2
    w_main = jnp.concatenate([w_in[:, :o_ba], w_in[:, o_tail:]], axis=1).astype(BF16)
    w_blk = w_main.reshape(d, -1, half).transpose(1, 0, 2)
    w_ba = w_in[:, o_ba:o_tail].astype(BF16)
    w_router = jnp.concatenate([w_rg, w_re], axis=1).astype(F32)
    b_router = jnp.concatenate([b_rg, b_re])[None, :].astype(F32)
    return dict(
        c_conv=c_conv, qkv_dim=qkv_dim,
        w_blk=w_blk, b_in=b2,
        w_ba=pad_cols(w_ba, LANES),
        b_ba=pad_cols(b2[:, o_ba:o_tail], LANES),
        w_bat=w_ba.T,
        b_bat=jnp.broadcast_to(b_in[o_ba:o_tail, None].astype(F32), (2 * nh, LANES)),
        b_tail=b2[:, o_tail:],
        p_row=pad_cols(head_params, LANES),
        p_col=pad_cols(head_params.T, LANES),
        w_dw=w_dw.astype(F32), b_dw=b_dw[None, :].astype(F32),
        lnc_g=lnc_g[None, :].astype(F32), lnc_b=lnc_b[None, :].astype(F32),
        w_conv_out=w_conv_out.astype(BF16),
        w_short=w_short.astype(F32),
        o_norm_g=o_norm_g[None, :].astype(F32),
        w_o=w_o.astype(BF16), w_out=w_out.astype(BF16),
        ln1_g=ln1_g[None, :].astype(F32), ln1_b=ln1_b[None, :].astype(F32),
        w_router=pad_cols(w_router, LANES), b_router=pad_cols(b_router, LANES),
        w_gate=w_gate.astype(F32), w_up=w_up.astype(F32), w_down=w_down.astype(F32),
        ln2_g=ln2_g[None, :].astype(F32), ln2_b=ln2_b[None, :].astype(F32),
    )


_MERGE_TM = 512
_COMBINE_TM = 256
_DEST_TN = 2048
_EXPERT_TE = 256
_DISPATCH_TM = 256


def _block(x, conv_buf, short_buf, s0, p, alpha, tm, conv_tt):
    b, t, d = x.shape
    n = b * t
    x2 = x.reshape(n, d).astype(F32)
    u, qkv, bg, bgt, gsil, sgc, sgd = _inproj(x2, p, tm)
    c_conv = u.shape[1]
    u3 = u.reshape(b, t, c_conv)
    qkv3 = qkv.reshape(b, t, -1)
    convg = _conv_branch(u3, conv_buf, sgc.reshape(b, t, d), p, conv_tt)
    og, s_new = _delta_branch(qkv3, short_buf, bg, bgt, gsil.reshape(b, t, -1), s0, p)
    h, hp, route, route_t, cnt = _merge(convg.reshape(n, d), og.reshape(n, -1), sgd, x2, p, alpha, min(_MERGE_TM, n))
    dest, items = _route_plan(route_t, cnt, _EXPERT_TE)
    xs = _dispatch(hp, dest, min(_DISPATCH_TM, n))
    rows = _experts(xs, items, p, _EXPERT_TE)
    y = _combine(h, route, rows, dest, p, alpha, min(_COMBINE_TM, n))
    kc = conv_buf.shape[1]
    ks = short_buf.shape[1]
    assert t >= kc and t >= ks
    return (y.reshape(b, t, d).astype(x.dtype), u3[:, t - kc:].astype(x.dtype),
            qkv3[:, t - ks:].astype(x.dtype), s_new.astype(s0.dtype))


def kernel(x_prompt, x_sample, cache_conv, cache_short, state_delta, w_in, b_in, w_dw, b_dw, lnc_g, lnc_b, w_conv_out, w_short, a_log, dt_bias, o_norm_g, w_o, w_out, ln1_g, ln1_b, w_rg, b_rg, w_re, b_re, w_gate, w_up, w_down, ln2_g, ln2_b):
    weights = (w_in, b_in, w_dw, b_dw, lnc_g, lnc_b, w_conv_out, w_short, a_log, dt_bias, o_norm_g,
               w_o, w_out, ln1_g, ln1_b, w_rg, b_rg, w_re, b_re, w_gate, w_up, w_down, ln2_g, ln2_b)
    depth = w_in.shape[0]
    alpha = (2.0 * depth) ** 0.25
    yp, ys = x_prompt, x_sample
    bp = x_prompt.shape[0]
    outs = [[] for _ in range(6)]
    for l in range(depth):
        p = _pack_layer(*(wt[l] for wt in weights))
        zc = jnp.zeros((bp,) + cache_conv.shape[2:], x_prompt.dtype)
        zs = jnp.zeros((bp,) + cache_short.shape[2:], x_prompt.dtype)
        zd = jnp.zeros((bp,) + state_delta.shape[2:], state_delta.dtype)
        yp, c, s, dl = _block(yp, zc, zs, zd, p, alpha, 1024, 256)
        outs[0].append(c), outs[1].append(s), outs[2].append(dl)
        ys, c, s, dl = _block(ys, cache_conv[l], cache_short[l], state_delta[l], p, alpha, 1024, 64)
        outs[3].append(c), outs[4].append(s), outs[5].append(dl)
    return (yp, ys) + tuple(jnp.stack(o) for o in outs)
```

```python
import functools

import jax
import jax.numpy as jnp
from jax import lax
from jax.experimental import pallas as pl
from jax.experimental.pallas import tpu as pltpu

F32 = jnp.float32
BF16 = jnp.bfloat16

CHUNK = 64
N_HEADS = 8
HEAD_DIM = 128
N_GROUPS = 4
EXP_PER_GROUP = 8
N_EXPERTS = N_GROUPS * EXP_PER_GROUP
LN_EPS = 1e-5
NORM_EPS = 1e-6
LANES = 128
ROUTE_COL0 = N_GROUPS
VMEM_LIMIT = 56 * 1024 * 1024


def _dot(a, b):
    return jnp.dot(a, b, preferred_element_type=F32)


def _dot_nt(a, b):
    return lax.dot_general(a, b, (((1,), (1,)), ((), ())), preferred_element_type=F32)


def _dot_tn(a, b):
    return lax.dot_general(a, b, (((0,), (0,)), ((), ())), preferred_element_type=F32)


def _split3(x):
    hi = x.astype(BF16)
    r1 = x - hi.astype(F32)
    mid = r1.astype(BF16)
    lo = (r1 - mid.astype(F32)).astype(BF16)
    return hi, mid, lo


def _pack_pair(lo, hi):
    lo_bits = lax.bitcast_convert_type(lo.astype(BF16).astype(F32), jnp.uint32)
    hi_bits = lax.bitcast_convert_type(hi.astype(BF16).astype(F32), jnp.uint32)
    return (hi_bits & jnp.uint32(0xFFFF0000)) | (lo_bits >> 16)


def _unpack_pair(packed):
    lo = lax.bitcast_convert_type(packed << 16, F32)
    hi = lax.bitcast_convert_type(packed & jnp.uint32(0xFFFF0000), F32)
    return lo, hi


def _sigmoid(x):
    return 1.0 / (1.0 + jnp.exp(-x))


def _silu(x):
    return x * _sigmoid(x)


def _softplus(x):
    return jnp.maximum(x, 0.0) + jnp.log1p(jnp.exp(-jnp.abs(x)))


def _layer_norm(x, g, b):
    mu = jnp.mean(x, axis=-1, keepdims=True)
    xc = x - mu
    var = jnp.mean(xc * xc, axis=-1, keepdims=True)
    return xc * lax.rsqrt(var + LN_EPS) * g + b


def _clamp(v, lo, hi):
    return jnp.minimum(jnp.maximum(v, lo), hi)


_TN = 512
_J_GLU, _J_QKV, _J_BA, _J_GO, _J_GC, _J_GD, _J_END = 0, 4, 10, 11, 13, 17, 21


def _wide_dot(x, w_ref):
    return jnp.concatenate([_dot(x, w_ref[b]) for b in range(w_ref.shape[0])], axis=1)


def _inproj_kernel(x_ref, wga_ref, wgb_ref, wqkv_ref, wba_ref, wbat_ref, wtail_ref,
                   bga_ref, bgb_ref, bqkv_ref, bba_ref, bbat_ref, btail_ref, prow_ref, pcol_ref,
                   u_ref, qkv_ref, bg_ref, bgt_ref, gsil_ref, sgc_ref, sgd_ref, xb_ref):
    j = pl.program_id(1)

    @pl.when(j == 0)
    def _():
        xb_ref[...] = x_ref[...].astype(BF16)

    @pl.when(j < _J_QKV)
    def _():
        xb = xb_ref[...]
        value = _dot(xb, wga_ref[0]) + bga_ref[...]
        gate = _dot(xb, wgb_ref[0]) + bgb_ref[...]
        u_ref[...] = value * _sigmoid(gate)

    @pl.when((j >= _J_QKV) & (j < _J_BA))
    def _():
        qkv_ref[...] = _wide_dot(xb_ref[...], wqkv_ref) + bqkv_ref[...]

    @pl.when(j == _J_BA)
    def _():
        xb = xb_ref[...]
        z = _dot(xb, wba_ref[...]) + bba_ref[...]
        col = lax.broadcasted_iota(jnp.int32, z.shape, 1)
        g = -jnp.exp(prow_ref[0:1, :]) * _softplus(z + prow_ref[1:2, :])
        bg_ref[...] = jnp.where(col < N_HEADS, _sigmoid(z), g)
        zt = _dot_nt(wbat_ref[...], xb) + bbat_ref[:, 0:1]
        row = lax.broadcasted_iota(jnp.int32, zt.shape, 0)
        gt = -jnp.exp(pcol_ref[:, 0:1]) * _softplus(zt + pcol_ref[:, 1:2])
        bgt_ref[...] = jnp.where(row < N_HEADS, _sigmoid(zt), gt)

    @pl.when((j >= _J_GO) & (j < _J_GC))
    def _():
        z = _wide_dot(xb_ref[...], wtail_ref) + btail_ref[...]
        gsil_ref[...] = _silu(z).astype(BF16)

    @pl.when((j >= _J_GC) & (j < _J_GD))
    def _():
        z = _wide_dot(xb_ref[...], wtail_ref) + btail_ref[...]
        sgc_ref[...] = _sigmoid(z).astype(BF16)

    @pl.when(j >= _J_GD)
    def _():
        z = _wide_dot(xb_ref[...], wtail_ref) + btail_ref[...]
        sgd_ref[...] = _sigmoid(z).astype(BF16)


def _inproj(x, pk, tm):
    n, d = x.shape
    c_conv, qkv_dim = pk["c_conv"], pk["qkv_dim"]
    val_dim = N_HEADS * HEAD_DIM
    half = _TN // 2
    n_glu = _J_QKV - _J_GLU
    assert n % tm == 0 and c_conv == n_glu * half and (2 * c_conv) % _TN == 0
    qkv_blk0 = 2 * c_conv // _TN
    tail_blk0 = (2 * c_conv + qkv_dim) // _TN
    assert (2 * c_conv + qkv_dim) % _TN == 0

    def cm(lo, hi):
        return lambda i, j: (0, _clamp(j - lo, 0, hi - lo - 1))

    def om(lo, hi):
        return lambda i, j: (i, _clamp(j - lo, 0, hi - lo - 1))

    in_specs = [
        pl.BlockSpec((tm, d), lambda i, j: (i, 0)),
        pl.BlockSpec((1, d, half), lambda i, j: (_clamp(j, 0, n_glu - 1), 0, 0)),
        pl.BlockSpec((1, d, half), lambda i, j: (n_glu + _clamp(j, 0, n_glu - 1), 0, 0)),
        pl.BlockSpec((2, d, half), lambda i, j: (qkv_blk0 + _clamp(j - _J_QKV, 0, _J_BA - _J_QKV - 1), 0, 0)),
        pl.BlockSpec((d, LANES), lambda i, j: (0, 0)),
        pl.BlockSpec((2 * N_HEADS, d), lambda i, j: (0, 0)),
        pl.BlockSpec((2, d, half), lambda i, j: (tail_blk0 + _clamp(j - _J_GO, 0, _J_END - _J_GO - 1), 0, 0)),
        pl.BlockSpec((1, half), lambda i, j: (0, _clamp(j, 0, n_glu - 1))),
        pl.BlockSpec((1, half), lambda i, j: (0, n_glu + _clamp(j, 0, n_glu - 1))),
        pl.BlockSpec((1, _TN), lambda i, j: (0, qkv_blk0 + _clamp(j - _J_QKV, 0, _J_BA - _J_QKV - 1))),
        pl.BlockSpec((1, LANES), lambda i, j: (0, 0)),
        pl.BlockSpec((2 * N_HEADS, LANES), lambda i, j: (0, 0)),
        pl.BlockSpec((1, _TN), cm(_J_GO, _J_END)),
        pl.BlockSpec((2, LANES), lambda i, j: (0, 0)),
        pl.BlockSpec((2 * N_HEADS, LANES), lambda i, j: (0, 0)),
    ]
    out_shape = [
        jax.ShapeDtypeStruct((n, c_conv), F32),
        jax.ShapeDtypeStruct((n, qkv_dim), F32),
        jax.ShapeDtypeStruct((n, LANES), F32),
        jax.ShapeDtypeStruct((2 * N_HEADS, n), F32),
        jax.ShapeDtypeStruct((n, val_dim), BF16),
        jax.ShapeDtypeStruct((n, d), BF16),
        jax.ShapeDtypeStruct((n, d), BF16),
    ]
    out_specs = [
        pl.BlockSpec((tm, _TN // 2), om(_J_GLU, _J_QKV)),
        pl.BlockSpec((tm, _TN), om(_J_QKV, _J_BA)),
        pl.BlockSpec((tm, LANES), lambda i, j: (i, 0)),
        pl.BlockSpec((2 * N_HEADS, tm), lambda i, j: (0, i)),
        pl.BlockSpec((tm, _TN), om(_J_GO, _J_GC)),
        pl.BlockSpec((tm, _TN), om(_J_GC, _J_GD)),
        pl.BlockSpec((tm, _TN), om(_J_GD, _J_END)),
    ]
    return pl.pallas_call(
        _inproj_kernel,
        grid=(n // tm, _J_END),
        in_specs=in_specs,
        out_specs=out_specs,
        out_shape=out_shape,
        scratch_shapes=[pltpu.VMEM((tm, d), BF16)],
        compiler_params=pltpu.CompilerParams(
            dimension_semantics=("arbitrary", "arbitrary"), vmem_limit_bytes=VMEM_LIMIT),
        name="inproj",
    )(x, pk["w_blk"], pk["w_blk"], pk["w_blk"], pk["w_ba"], pk["w_bat"], pk["w_blk"],
      pk["b_in"], pk["b_in"], pk["b_in"], pk["b_ba"], pk["b_bat"], pk["b_tail"], pk["p_row"], pk["p_col"])


_HALO = 32
_CONV_RB = 64
_CONV_FB = 8
_CONV_SEQS = 4


def _conv_kernel(independent, u_ref, cache_ref, wdw_ref, bdw_ref, lng_ref, lnb_ref, wco_ref, sgc_ref,
                 out_ref, xt_ref, yt_ref, cn_ref):
    t = pl.program_id(1)
    nseq, tt, c_conv = u_ref.shape
    width = wdw_ref.shape[0]
    nfold = c_conv // LANES
    first = _HALO - (width - 1)

    def load_history(q):
        for s in range(nfold):
            xt_ref[pl.ds(s, _HALO, stride=nfold), :] = cache_ref[q, :, s * LANES:(s + 1) * LANES]

    def frames(ib, carry):
        f0 = ib * _CONV_FB
        acc = [None] * _CONV_FB
        for k in range(width):
            wk = wdw_ref[k]
            for j in range(_CONV_FB):
                row = pl.multiple_of((first + f0 + j + k) * nfold, nfold)
                term = wk * xt_ref[pl.ds(row, nfold), :]
                acc[j] = term if acc[j] is None else acc[j] + term
        for j in range(_CONV_FB):
            yt_ref[pl.ds(pl.multiple_of((f0 + j) * nfold, nfold), nfold), :] = acc[j]
        return carry

    if not independent:
        pl.when(t == 0)(lambda: load_history(0))

    for q in range(nseq):
        if independent:
            load_history(q)
        for s in range(nfold):
            xt_ref[pl.ds(_HALO * nfold + s, tt, stride=nfold), :] = u_ref[q, :, s * LANES:(s + 1) * LANES]
        lax.fori_loop(0, tt // _CONV_FB, frames, 0)
        for rb in range(tt // _CONV_RB):
            r0 = rb * _CONV_RB
            y = jnp.concatenate(
                [yt_ref[pl.ds(r0 * nfold + s, _CONV_RB, stride=nfold), :] for s in range(nfold)], axis=1)
            y = _layer_norm(y + bdw_ref[...], lng_ref[...], lnb_ref[...])
            cn_ref[q * tt + r0:q * tt + r0 + _CONV_RB, :] = _silu(y).astype(BF16)

    co = _dot(cn_ref[...], wco_ref[...])
    for q in range(nseq):
        out_ref[q] = (co[q * tt:(q + 1) * tt] * sgc_ref[q].astype(F32)).astype(BF16)
    if not independent:
        xt_ref[0:_HALO * nfold, :] = xt_ref[tt * nfold:(tt + _HALO) * nfold, :]


def _conv_branch(u, cache, sgc, p, tt):
    b, t, c_conv = u.shape
    d = sgc.shape[-1]
    width = p["w_dw"].shape[0]
    assert t % tt == 0 and tt % _CONV_RB == 0 and tt >= _HALO and width - 1 <= _HALO
    assert c_conv % (8 * LANES) == 0
    nfold = c_conv // LANES
    cache_p = jnp.pad(cache.astype(F32), ((0, 0), (_HALO - (width - 1), 0), (0, 0)))
    independent = t == tt
    nseq = _CONV_SEQS if independent and b % _CONV_SEQS == 0 else 1
    full2 = lambda shape: pl.BlockSpec(shape, lambda i, j: (0, 0))
    return pl.pallas_call(
        functools.partial(_conv_kernel, independent),
        grid=(b // nseq, t // tt),
        in_specs=[
            pl.BlockSpec((nseq, tt, c_conv), lambda i, j: (i, j, 0)),
            pl.BlockSpec((nseq, _HALO, c_conv), lambda i, j: (i, 0, 0)),
            pl.BlockSpec((width, nfold, LANES), lambda i, j: (0, 0, 0)),
            full2((1, c_conv)), full2((1, c_conv)), full2((1, c_conv)),
            full2((c_conv, d)),
            pl.BlockSpec((nseq, tt, d), lambda i, j: (i, j, 0)),
        ],
        out_specs=pl.BlockSpec((nseq, tt, d), lambda i, j: (i, j, 0)),
        out_shape=jax.ShapeDtypeStruct((b, t, d), BF16),
        scratch_shapes=[pltpu.VMEM(((tt + _HALO) * nfold, LANES), F32),
                        pltpu.VMEM((tt * nfold, LANES), F32),
                        pltpu.VMEM((nseq * tt, c_conv), BF16)],
        compiler_params=pltpu.CompilerParams(
            dimension_semantics=("arbitrary", "arbitrary"), vmem_limit_bytes=VMEM_LIMIT),
        name="conv_branch",
    )(u, cache_p, p["w_dw"].reshape(width, nfold, LANES), p["b_dw"], p["lnc_g"], p["lnc_b"],
      p["w_conv_out"], sgc)


_SHORT_PAD = 8
_DELTA_CPS = 4


def _delta_kernel(n_steps, cps, independent, qkv_ref, cache_ref, wsh_ref, bg_ref, bgt_ref, gsil_ref, ong_ref,
                  s0_ref, o_ref, sfin_ref, xb_ref, s_ref):
    c = pl.program_id(1)
    ck = CHUNK
    key_dim = N_HEADS * HEAD_DIM
    sw = wsh_ref.shape[0]
    unit_rows = _SHORT_PAD + ck
    ub = lambda cc: cc * unit_rows

    if independent:
        for cc in range(cps):
            xb_ref[ub(cc):ub(cc) + _SHORT_PAD, :] = cache_ref[cc]
            xb_ref[ub(cc) + _SHORT_PAD:ub(cc) + unit_rows, :] = qkv_ref[cc]
    else:
        @pl.when(c == 0)
        def _():
            s_ref[...] = s0_ref[0]
            xb_ref[0:_SHORT_PAD, :] = cache_ref[0]

        for cc in range(cps):
            xb_ref[ub(cc) + _SHORT_PAD:ub(cc) + unit_rows, :] = qkv_ref[0, cc * ck:(cc + 1) * ck, :]
            if cc > 0:
                xb_ref[ub(cc):ub(cc) + _SHORT_PAD, :] = qkv_ref[0, cc * ck - _SHORT_PAD:cc * ck, :]

    def conv_cols(cc, lo):
        lanes = slice(lo, lo + HEAD_DIM)
        base = ub(cc) + _SHORT_PAD
        acc = wsh_ref[sw - 1:sw, lanes] * xb_ref[base:base + ck, lanes]
        for k in range(sw - 1):
            r = base - (sw - 1) + k
            acc = acc + wsh_ref[k:k + 1, lanes] * xb_ref[r:r + ck, lanes]
        return _silu(acc)

    rowp = lax.broadcasted_iota(jnp.int32, (ck, 2 * ck), 0)
    lanep = lax.broadcasted_iota(jnp.int32, (ck, 2 * ck), 1)
    odd = lanep >= ck
    lcol = jnp.where(odd, lanep - ck, lanep)
    incl = rowp >= lcol
    strict = rowp > lcol
    eye = jnp.where(rowp == lcol, 1.0, 0.0).astype(F32)
    bd8 = (rowp // 8) == (lcol // 8)
    lvl = [((rowp // (2 * s)) == (lcol // (2 * s))) & ((rowp // s) != (lcol // s)) for s in (8, 16, 32)]
    ri = lax.broadcasted_iota(jnp.int32, (ck, ck), 0)
    ci = lax.broadcasted_iota(jnp.int32, (ck, ck), 1)
    tri_l = jnp.where(ri >= ci, 1.0, 0.0).astype(BF16)
    r2 = lax.broadcasted_iota(jnp.int32, (2 * ck, 2 * ck), 0)
    c2 = lax.broadcasted_iota(jnp.int32, (2 * ck, 2 * ck), 1)
    tri_u2 = jnp.where(((r2 >= ck) == (c2 >= ck)) & (r2 <= c2), 1.0, 0.0).astype(BF16)

    bf = lambda m: m.astype(BF16)

    def block_diag(pm):
        return bf(jnp.concatenate([jnp.where(odd, 0.0, pm), jnp.where(odd, pm, 0.0)], axis=0))

    def block_rows(top, bottom):
        z = jnp.zeros_like(top)
        return jnp.concatenate([jnp.concatenate([top, z], axis=1), jnp.concatenate([z, bottom], axis=1)], axis=0)

    bg = [bg_ref[cc] if independent else bg_ref[0, cc * ck:(cc + 1) * ck, :] for cc in range(cps)]
    gc_cols = [sum(_dot(tri_l, part) for part in _split3(bg[cc])) for cc in range(cps)]
    gc_rows = [sum(_dot(part, tri_u2) for part in _split3(bgt_ref[cc])) for cc in range(cps)]

    nh2 = N_HEADS // 2
    heads = range(cps * N_HEADS)
    pairs = range(cps * nh2)
    hcc = lambda i: (i // N_HEADS, i % N_HEADS)
    ev = lambda j: (j // nh2) * N_HEADS + 2 * (j % nh2)
    q = [conv_cols(hcc(i)[0], hcc(i)[1] * HEAD_DIM) for i in heads]
    k = [conv_cols(hcc(i)[0], key_dim + hcc(i)[1] * HEAD_DIM) for i in heads]
    v = [conv_cols(hcc(i)[0], 2 * key_dim + hcc(i)[1] * HEAD_DIM) for i in heads]
    q = [x * lax.rsqrt(jnp.sum(x * x, axis=-1, keepdims=True) + NORM_EPS) * (HEAD_DIM ** -0.5) for x in q]
    k = [x * lax.rsqrt(jnp.sum(x * x, axis=-1, keepdims=True) + NORM_EPS) for x in k]
    beta = [bg[hcc(i)[0]][:, hcc(i)[1]:hcc(i)[1] + 1] for i in heads]
    gcc = [gc_cols[hcc(i)[0]][:, N_HEADS + hcc(i)[1]:N_HEADS + hcc(i)[1] + 1] for i in heads]
    kb = [k[i] * beta[i] for i in heads]
    kbf = [bf(x) for x in k]
    kk = [block_rows(kbf[ev(j)], kbf[ev(j) + 1]) for j in pairs]
    gcc2 = [jnp.where(odd, gcc[ev(j) + 1], gcc[ev(j)]) for j in pairs]
    gcr2 = [gc_rows[j // nh2][nh2 + j % nh2:nh2 + j % nh2 + 1, :] for j in pairs]
    decay = [jnp.exp(jnp.where(incl, gcc2[j] - gcr2[j], -jnp.inf)) for j in pairs]
    a = [jnp.where(strict,
                   _dot_nt(jnp.concatenate([bf(kb[ev(j)]), bf(kb[ev(j) + 1])], axis=1), kk[j]) * decay[j], 0.0)
         for j in pairs]
    qk = [_dot_nt(jnp.concatenate([bf(q[ev(j)]), bf(q[ev(j) + 1])], axis=1), kk[j]) * decay[j] for j in pairs]

    ad = [jnp.where(bd8, x, 0.0) for x in a]
    adb = [bf(x) for x in ad]
    a2 = [_dot(adb[j], block_diag(ad[j])) for j in pairs]
    a2d = [block_diag(x) for x in a2]
    a3 = [_dot(adb[j], a2d[j]) for j in pairs]
    a4 = [_dot(bf(a2[j]), a2d[j]) for j in pairs]
    t = [eye - ad[j] + a2[j] - a3[j] for j in pairs]
    t = [t[j] + _dot(bf(t[j]), block_diag(a4[j])) for j in pairs]
    for m in lvl:
        x = [_dot(bf(jnp.where(m, a[j], 0.0)), block_diag(t[j])) for j in pairs]
        t = [t[j] - _dot(bf(t[j]), block_diag(x[j])) for j in pairs]

    egc = [jnp.exp(x) for x in gcc]
    rhs = [bf(jnp.concatenate([v[i] * beta[i], kb[i] * egc[i]], axis=1)) for i in heads]
    sol2 = [_dot(bf(t[j]), block_rows(rhs[ev(j)], rhs[ev(j) + 1])) for j in pairs]
    sol = [sol2[(i // N_HEADS) * nh2 + (i % N_HEADS) // 2][:, (i % 2) * 2 * HEAD_DIM:(i % 2 + 1) * 2 * HEAD_DIM]
           for i in heads]
    g_last = [x[ck - 1:ck, :] for x in gcc]
    k_dec = [bf(k[i] * jnp.exp(g_last[i] - gcc[i])) for i in heads]
    wq_lhs = [bf(jnp.concatenate([sol[i][:, HEAD_DIM:], q[i] * egc[i]], axis=0)) for i in heads]
    qkb = [bf(x) for x in qk]

    def o_rows(ref, cc, lanes):
        return ref.at[cc, :, lanes] if independent else ref.at[0, cc * ck:(cc + 1) * ck, lanes]

    groups = [list(range(cps))] if independent else [[cc] for cc in range(cps)]
    carried = None if independent else [s_ref[h] for h in range(N_HEADS)]
    for group in groups:
        units = [(cc, h) for cc in group for h in range(N_HEADS)]
        s_in = [s0_ref[cc, h] if independent else carried[h] for cc, h in units]
        sb = [bf(x) for x in s_in]
        wq = [_dot(wq_lhs[cc * N_HEADS + h], sb[i]) for i, (cc, h) in enumerate(units)]
        vb = [bf(sol[cc * N_HEADS + h][:, :HEAD_DIM] - wq[i][:ck]) for i, (cc, h) in enumerate(units)]
        o2 = [_dot(qkb[cc * nh2 + p], block_rows(vb[g * N_HEADS + 2 * p], vb[g * N_HEADS + 2 * p + 1]))
              for g, cc in enumerate(group) for p in range(nh2)]
        s_out = [s_in[i] * jnp.exp(g_last[cc * N_HEADS + h]) + _dot_tn(k_dec[cc * N_HEADS + h], vb[i])
                 for i, (cc, h) in enumerate(units)]
        for i, (cc, h) in enumerate(units):
            g = i // N_HEADS
            o = wq[i][ck:] + o2[g * nh2 + h // 2][:, (h % 2) * HEAD_DIM:(h % 2 + 1) * HEAD_DIM]
            on = o * lax.rsqrt(jnp.mean(o * o, axis=-1, keepdims=True) + NORM_EPS) * ong_ref[...]
            lanes = slice(h * HEAD_DIM, (h + 1) * HEAD_DIM)
            o_rows(o_ref, cc, lanes)[...] = (on * o_rows(gsil_ref, cc, lanes)[...].astype(F32)).astype(BF16)
            if independent:
                sfin_ref[cc, h] = s_out[i]
        if not independent:
            carried = s_out

    if not independent:
        for h in range(N_HEADS):
            s_ref[h] = carried[h]
        xb_ref[0:_SHORT_PAD, :] = qkv_ref[0, cps * ck - _SHORT_PAD:cps * ck, :]

        @pl.when(c == n_steps - 1)
        def _():
            sfin_ref[0] = s_ref[...]


def _delta_branch(qkv, cache, bg, bgt, gsil, s0, p):
    b, t, qkv_dim = qkv.shape
    assert t % CHUNK == 0
    nc = t // CHUNK
    sw = p["w_short"].shape[0]
    cache_p = jnp.pad(cache.astype(F32), ((0, 0), (_SHORT_PAD - (sw - 1), 0), (0, 0)))
    val_dim = N_HEADS * HEAD_DIM
    bgt3 = bgt.reshape(2 * N_HEADS, b * nc, CHUNK).transpose(1, 0, 2).reshape(b * nc, N_HEADS, 2 * CHUNK)
    independent = nc == 1
    if independent:
        cps = _DELTA_CPS if b % _DELTA_CPS == 0 else 1
        grid = (b // cps, 1)
        seq = lambda i, j: (i, 0, 0)
        blk = lambda width: pl.BlockSpec((cps, CHUNK, width), seq)
        cache_spec = pl.BlockSpec((cps, _SHORT_PAD, qkv_dim), seq)
        bgt_spec = pl.BlockSpec((cps, N_HEADS, 2 * CHUNK), seq)
        state_spec = pl.BlockSpec((cps, N_HEADS, HEAD_DIM, HEAD_DIM), lambda i, j: (i, 0, 0, 0))
        n_steps = 1
    else:
        cps = _DELTA_CPS if nc % _DELTA_CPS == 0 else 1
        n_steps = nc // cps
        grid = (b, n_steps)
        blk = lambda width: pl.BlockSpec((1, cps * CHUNK, width), lambda i, j: (i, j, 0))
        cache_spec = pl.BlockSpec((1, _SHORT_PAD, qkv_dim), lambda i, j: (i, 0, 0))
        bgt_spec = pl.BlockSpec((cps, N_HEADS, 2 * CHUNK), lambda i, j: (i * n_steps + j, 0, 0))
        state_spec = pl.BlockSpec((1, N_HEADS, HEAD_DIM, HEAD_DIM), lambda i, j: (i, 0, 0, 0))
    return pl.pallas_call(
        functools.partial(_delta_kernel, n_steps, cps, independent),
        grid=grid,
        in_specs=[
            blk(qkv_dim),
            cache_spec,
            pl.BlockSpec((sw, qkv_dim), lambda i, j: (0, 0)),
            blk(LANES),
            bgt_spec,
            blk(val_dim),
            pl.BlockSpec((1, HEAD_DIM), lambda i, j: (0, 0)),
            state_spec,
        ],
        out_specs=[blk(val_dim), state_spec],
        out_shape=[
            jax.ShapeDtypeStruct((b, t, val_dim), BF16),
            jax.ShapeDtypeStruct((b, N_HEADS, HEAD_DIM, HEAD_DIM), F32),
        ],
        scratch_shapes=[pltpu.VMEM((cps * (_SHORT_PAD + CHUNK), qkv_dim), F32),
                        pltpu.VMEM((N_HEADS, HEAD_DIM, HEAD_DIM), F32)],
        compiler_params=pltpu.CompilerParams(
            dimension_semantics=("arbitrary", "arbitrary"), vmem_limit_bytes=VMEM_LIMIT),
        name="delta_rule",
    )(qkv, cache_p, p["w_short"], bg.reshape(b, t, LANES), bgt3, gsil, p["o_norm_g"], s0.astype(F32))


_R_E1, _R_E2, _R_RANK1, _R_RANK2, _R_W1, _R_W2 = range(6)


def _merge_kernel(alpha, convg_ref, og_ref, sgd_ref, x_ref, wo_ref, wout_ref, g_ref, b_ref,
                  wr_ref, br_ref, h_ref, hp_ref, route_ref, routet_ref, cnt_ref, carry_ref):
    @pl.when(pl.program_id(0) == 0)
    def _():
        carry_ref[...] = jnp.zeros_like(carry_ref)

    d_out = _dot(og_ref[...], wo_ref[...])
    merged = convg_ref[...].astype(F32) + d_out * sgd_ref[...].astype(F32)
    mix = _dot(merged.astype(BF16), wout_ref[...])
    h = _layer_norm(alpha * x_ref[...] + mix, g_ref[...], b_ref[...])
    h_ref[...] = h
    half_d = h.shape[1] // 2
    hp_ref[...] = _pack_pair(h[:, :half_d], h[:, half_d:])

    h_hi, h_mid, _ = _split3(h)
    w_hi, w_mid, _ = _split3(wr_ref[...])
    logits = _dot(h_hi, w_hi) + _dot(h_mid, w_hi) + _dot(h_hi, w_mid) + br_ref[...]
    tm = logits.shape[0]
    col = lax.broadcasted_iota(jnp.int32, logits.shape, 1).astype(F32)
    big = float(LANES)
    is_g = col < N_GROUPS
    mg = jnp.max(jnp.where(is_g, logits, -jnp.inf), axis=-1, keepdims=True)
    sg = jnp.sum(jnp.where(is_g, jnp.exp(jnp.where(is_g, logits, mg) - mg), 0.0), axis=-1, keepdims=True)
    pg_top = 1.0 / sg
    gidx = jnp.min(jnp.where(is_g & (logits == mg), col, big), axis=-1, keepdims=True)
    lo = ROUTE_COL0 + EXP_PER_GROUP * gidx
    sel = (col >= lo) & (col < lo + EXP_PER_GROUP)
    le = jnp.where(sel, logits, -jnp.inf)
    m1 = jnp.max(le, axis=-1, keepdims=True)
    i1 = jnp.min(jnp.where(le == m1, col, big), axis=-1, keepdims=True)
    le2 = jnp.where(col == i1, -jnp.inf, le)
    m2 = jnp.max(le2, axis=-1, keepdims=True)
    i2 = jnp.min(jnp.where(le2 == m2, col, big), axis=-1, keepdims=True)
    e2 = jnp.exp(m2 - m1)
    den = 1.0 + e2
    w1 = pg_top / den
    w2 = pg_top * e2 / den

    hit1 = col == i1
    hit2 = col == i2
    member = jnp.where(hit1 | hit2, 1.0, 0.0)
    ri = lax.broadcasted_iota(jnp.int32, (tm, tm), 0)
    ci = lax.broadcasted_iota(jnp.int32, (tm, tm), 1)
    earlier = jnp.where(ri > ci, 1.0, 0.0).astype(BF16)
    before = _dot(earlier, member.astype(BF16)) + carry_ref[...]
    rank1 = jnp.sum(jnp.where(hit1, before, 0.0), axis=-1, keepdims=True)
    rank2 = jnp.sum(jnp.where(hit2, before, 0.0), axis=-1, keepdims=True)
    carry_ref[...] += jnp.sum(member, axis=0, keepdims=True)
    cnt_ref[...] = jnp.broadcast_to(carry_ref[...], cnt_ref.shape)

    fields = (i1 - ROUTE_COL0, i2 - ROUTE_COL0, rank1, rank2, w1, w2)
    route = jnp.zeros_like(logits)
    for c, val in enumerate(fields):
        route = jnp.where(col == float(c), val, route)
    route_ref[...] = route
    sr = lax.broadcasted_iota(jnp.int32, (8, LANES), 0)
    sc = lax.broadcasted_iota(jnp.int32, (8, LANES), 1)
    pick = jnp.where(sr == sc, 1.0, 0.0).astype(BF16)
    routet_ref[...] = sum(_dot_nt(pick, part) for part in _split3(route))


def _merge(convg, og, sgd, x, p, alpha, tm):
    n, d = x.shape
    val_dim = og.shape[1]
    assert n % tm == 0
    row = lambda w: pl.BlockSpec((tm, w), lambda i: (i, 0))
    full = lambda shape: pl.BlockSpec(shape, lambda i: (0, 0))
    const = lambda shape: pl.BlockSpec(shape, lambda i: (0, 0), pipeline_mode=pl.Buffered(1))
    return pl.pallas_call(
        functools.partial(_merge_kernel, alpha),
        grid=(n // tm,),
        in_specs=[row(d), row(val_dim), row(d), row(d), const((val_dim, d)), const((d, d)),
                  full((1, d)), full((1, d)), const((d, LANES)), full((1, LANES))],
        out_specs=[row(d), row(d // 2), row(LANES), pl.BlockSpec((8, tm), lambda i: (0, i)), full((8, LANES))],
        out_shape=[jax.ShapeDtypeStruct((n, d), F32), jax.ShapeDtypeStruct((n, d // 2), jnp.uint32),
                   jax.ShapeDtypeStruct((n, LANES), F32),
                   jax.ShapeDtypeStruct((8, n), F32), jax.ShapeDtypeStruct((8, LANES), F32)],
        scratch_shapes=[pltpu.VMEM((1, LANES), F32)],
        compiler_params=pltpu.CompilerParams(
            dimension_semantics=("arbitrary",), vmem_limit_bytes=VMEM_LIMIT),
        name="merge_outproj",
    )(convg, og, sgd, x, p["w_o"], p["w_out"], p["ln1_g"], p["ln1_b"], p["w_router"], p["b_router"])


def _dest_kernel(starts_ref, rt_ref, dest_ref):
    rt = rt_ref[...]
    base = jnp.zeros_like(rt)
    for e in range(N_EXPERTS):
        base = jnp.where(rt == float(e), starts_ref[e].astype(F32), base)
    d = (base[0:2] + rt[2:4]).astype(jnp.int32)
    dest_ref[...] = jnp.concatenate([d, jnp.zeros((rt.shape[0] - 2, rt.shape[1]), jnp.int32)], axis=0)


def _dest_rows(route_t, starts, tn):
    rows, n = route_t.shape
    assert n % tn == 0
    grid_spec = pltpu.PrefetchScalarGridSpec(
        num_scalar_prefetch=1,
        grid=(n // tn,),
        in_specs=[pl.BlockSpec((rows, tn), lambda i, st: (0, i))],
        out_specs=pl.BlockSpec((rows, tn), lambda i, st: (0, i)),
    )
    out = pl.pallas_call(
        _dest_kernel,
        grid_spec=grid_spec,
        out_shape=jax.ShapeDtypeStruct((rows, n), jnp.int32),
        compiler_params=pltpu.CompilerParams(dimension_semantics=("arbitrary",)),
        name="moe_dest_rows",
    )(starts, route_t)
    return out[:2]


def _route_plan(route_t, cnt, te):
    n = route_t.shape[1]
    i32 = jnp.int32
    counts = cnt[0, ROUTE_COL0:ROUTE_COL0 + N_EXPERTS].astype(i32)
    ends = jnp.cumsum(counts)
    starts = ends - counts
    eids = jnp.arange(N_EXPERTS, dtype=i32)

    def lookup(table, idx):
        return jnp.sum(jnp.where(idx[None, :] == eids[:, None], table[:, None], 0), axis=0)

    dest = _dest_rows(route_t, starts, min(_DEST_TN, n))

    first_tile = starts // te
    last_tile = (ends - 1) // te
    items_e = jnp.where(counts > 0, last_tile - first_tile + 1, 0)
    item_end = jnp.cumsum(items_e)
    item_start = item_end - items_e
    total = item_end[-1]
    n_items = (2 * n) // te + N_EXPERTS - 1
    w = jnp.minimum(jnp.arange(n_items, dtype=i32), total - 1)
    item_e = jnp.sum((item_end[:, None] <= w[None, :]).astype(i32), axis=0)
    item_tile = lookup(first_tile, item_e) + w - lookup(item_start, item_e)
    lo = jnp.clip(lookup(starts, item_e) - item_tile * te, 0, te)
    hi = jnp.clip(lookup(ends, item_e) - item_tile * te, 0, te)
    return dest, (item_tile, item_e, lo, hi, total.reshape(1))


def _dispatch_kernel(d_ref, h_ref, xs_hbm, sem):
    tm = d_ref.shape[-1]

    def row_copy(r, dst):
        return pltpu.make_async_copy(h_ref.at[pl.ds(r, 1)], xs_hbm.at[pl.ds(dst, 1)], sem.at[0])

    def wait(r, carry):
        row_copy(r, 0).wait()
        row_copy(r, 0).wait()
        return carry

    for r in range(tm):
        row_copy(r, d_ref[0, 0, 0, r]).start(priority=0)
        row_copy(r, d_ref[1, 0, 0, r]).start(priority=1)
    lax.fori_loop(0, tm, wait, 0, unroll=8)


def _dispatch(h, dest, tm):
    n, d = h.shape
    assert n % tm == 0
    n_steps = n // tm
    return pl.pallas_call(
        _dispatch_kernel,
        grid=(n_steps,),
        in_specs=[pl.BlockSpec((2, 1, 1, tm), lambda i: (0, i, 0, 0), memory_space=pltpu.SMEM),
                  pl.BlockSpec((tm, d), lambda i: (i, 0))],
        out_specs=pl.BlockSpec(memory_space=pl.ANY),
        out_shape=jax.ShapeDtypeStruct((2 * n, d), h.dtype),
        scratch_shapes=[pltpu.SemaphoreType.DMA((1,))],
        compiler_params=pltpu.CompilerParams(
            dimension_semantics=("arbitrary",), vmem_limit_bytes=VMEM_LIMIT),
        name="moe_dispatch",
    )(dest.reshape(2, n_steps, 1, tm), h)


def _expert_kernel(tile_ref, exp_ref, lo_ref, hi_ref, tot_ref, xs_ref, wg_ref, wu_ref, wd_ref,
                   out_ref, wgb_ref, wub_ref, wdb_ref):
    w = pl.program_id(0)
    prev = jnp.maximum(w - 1, 0)
    live = w < tot_ref[0]
    new_expert = (w == 0) | (exp_ref[w] != exp_ref[prev])
    first_of_tile = (w == 0) | (tile_ref[w] != tile_ref[prev])

    @pl.when(live & new_expert)
    def _():
        wgb_ref[...] = wg_ref[0].astype(BF16)
        wub_ref[...] = wu_ref[0].astype(BF16)
        wdb_ref[...] = wd_ref[0].astype(BF16)

    @pl.when(live)
    def _():
        x_lo, x_hi = _unpack_pair(xs_ref[...])
        x = jnp.concatenate([x_lo.astype(BF16), x_hi.astype(BF16)], axis=1)
        hg = _dot(x, wgb_ref[...])
        hu = _dot(x, wub_ref[...])
        row = lax.broadcasted_iota(jnp.int32, (x.shape[0], 1), 0)
        mine = (row >= lo_ref[w]) & (row < hi_ref[w])
        act = jnp.where(mine, _silu(hg) * hu, 0.0).astype(BF16)
        part = _dot(act, wdb_ref[...])
        half_d = part.shape[1] // 2
        packed = _pack_pair(part[:, :half_d], part[:, half_d:])

        @pl.when(first_of_tile)
        def _():
            out_ref[...] = packed

        @pl.when(jnp.logical_not(first_of_tile))
        def _():
            out_ref[...] = jnp.where(mine, packed, out_ref[...])


def _experts(xs, items, p, te):
    rows, dp = xs.shape
    ne, d, f = p["w_gate"].shape
    assert d == 2 * dp
    item_tile, item_e, lo, hi, total = items
    n_items = item_tile.shape[0]
    tile_map = lambda w, t_, e_, lo_, hi_, n_: (t_[w], 0)
    exp_map = lambda w, t_, e_, lo_, hi_, n_: (e_[w], 0, 0)
    grid_spec = pltpu.PrefetchScalarGridSpec(
        num_scalar_prefetch=5,
        grid=(n_items,),
        in_specs=[
            pl.BlockSpec((te, dp), tile_map),
            pl.BlockSpec((1, d, f), exp_map),
            pl.BlockSpec((1, d, f), exp_map),
            pl.BlockSpec((1, f, d), exp_map),
        ],
        out_specs=pl.BlockSpec((te, dp), tile_map),
        scratch_shapes=[pltpu.VMEM((d, f), BF16), pltpu.VMEM((d, f), BF16), pltpu.VMEM((f, d), BF16)],
    )
    return pl.pallas_call(
        _expert_kernel,
        grid_spec=grid_spec,
        out_shape=jax.ShapeDtypeStruct((rows, dp), jnp.uint32),
        compiler_params=pltpu.CompilerParams(
            dimension_semantics=("arbitrary",), vmem_limit_bytes=VMEM_LIMIT),
        name="moe_experts",
    )(item_tile, item_e, lo, hi, total, xs, p["w_gate"], p["w_up"], p["w_down"])


def _combine_kernel(alpha, d_ref, dn_ref, h_ref, route_ref, rows_hbm, g_ref, b_ref, y_ref, o_ref, sem):
    i = pl.program_id(0)
    n_i = pl.num_programs(0)
    tm = h_ref.shape[0]
    slot = i % 2

    def row_copy(src_row, slot_, k, r):
        return pltpu.make_async_copy(rows_hbm.at[pl.ds(src_row, 1)], o_ref.at[slot_, k, pl.ds(r, 1)],
                                     sem.at[slot_])

    def start(dref, slot_):
        for r in range(tm):
            row_copy(dref[0, 0, 0, r], slot_, 0, r).start(priority=0)
            row_copy(dref[1, 0, 0, r], slot_, 1, r).start(priority=1)

    def wait(slot_):
        def body(r, carry):
            row_copy(0, slot_, 0, r).wait()
            row_copy(0, slot_, 1, r).wait()
            return carry
        lax.fori_loop(0, tm, body, 0, unroll=8)

    @pl.when(i == 0)
    def _():
        start(d_ref, 0)

    for nxt in (0, 1):
        @pl.when((i + 1 < n_i) & (slot == 1 - nxt))
        def _():
            start(dn_ref, nxt)

    wait(slot)
    route = route_ref[...]
    col = lax.broadcasted_iota(jnp.int32, route.shape, 1)
    w1 = jnp.sum(jnp.where(col == _R_W1, route, 0.0), axis=-1, keepdims=True)
    w2 = jnp.sum(jnp.where(col == _R_W2, route, 0.0), axis=-1, keepdims=True)
    a_lo, a_hi = _unpack_pair(o_ref[slot, 0])
    b_lo, b_hi = _unpack_pair(o_ref[slot, 1])
    moe = jnp.concatenate([w1 * a_lo + w2 * b_lo, w1 * a_hi + w2 * b_hi], axis=1)
    y_ref[...] = _layer_norm(alpha * h_ref[...] + moe, g_ref[...], b_ref[...])


def _combine(h, route, rows, dest, p, alpha, tm):
    n, d = h.shape
    assert n % tm == 0
    n_i = n // tm
    dest4 = dest.reshape(2, n_i, 1, tm)
    cur = pl.BlockSpec((2, 1, 1, tm), lambda i: (0, i, 0, 0), memory_space=pltpu.SMEM)
    nxt = pl.BlockSpec((2, 1, 1, tm), lambda i: (0, jnp.minimum(i + 1, n_i - 1), 0, 0),
                       memory_space=pltpu.SMEM)
    return pl.pallas_call(
        functools.partial(_combine_kernel, alpha),
        grid=(n_i,),
        in_specs=[cur, nxt,
                  pl.BlockSpec((tm, d), lambda i: (i, 0)),
                  pl.BlockSpec((tm, LANES), lambda i: (i, 0)),
                  pl.BlockSpec(memory_space=pl.ANY),
                  pl.BlockSpec((1, d), lambda i: (0, 0)),
                  pl.BlockSpec((1, d), lambda i: (0, 0))],
        out_specs=pl.BlockSpec((tm, d), lambda i: (i, 0)),
        out_shape=jax.ShapeDtypeStruct((n, d), F32),
        scratch_shapes=[pltpu.VMEM((2, 2, tm, d // 2), jnp.uint32), pltpu.SemaphoreType.DMA((2,))],
        compiler_params=pltpu.CompilerParams(
            dimension_semantics=("arbitrary",), vmem_limit_bytes=VMEM_LIMIT),
        name="moe_combine",
    )(dest4, dest4, h, route, rows, p["ln2_g"], p["ln2_b"])


def _pack_layer(w_in, b_in, w_dw, b_dw, lnc_g, lnc_b, w_conv_out, w_short, a_log, dt_bias, o_norm_g,
                w_o, w_out, ln1_g, ln1_b, w_rg, b_rg, w_re, b_re, w_gate, w_up, w_down, ln2_g, ln2_b):
    d = w_in.shape[0]
    c_conv = w_dw.shape[1]
    qkv_dim = w_short.shape[1]
    o_qkv = 2 * c_conv
    o_ba = o_qkv + qkv_dim
    o_tail = o_ba + 2 * N_HEADS

    def pad_cols(a, width):
        return jnp.pad(a, ((0, 0), (0, width - a.shape[1])))

    b2 = b_in[None, :].astype(F32)
    nh = N_HEADS
    zeros_h = jnp.zeros((nh,), F32)
    head_params = jnp.stack([jnp.concatenate([zeros_h, a_log.astype(F32)]),
                             jnp.concatenate([zeros_h, dt_bias.astype(F32)])])
    half = _TN // 2
    w_main = jnp.concatenate([w_in[:, :o_ba], w_in[:, o_tail:]], axis=1).astype(BF16)
    w_blk = w_main.reshape(d, -1, half).transpose(1, 0, 2)
    w_ba = w_in[:, o_ba:o_tail].astype(BF16)
    w_router = jnp.concatenate([w_rg, w_re], axis=1).astype(F32)
    b_router = jnp.concatenate([b_rg, b_re])[None, :].astype(F32)
    return dict(
        c_conv=c_conv, qkv_dim=qkv_dim,
        w_blk=w_blk, b_in=b2,
        w_ba=pad_cols(w_ba, LANES),
        b_ba=pad_cols(b2[:, o_ba:o_tail], LANES),
        w_bat=w_ba.T,
        b_bat=jnp.broadcast_to(b_in[o_ba:o_tail, None].astype(F32), (2 * nh, LANES)),
        b_tail=b2[:, o_tail:],
        p_row=pad_cols(head_params, LANES),
        p_col=pad_cols(head_params.T, LANES),
        w_dw=w_dw.astype(F32), b_dw=b_dw[None, :].astype(F32),
        lnc_g=lnc_g[None, :].astype(F32), lnc_b=lnc_b[None, :].astype(F32),
        w_conv_out=w_conv_out.astype(BF16),
        w_short=w_short.astype(F32),
        o_norm_g=o_norm_g[None, :].astype(F32),
        w_o=w_o.astype(BF16), w_out=w_out.astype(BF16),
        ln1_g=ln1_g[None, :].astype(F32), ln1_b=ln1_b[None, :].astype(F32),
        w_router=pad_cols(w_router, LANES), b_router=pad_cols(b_router, LANES),
        w_gate=w_gate.astype(F32), w_up=w_up.astype(F32), w_down=w_down.astype(F32),
        ln2_g=ln2_g[None, :].astype(F32), ln2_b=ln2_b[None, :].astype(F32),
    )


_MERGE_TM = 512
_COMBINE_TM = 256
_DEST_TN = 2048
_EXPERT_TE = 256
_EXPERT_TE_LARGE = 512
_DISPATCH_TM = 512


def _block(x, conv_buf, short_buf, s0, p, alpha, tm, conv_tt):
    b, t, d = x.shape
    n = b * t
    x2 = x.reshape(n, d).astype(F32)
    u, qkv, bg, bgt, gsil, sgc, sgd = _inproj(x2, p, tm)
    c_conv = u.shape[1]
    u3 = u.reshape(b, t, c_conv)
    qkv3 = qkv.reshape(b, t, -1)
    convg = _conv_branch(u3, conv_buf, sgc.reshape(b, t, d), p, conv_tt)
    og, s_new = _delta_branch(qkv3, short_buf, bg, bgt, gsil.reshape(b, t, -1), s0, p)
    h, hp, route, route_t, cnt = _merge(convg.reshape(n, d), og.reshape(n, -1), sgd, x2, p, alpha, min(_MERGE_TM, n))
    te = _EXPERT_TE_LARGE if 2 * n >= 2 * _EXPERT_TE_LARGE * N_EXPERTS else _EXPERT_TE
    dest, items = _route_plan(route_t, cnt, te)
    xs = _dispatch(hp, dest, min(_DISPATCH_TM, n))
    rows = _experts(xs, items, p, te)
    y = _combine(h, route, rows, dest, p, alpha, min(_COMBINE_TM, n))
    kc = conv_buf.shape[1]
    ks = short_buf.shape[1]
    assert t >= kc and t >= ks
    return (y.reshape(b, t, d).astype(x.dtype), u3[:, t - kc:].astype(x.dtype),
            qkv3[:, t - ks:].astype(x.dtype), s_new.astype(s0.dtype))


def kernel(x_prompt, x_sample, cache_conv, cache_short, state_delta, w_in, b_in, w_dw, b_dw, lnc_g, lnc_b, w_conv_out, w_short, a_log, dt_bias, o_norm_g, w_o, w_out, ln1_g, ln1_b, w_rg, b_rg, w_re, b_re, w_gate, w_up, w_down, ln2_g, ln2_b):
    weights = (w_in, b_in, w_dw, b_dw, lnc_g, lnc_b, w_conv_out, w_short, a_log, dt_bias, o_norm_g,
               w_o, w_out, ln1_g, ln1_b, w_rg, b_rg, w_re, b_re, w_gate, w_up, w_down, ln2_g, ln2_b)
    depth = w_in.shape[0]
    alpha = (2.0 * depth) ** 0.25
    yp, ys = x_prompt, x_sample
    bp = x_prompt.shape[0]
    outs = [[] for _ in range(6)]
    for l in range(depth):
        p = _pack_layer(*(wt[l] for wt in weights))
        zc = jnp.zeros((bp,) + cache_conv.shape[2:], x_prompt.dtype)
        zs = jnp.zeros((bp,) + cache_short.shape[2:], x_prompt.dtype)
        zd = jnp.zeros((bp,) + state_delta.shape[2:], state_delta.dtype)
        yp, c, s, dl = _block(yp, zc, zs, zd, p, alpha, 1024, 256)
        outs[0].append(c), outs[1].append(s), outs[2].append(dl)
        ys, c, s, dl = _block(ys, cache_conv[l], cache_short[l], state_delta[l], p, alpha, 1024, 64)
        outs[3].append(c), outs[4].append(s), outs[5].append(dl)
    return (yp, ys) + tuple(jnp.stack(o) for o in outs)
```

```python
import functools

import jax
import jax.numpy as jnp
from jax import lax
from jax.experimental import pallas as pl
from jax.experimental.pallas import tpu as pltpu

F32 = jnp.float32
BF16 = jnp.bfloat16

CHUNK = 64
N_HEADS = 8
HEAD_DIM = 128
N_GROUPS = 4
EXP_PER_GROUP = 8
N_EXPERTS = N_GROUPS * EXP_PER_GROUP
LN_EPS = 1e-5
NORM_EPS = 1e-6
LANES = 128
ROUTE_COL0 = N_GROUPS
VMEM_LIMIT = 56 * 1024 * 1024


def _dot(a, b):
    return jnp.dot(a, b, preferred_element_type=F32)


def _dot_nt(a, b):
    return lax.dot_general(a, b, (((1,), (1,)), ((), ())), preferred_element_type=F32)


def _dot_tn(a, b):
    return lax.dot_general(a, b, (((0,), (0,)), ((), ())), preferred_element_type=F32)


def _split3(x):
    hi = x.astype(BF16)
    r1 = x - hi.astype(F32)
    mid = r1.astype(BF16)
    lo = (r1 - mid.astype(F32)).astype(BF16)
    return hi, mid, lo


def _pack_pair(lo, hi):
    lo_bits = lax.bitcast_convert_type(lo.astype(BF16).astype(F32), jnp.uint32)
    hi_bits = lax.bitcast_convert_type(hi.astype(BF16).astype(F32), jnp.uint32)
    return (hi_bits & jnp.uint32(0xFFFF0000)) | (lo_bits >> 16)


def _unpack_pair(packed):
    lo = lax.bitcast_convert_type(packed << 16, F32)
    hi = lax.bitcast_convert_type(packed & jnp.uint32(0xFFFF0000), F32)
    return lo, hi


def _sigmoid(x):
    return 1.0 / (1.0 + jnp.exp(-x))


def _silu(x):
    return x * _sigmoid(x)


def _softplus(x):
    return jnp.maximum(x, 0.0) + jnp.log1p(jnp.exp(-jnp.abs(x)))


def _layer_norm(x, g, b):
    mu = jnp.mean(x, axis=-1, keepdims=True)
    xc = x - mu
    var = jnp.mean(xc * xc, axis=-1, keepdims=True)
    return xc * lax.rsqrt(var + LN_EPS) * g + b


def _clamp(v, lo, hi):
    return jnp.minimum(jnp.maximum(v, lo), hi)


_TN = 512
_J_GLU, _J_QKV, _J_BA, _J_GO, _J_GC, _J_GD, _J_END = 0, 4, 10, 11, 13, 17, 21


def _wide_dot(x, w_ref):
    return jnp.concatenate([_dot(x, w_ref[b]) for b in range(w_ref.shape[0])], axis=1)


def _inproj_kernel(x_ref, wga_ref, wgb_ref, wqkv_ref, wba_ref, wbat_ref, wtail_ref,
                   bga_ref, bgb_ref, bqkv_ref, bba_ref, bbat_ref, btail_ref, prow_ref, pcol_ref,
                   u_ref, qkv_ref, bg_ref, bgt_ref, gsil_ref, sgc_ref, sgd_ref, xb_ref):
    j = pl.program_id(1)

    @pl.when(j == 0)
    def _():
        xb_ref[...] = x_ref[...].astype(BF16)

    @pl.when(j < _J_QKV)
    def _():
        xb = xb_ref[...]
        value = _dot(xb, wga_ref[0]) + bga_ref[...]
        gate = _dot(xb, wgb_ref[0]) + bgb_ref[...]
        u_ref[...] = value * _sigmoid(gate)

    @pl.when((j >= _J_QKV) & (j < _J_BA))
    def _():
        qkv_ref[...] = _wide_dot(xb_ref[...], wqkv_ref) + bqkv_ref[...]

    @pl.when(j == _J_BA)
    def _():
        xb = xb_ref[...]
        z = _dot(xb, wba_ref[...]) + bba_ref[...]
        col = lax.broadcasted_iota(jnp.int32, z.shape, 1)
        g = -jnp.exp(prow_ref[0:1, :]) * _softplus(z + prow_ref[1:2, :])
        bg_ref[...] = jnp.where(col < N_HEADS, _sigmoid(z), g)
        zt = _dot_nt(wbat_ref[...], xb) + bbat_ref[:, 0:1]
        row = lax.broadcasted_iota(jnp.int32, zt.shape, 0)
        gt = -jnp.exp(pcol_ref[:, 0:1]) * _softplus(zt + pcol_ref[:, 1:2])
        bgt_ref[...] = jnp.where(row < N_HEADS, _sigmoid(zt), gt)

    @pl.when((j >= _J_GO) & (j < _J_GC))
    def _():
        z = _wide_dot(xb_ref[...], wtail_ref) + btail_ref[...]
        gsil_ref[...] = _silu(z).astype(BF16)

    @pl.when((j >= _J_GC) & (j < _J_GD))
    def _():
        z = _wide_dot(xb_ref[...], wtail_ref) + btail_ref[...]
        sgc_ref[...] = _sigmoid(z).astype(BF16)

    @pl.when(j >= _J_GD)
    def _():
        z = _wide_dot(xb_ref[...], wtail_ref) + btail_ref[...]
        sgd_ref[...] = _sigmoid(z).astype(BF16)


def _inproj(x, pk, tm):
    n, d = x.shape
    c_conv, qkv_dim = pk["c_conv"], pk["qkv_dim"]
    val_dim = N_HEADS * HEAD_DIM
    half = _TN // 2
    n_glu = _J_QKV - _J_GLU
    assert n % tm == 0 and c_conv == n_glu * half and (2 * c_conv) % _TN == 0
    qkv_blk0 = 2 * c_conv // _TN
    tail_blk0 = (2 * c_conv + qkv_dim) // _TN
    assert (2 * c_conv + qkv_dim) % _TN == 0

    def cm(lo, hi):
        return lambda i, j: (0, _clamp(j - lo, 0, hi - lo - 1))

    def om(lo, hi):
        return lambda i, j: (i, _clamp(j - lo, 0, hi - lo - 1))

    in_specs = [
        pl.BlockSpec((tm, d), lambda i, j: (i, 0)),
        pl.BlockSpec((1, d, half), lambda i, j: (_clamp(j, 0, n_glu - 1), 0, 0)),
        pl.BlockSpec((1, d, half), lambda i, j: (n_glu + _clamp(j, 0, n_glu - 1), 0, 0)),
        pl.BlockSpec((2, d, half), lambda i, j: (qkv_blk0 + _clamp(j - _J_QKV, 0, _J_BA - _J_QKV - 1), 0, 0)),
        pl.BlockSpec((d, LANES), lambda i, j: (0, 0)),
        pl.BlockSpec((2 * N_HEADS, d), lambda i, j: (0, 0)),
        pl.BlockSpec((2, d, half), lambda i, j: (tail_blk0 + _clamp(j - _J_GO, 0, _J_END - _J_GO - 1), 0, 0)),
        pl.BlockSpec((1, half), lambda i, j: (0, _clamp(j, 0, n_glu - 1))),
        pl.BlockSpec((1, half), lambda i, j: (0, n_glu + _clamp(j, 0, n_glu - 1))),
        pl.BlockSpec((1, _TN), lambda i, j: (0, qkv_blk0 + _clamp(j - _J_QKV, 0, _J_BA - _J_QKV - 1))),
        pl.BlockSpec((1, LANES), lambda i, j: (0, 0)),
        pl.BlockSpec((2 * N_HEADS, LANES), lambda i, j: (0, 0)),
        pl.BlockSpec((1, _TN), cm(_J_GO, _J_END)),
        pl.BlockSpec((2, LANES), lambda i, j: (0, 0)),
        pl.BlockSpec((2 * N_HEADS, LANES), lambda i, j: (0, 0)),
    ]
    out_shape = [
        jax.ShapeDtypeStruct((n, c_conv), F32),
        jax.ShapeDtypeStruct((n, qkv_dim), F32),
        jax.ShapeDtypeStruct((n, LANES), F32),
        jax.ShapeDtypeStruct((2 * N_HEADS, n), F32),
        jax.ShapeDtypeStruct((n, val_dim), BF16),
        jax.ShapeDtypeStruct((n, d), BF16),
        jax.ShapeDtypeStruct((n, d), BF16),
    ]
    out_specs = [
        pl.BlockSpec((tm, _TN // 2), om(_J_GLU, _J_QKV)),
        pl.BlockSpec((tm, _TN), om(_J_QKV, _J_BA)),
        pl.BlockSpec((tm, LANES), lambda i, j: (i, 0)),
        pl.BlockSpec((2 * N_HEADS, tm), lambda i, j: (0, i)),
        pl.BlockSpec((tm, _TN), om(_J_GO, _J_GC)),
        pl.BlockSpec((tm, _TN), om(_J_GC, _J_GD)),
        pl.BlockSpec((tm, _TN), om(_J_GD, _J_END)),
    ]
    return pl.pallas_call(
        _inproj_kernel,
        grid=(n // tm, _J_END),
        in_specs=in_specs,
        out_specs=out_specs,
        out_shape=out_shape,
        scratch_shapes=[pltpu.VMEM((tm, d), BF16)],
        compiler_params=pltpu.CompilerParams(
            dimension_semantics=("arbitrary", "arbitrary"), vmem_limit_bytes=VMEM_LIMIT),
        name="inproj",
    )(x, pk["w_blk"], pk["w_blk"], pk["w_blk"], pk["w_ba"], pk["w_bat"], pk["w_blk"],
      pk["b_in"], pk["b_in"], pk["b_in"], pk["b_ba"], pk["b_bat"], pk["b_tail"], pk["p_row"], pk["p_col"])


_HALO = 32
_CONV_RB = 64
_CONV_FB = 8
_CONV_SEQS = 4


def _conv_kernel(independent, u_ref, cache_ref, wdw_ref, bdw_ref, lng_ref, lnb_ref, wco_ref, sgc_ref,
                 out_ref, xt_ref, yt_ref, cn_ref):
    t = pl.program_id(1)
    nseq, tt, c_conv = u_ref.shape
    width = wdw_ref.shape[0]
    nfold = c_conv // LANES
    first = _HALO - (width - 1)

    def load_history(q):
        for s in range(nfold):
            xt_ref[pl.ds(s, _HALO, stride=nfold), :] = cache_ref[q, :, s * LANES:(s + 1) * LANES]

    def frames(ib, carry):
        f0 = ib * _CONV_FB
        acc = [None] * _CONV_FB
        for k in range(width):
            wk = wdw_ref[k]
            for j in range(_CONV_FB):
                row = pl.multiple_of((first + f0 + j + k) * nfold, nfold)
                term = wk * xt_ref[pl.ds(row, nfold), :]
                acc[j] = term if acc[j] is None else acc[j] + term
        for j in range(_CONV_FB):
            yt_ref[pl.ds(pl.multiple_of((f0 + j) * nfold, nfold), nfold), :] = acc[j]
        return carry

    if not independent:
        pl.when(t == 0)(lambda: load_history(0))

    for q in range(nseq):
        if independent:
            load_history(q)
        for s in range(nfold):
            xt_ref[pl.ds(_HALO * nfold + s, tt, stride=nfold), :] = u_ref[q, :, s * LANES:(s + 1) * LANES]
        lax.fori_loop(0, tt // _CONV_FB, frames, 0)
        for rb in range(tt // _CONV_RB):
            r0 = rb * _CONV_RB
            y = jnp.concatenate(
                [yt_ref[pl.ds(r0 * nfold + s, _CONV_RB, stride=nfold), :] for s in range(nfold)], axis=1)
            y = _layer_norm(y + bdw_ref[...], lng_ref[...], lnb_ref[...])
            cn_ref[q * tt + r0:q * tt + r0 + _CONV_RB, :] = _silu(y).astype(BF16)

    co = _dot(cn_ref[...], wco_ref[...])
    for q in range(nseq):
        out_ref[q] = (co[q * tt:(q + 1) * tt] * sgc_ref[q].astype(F32)).astype(BF16)
    if not independent:
        xt_ref[0:_HALO * nfold, :] = xt_ref[tt * nfold:(tt + _HALO) * nfold, :]


def _conv_branch(u, cache, sgc, p, tt):
    b, t, c_conv = u.shape
    d = sgc.shape[-1]
    width = p["w_dw"].shape[0]
    assert t % tt == 0 and tt % _CONV_RB == 0 and tt >= _HALO and width - 1 <= _HALO
    assert c_conv % (8 * LANES) == 0
    nfold = c_conv // LANES
    cache_p = jnp.pad(cache.astype(F32), ((0, 0), (_HALO - (width - 1), 0), (0, 0)))
    independent = t == tt
    nseq = _CONV_SEQS if independent and b % _CONV_SEQS == 0 else 1
    full2 = lambda shape: pl.BlockSpec(shape, lambda i, j: (0, 0))
    return pl.pallas_call(
        functools.partial(_conv_kernel, independent),
        grid=(b // nseq, t // tt),
        in_specs=[
            pl.BlockSpec((nseq, tt, c_conv), lambda i, j: (i, j, 0)),
            pl.BlockSpec((nseq, _HALO, c_conv), lambda i, j: (i, 0, 0)),
            pl.BlockSpec((width, nfold, LANES), lambda i, j: (0, 0, 0)),
            full2((1, c_conv)), full2((1, c_conv)), full2((1, c_conv)),
            full2((c_conv, d)),
            pl.BlockSpec((nseq, tt, d), lambda i, j: (i, j, 0)),
        ],
        out_specs=pl.BlockSpec((nseq, tt, d), lambda i, j: (i, j, 0)),
        out_shape=jax.ShapeDtypeStruct((b, t, d), BF16),
        scratch_shapes=[pltpu.VMEM(((tt + _HALO) * nfold, LANES), F32),
                        pltpu.VMEM((tt * nfold, LANES), F32),
                        pltpu.VMEM((nseq * tt, c_conv), BF16)],
        compiler_params=pltpu.CompilerParams(
            dimension_semantics=("arbitrary", "arbitrary"), vmem_limit_bytes=VMEM_LIMIT),
        name="conv_branch",
    )(u, cache_p, p["w_dw"].reshape(width, nfold, LANES), p["b_dw"], p["lnc_g"], p["lnc_b"],
      p["w_conv_out"], sgc)


_SHORT_PAD = 8
_DELTA_CPS = 8


def _delta_kernel(n_steps, cps, independent, qkv_ref, cache_ref, wsh_ref, bg_ref, bgt_ref, gsil_ref, ong_ref,
                  s0_ref, o_ref, sfin_ref, xb_ref, s_ref):
    c = pl.program_id(1)
    ck = CHUNK
    key_dim = N_HEADS * HEAD_DIM
    sw = wsh_ref.shape[0]
    unit_rows = _SHORT_PAD + ck
    ub = lambda cc: cc * unit_rows

    if independent:
        for cc in range(cps):
            xb_ref[ub(cc):ub(cc) + _SHORT_PAD, :] = cache_ref[cc]
            xb_ref[ub(cc) + _SHORT_PAD:ub(cc) + unit_rows, :] = qkv_ref[cc]
    else:
        @pl.when(c == 0)
        def _():
            s_ref[...] = s0_ref[0]
            xb_ref[0:_SHORT_PAD, :] = cache_ref[0]

        for cc in range(cps):
            xb_ref[ub(cc) + _SHORT_PAD:ub(cc) + unit_rows, :] = qkv_ref[0, cc * ck:(cc + 1) * ck, :]
            if cc > 0:
                xb_ref[ub(cc):ub(cc) + _SHORT_PAD, :] = qkv_ref[0, cc * ck - _SHORT_PAD:cc * ck, :]

    def conv_cols(cc, lo):
        lanes = slice(lo, lo + HEAD_DIM)
        base = ub(cc) + _SHORT_PAD
        acc = wsh_ref[sw - 1:sw, lanes] * xb_ref[base:base + ck, lanes]
        for k in range(sw - 1):
            r = base - (sw - 1) + k
            acc = acc + wsh_ref[k:k + 1, lanes] * xb_ref[r:r + ck, lanes]
        return _silu(acc)

    rowp = lax.broadcasted_iota(jnp.int32, (ck, 2 * ck), 0)
    lanep = lax.broadcasted_iota(jnp.int32, (ck, 2 * ck), 1)
    odd = lanep >= ck
    lcol = jnp.where(odd, lanep - ck, lanep)
    incl = rowp >= lcol
    strict = rowp > lcol
    eye = jnp.where(rowp == lcol, 1.0, 0.0).astype(F32)
    bd8 = (rowp // 8) == (lcol // 8)
    lvl = [((rowp // (2 * s)) == (lcol // (2 * s))) & ((rowp // s) != (lcol // s)) for s in (8, 16, 32)]
    ri = lax.broadcasted_iota(jnp.int32, (ck, ck), 0)
    ci = lax.broadcasted_iota(jnp.int32, (ck, ck), 1)
    tri_l = jnp.where(ri >= ci, 1.0, 0.0).astype(BF16)
    r2 = lax.broadcasted_iota(jnp.int32, (2 * ck, 2 * ck), 0)
    c2 = lax.broadcasted_iota(jnp.int32, (2 * ck, 2 * ck), 1)
    tri_u2 = jnp.where(((r2 >= ck) == (c2 >= ck)) & (r2 <= c2), 1.0, 0.0).astype(BF16)

    bf = lambda m: m.astype(BF16)

    def block_diag(pm):
        return bf(jnp.concatenate([jnp.where(odd, 0.0, pm), jnp.where(odd, pm, 0.0)], axis=0))

    def block_rows(top, bottom):
        z = jnp.zeros_like(top)
        return jnp.concatenate([jnp.concatenate([top, z], axis=1), jnp.concatenate([z, bottom], axis=1)], axis=0)

    bg = [bg_ref[cc] if independent else bg_ref[0, cc * ck:(cc + 1) * ck, :] for cc in range(cps)]
    gc_cols = [sum(_dot(tri_l, part) for part in _split3(bg[cc])) for cc in range(cps)]
    gc_rows = [sum(_dot(part, tri_u2) for part in _split3(bgt_ref[cc])) for cc in range(cps)]

    nh2 = N_HEADS // 2
    heads = range(cps * N_HEADS)
    pairs = range(cps * nh2)
    hcc = lambda i: (i // N_HEADS, i % N_HEADS)
    ev = lambda j: (j // nh2) * N_HEADS + 2 * (j % nh2)
    q = [conv_cols(hcc(i)[0], hcc(i)[1] * HEAD_DIM) for i in heads]
    k = [conv_cols(hcc(i)[0], key_dim + hcc(i)[1] * HEAD_DIM) for i in heads]
    v = [conv_cols(hcc(i)[0], 2 * key_dim + hcc(i)[1] * HEAD_DIM) for i in heads]
    q = [x * lax.rsqrt(jnp.sum(x * x, axis=-1, keepdims=True) + NORM_EPS) * (HEAD_DIM ** -0.5) for x in q]
    k = [x * lax.rsqrt(jnp.sum(x * x, axis=-1, keepdims=True) + NORM_EPS) for x in k]
    beta = [bg[hcc(i)[0]][:, hcc(i)[1]:hcc(i)[1] + 1] for i in heads]
    gcc = [gc_cols[hcc(i)[0]][:, N_HEADS + hcc(i)[1]:N_HEADS + hcc(i)[1] + 1] for i in heads]
    kb = [k[i] * beta[i] for i in heads]
    kbf = [bf(x) for x in k]
    kk = [block_rows(kbf[ev(j)], kbf[ev(j) + 1]) for j in pairs]
    gcc2 = [jnp.where(odd, gcc[ev(j) + 1], gcc[ev(j)]) for j in pairs]
    gcr2 = [gc_rows[j // nh2][nh2 + j % nh2:nh2 + j % nh2 + 1, :] for j in pairs]
    decay = [jnp.exp(jnp.where(incl, gcc2[j] - gcr2[j], -jnp.inf)) for j in pairs]
    a = [jnp.where(strict,
                   _dot_nt(jnp.concatenate([bf(kb[ev(j)]), bf(kb[ev(j) + 1])], axis=1), kk[j]) * decay[j], 0.0)
         for j in pairs]
    qk = [_dot_nt(jnp.concatenate([bf(q[ev(j)]), bf(q[ev(j) + 1])], axis=1), kk[j]) * decay[j] for j in pairs]

    ad = [jnp.where(bd8, x, 0.0) for x in a]
    adb = [bf(x) for x in ad]
    a2 = [_dot(adb[j], block_diag(ad[j])) for j in pairs]
    a2d = [block_diag(x) for x in a2]
    a3 = [_dot(adb[j], a2d[j]) for j in pairs]
    a4 = [_dot(bf(a2[j]), a2d[j]) for j in pairs]
    t = [eye - ad[j] + a2[j] - a3[j] for j in pairs]
    t = [t[j] + _dot(bf(t[j]), block_diag(a4[j])) for j in pairs]
    for m in lvl:
        x = [_dot(bf(jnp.where(m, a[j], 0.0)), block_diag(t[j])) for j in pairs]
        t = [t[j] - _dot(bf(t[j]), block_diag(x[j])) for j in pairs]

    egc = [jnp.exp(x) for x in gcc]
    rhs = [bf(jnp.concatenate([v[i] * beta[i], kb[i] * egc[i]], axis=1)) for i in heads]
    sol2 = [_dot(bf(t[j]), block_rows(rhs[ev(j)], rhs[ev(j) + 1])) for j in pairs]
    sol = [sol2[(i // N_HEADS) * nh2 + (i % N_HEADS) // 2][:, (i % 2) * 2 * HEAD_DIM:(i % 2 + 1) * 2 * HEAD_DIM]
           for i in heads]
    g_last = [x[ck - 1:ck, :] for x in gcc]
    k_dec = [bf(k[i] * jnp.exp(g_last[i] - gcc[i])) for i in heads]
    wq_lhs = [bf(jnp.concatenate([sol[i][:, HEAD_DIM:], q[i] * egc[i]], axis=0)) for i in heads]
    qkb = [bf(x) for x in qk]

    def o_rows(ref, cc, lanes):
        return ref.at[cc, :, lanes] if independent else ref.at[0, cc * ck:(cc + 1) * ck, lanes]

    groups = [list(range(cps))] if independent else [[cc] for cc in range(cps)]
    carried = None if independent else [s_ref[h] for h in range(N_HEADS)]
    for group in groups:
        units = [(cc, h) for cc in group for h in range(N_HEADS)]
        s_in = [s0_ref[cc, h] if independent else carried[h] for cc, h in units]
        sb = [bf(x) for x in s_in]
        wq = [_dot(wq_lhs[cc * N_HEADS + h], sb[i]) for i, (cc, h) in enumerate(units)]
        vb = [bf(sol[cc * N_HEADS + h][:, :HEAD_DIM] - wq[i][:ck]) for i, (cc, h) in enumerate(units)]
        o2 = [_dot(qkb[cc * nh2 + p], block_rows(vb[g * N_HEADS + 2 * p], vb[g * N_HEADS + 2 * p + 1]))
              for g, cc in enumerate(group) for p in range(nh2)]
        s_out = [s_in[i] * jnp.exp(g_last[cc * N_HEADS + h]) + _dot_tn(k_dec[cc * N_HEADS + h], vb[i])
                 for i, (cc, h) in enumerate(units)]
        for i, (cc, h) in enumerate(units):
            g = i // N_HEADS
            o = wq[i][ck:] + o2[g * nh2 + h // 2][:, (h % 2) * HEAD_DIM:(h % 2 + 1) * HEAD_DIM]
            on = o * lax.rsqrt(jnp.mean(o * o, axis=-1, keepdims=True) + NORM_EPS) * ong_ref[...]
            lanes = slice(h * HEAD_DIM, (h + 1) * HEAD_DIM)
            o_rows(o_ref, cc, lanes)[...] = (on * o_rows(gsil_ref, cc, lanes)[...].astype(F32)).astype(BF16)
            if independent:
                sfin_ref[cc, h] = s_out[i]
        if not independent:
            carried = s_out

    if not independent:
        for h in range(N_HEADS):
            s_ref[h] = carried[h]
        xb_ref[0:_SHORT_PAD, :] = qkv_ref[0, cps * ck - _SHORT_PAD:cps * ck, :]

        @pl.when(c == n_steps - 1)
        def _():
            sfin_ref[0] = s_ref[...]


def _delta_branch(qkv, cache, bg, bgt, gsil, s0, p):
    b, t, qkv_dim = qkv.shape
    assert t % CHUNK == 0
    nc = t // CHUNK
    sw = p["w_short"].shape[0]
    cache_p = jnp.pad(cache.astype(F32), ((0, 0), (_SHORT_PAD - (sw - 1), 0), (0, 0)))
    val_dim = N_HEADS * HEAD_DIM
    bgt3 = bgt.reshape(2 * N_HEADS, b * nc, CHUNK).transpose(1, 0, 2).reshape(b * nc, N_HEADS, 2 * CHUNK)
    independent = nc == 1
    if independent:
        cps = _DELTA_CPS if b % _DELTA_CPS == 0 else 1
        grid = (b // cps, 1)
        seq = lambda i, j: (i, 0, 0)
        blk = lambda width: pl.BlockSpec((cps, CHUNK, width), seq)
        cache_spec = pl.BlockSpec((cps, _SHORT_PAD, qkv_dim), seq)
        bgt_spec = pl.BlockSpec((cps, N_HEADS, 2 * CHUNK), seq)
        state_spec = pl.BlockSpec((cps, N_HEADS, HEAD_DIM, HEAD_DIM), lambda i, j: (i, 0, 0, 0))
        n_steps = 1
    else:
        cps = _DELTA_CPS if nc % _DELTA_CPS == 0 else 1
        n_steps = nc // cps
        grid = (b, n_steps)
        blk = lambda width: pl.BlockSpec((1, cps * CHUNK, width), lambda i, j: (i, j, 0))
        cache_spec = pl.BlockSpec((1, _SHORT_PAD, qkv_dim), lambda i, j: (i, 0, 0))
        bgt_spec = pl.BlockSpec((cps, N_HEADS, 2 * CHUNK), lambda i, j: (i * n_steps + j, 0, 0))
        state_spec = pl.BlockSpec((1, N_HEADS, HEAD_DIM, HEAD_DIM), lambda i, j: (i, 0, 0, 0))
    return pl.pallas_call(
        functools.partial(_delta_kernel, n_steps, cps, independent),
        grid=grid,
        in_specs=[
            blk(qkv_dim),
            cache_spec,
            pl.BlockSpec((sw, qkv_dim), lambda i, j: (0, 0)),
            blk(LANES),
            bgt_spec,
            blk(val_dim),
            pl.BlockSpec((1, HEAD_DIM), lambda i, j: (0, 0)),
            state_spec,
        ],
        out_specs=[blk(val_dim), state_spec],
        out_shape=[
            jax.ShapeDtypeStruct((b, t, val_dim), BF16),
            jax.ShapeDtypeStruct((b, N_HEADS, HEAD_DIM, HEAD_DIM), F32),
        ],
        scratch_shapes=[pltpu.VMEM((cps * (_SHORT_PAD + CHUNK), qkv_dim), F32),
                        pltpu.VMEM((N_HEADS, HEAD_DIM, HEAD_DIM), F32)],
        compiler_params=pltpu.CompilerParams(
            dimension_semantics=("arbitrary", "arbitrary"), vmem_limit_bytes=VMEM_LIMIT),
        name="delta_rule",
    )(qkv, cache_p, p["w_short"], bg.reshape(b, t, LANES), bgt3, gsil, p["o_norm_g"], s0.astype(F32))


_R_E1, _R_E2, _R_RANK1, _R_RANK2, _R_W1, _R_W2 = range(6)


def _merge_kernel(alpha, convg_ref, og_ref, sgd_ref, x_ref, wo_ref, wout_ref, g_ref, b_ref,
                  wr_ref, br_ref, h_ref, hp_ref, route_ref, routet_ref, cnt_ref, carry_ref):
    @pl.when(pl.program_id(0) == 0)
    def _():
        carry_ref[...] = jnp.zeros_like(carry_ref)

    d_out = _dot(og_ref[...], wo_ref[...])
    merged = convg_ref[...].astype(F32) + d_out * sgd_ref[...].astype(F32)
    mix = _dot(merged.astype(BF16), wout_ref[...])
    h = _layer_norm(alpha * x_ref[...] + mix, g_ref[...], b_ref[...])
    h_ref[...] = h
    half_d = h.shape[1] // 2
    hp_ref[...] = _pack_pair(h[:, :half_d], h[:, half_d:])

    h_hi, h_mid, _ = _split3(h)
    w_hi, w_mid, _ = _split3(wr_ref[...])
    logits = _dot(h_hi, w_hi) + _dot(h_mid, w_hi) + _dot(h_hi, w_mid) + br_ref[...]
    tm = logits.shape[0]
    col = lax.broadcasted_iota(jnp.int32, logits.shape, 1).astype(F32)
    big = float(LANES)
    is_g = col < N_GROUPS
    mg = jnp.max(jnp.where(is_g, logits, -jnp.inf), axis=-1, keepdims=True)
    sg = jnp.sum(jnp.where(is_g, jnp.exp(jnp.where(is_g, logits, mg) - mg), 0.0), axis=-1, keepdims=True)
    pg_top = 1.0 / sg
    gidx = jnp.min(jnp.where(is_g & (logits == mg), col, big), axis=-1, keepdims=True)
    lo = ROUTE_COL0 + EXP_PER_GROUP * gidx
    sel = (col >= lo) & (col < lo + EXP_PER_GROUP)
    le = jnp.where(sel, logits, -jnp.inf)
    m1 = jnp.max(le, axis=-1, keepdims=True)
    i1 = jnp.min(jnp.where(le == m1, col, big), axis=-1, keepdims=True)
    le2 = jnp.where(col == i1, -jnp.inf, le)
    m2 = jnp.max(le2, axis=-1, keepdims=True)
    i2 = jnp.min(jnp.where(le2 == m2, col, big), axis=-1, keepdims=True)
    e2 = jnp.exp(m2 - m1)
    den = 1.0 + e2
    w1 = pg_top / den
    w2 = pg_top * e2 / den

    hit1 = col == i1
    hit2 = col == i2
    member = jnp.where(hit1 | hit2, 1.0, 0.0)
    ri = lax.broadcasted_iota(jnp.int32, (tm, tm), 0)
    ci = lax.broadcasted_iota(jnp.int32, (tm, tm), 1)
    earlier = jnp.where(ri > ci, 1.0, 0.0).astype(BF16)
    before = _dot(earlier, member.astype(BF16)) + carry_ref[...]
    rank1 = jnp.sum(jnp.where(hit1, before, 0.0), axis=-1, keepdims=True)
    rank2 = jnp.sum(jnp.where(hit2, before, 0.0), axis=-1, keepdims=True)
    carry_ref[...] += jnp.sum(member, axis=0, keepdims=True)
    cnt_ref[...] = jnp.broadcast_to(carry_ref[...], cnt_ref.shape)

    fields = (i1 - ROUTE_COL0, i2 - ROUTE_COL0, rank1, rank2, w1, w2)
    route = jnp.zeros_like(logits)
    for c, val in enumerate(fields):
        route = jnp.where(col == float(c), val, route)
    route_ref[...] = route
    sr = lax.broadcasted_iota(jnp.int32, (8, LANES), 0)
    sc = lax.broadcasted_iota(jnp.int32, (8, LANES), 1)
    pick = jnp.where(sr == sc, 1.0, 0.0).astype(BF16)
    routet_ref[...] = sum(_dot_nt(pick, part) for part in _split3(route))


def _merge(convg, og, sgd, x, p, alpha, tm):
    n, d = x.shape
    val_dim = og.shape[1]
    assert n % tm == 0
    row = lambda w: pl.BlockSpec((tm, w), lambda i: (i, 0))
    full = lambda shape: pl.BlockSpec(shape, lambda i: (0, 0))
    const = lambda shape: pl.BlockSpec(shape, lambda i: (0, 0), pipeline_mode=pl.Buffered(1))
    return pl.pallas_call(
        functools.partial(_merge_kernel, alpha),
        grid=(n // tm,),
        in_specs=[row(d), row(val_dim), row(d), row(d), const((val_dim, d)), const((d, d)),
                  full((1, d)), full((1, d)), const((d, LANES)), full((1, LANES))],
        out_specs=[row(d), row(d // 2), row(LANES), pl.BlockSpec((8, tm), lambda i: (0, i)), full((8, LANES))],
        out_shape=[jax.ShapeDtypeStruct((n, d), F32), jax.ShapeDtypeStruct((n, d // 2), jnp.uint32),
                   jax.ShapeDtypeStruct((n, LANES), F32),
                   jax.ShapeDtypeStruct((8, n), F32), jax.ShapeDtypeStruct((8, LANES), F32)],
        scratch_shapes=[pltpu.VMEM((1, LANES), F32)],
        compiler_params=pltpu.CompilerParams(
            dimension_semantics=("arbitrary",), vmem_limit_bytes=VMEM_LIMIT),
        name="merge_outproj",
    )(convg, og, sgd, x, p["w_o"], p["w_out"], p["ln1_g"], p["ln1_b"], p["w_router"], p["b_router"])


def _dest_kernel(starts_ref, rt_ref, dest_ref):
    rt = rt_ref[...]
    base = jnp.zeros_like(rt)
    for e in range(N_EXPERTS):
        base = jnp.where(rt == float(e), starts_ref[e].astype(F32), base)
    d = (base[0:2] + rt[2:4]).astype(jnp.int32)
    dest_ref[...] = jnp.concatenate([d, jnp.zeros((rt.shape[0] - 2, rt.shape[1]), jnp.int32)], axis=0)


def _dest_rows(route_t, starts, tn):
    rows, n = route_t.shape
    assert n % tn == 0
    grid_spec = pltpu.PrefetchScalarGridSpec(
        num_scalar_prefetch=1,
        grid=(n // tn,),
        in_specs=[pl.BlockSpec((rows, tn), lambda i, st: (0, i))],
        out_specs=pl.BlockSpec((rows, tn), lambda i, st: (0, i)),
    )
    out = pl.pallas_call(
        _dest_kernel,
        grid_spec=grid_spec,
        out_shape=jax.ShapeDtypeStruct((rows, n), jnp.int32),
        compiler_params=pltpu.CompilerParams(dimension_semantics=("arbitrary",)),
        name="moe_dest_rows",
    )(starts, route_t)
    return out[:2]


def _route_plan(route_t, cnt, te):
    n = route_t.shape[1]
    i32 = jnp.int32
    counts = cnt[0, ROUTE_COL0:ROUTE_COL0 + N_EXPERTS].astype(i32)
    ends = jnp.cumsum(counts)
    starts = ends - counts
    eids = jnp.arange(N_EXPERTS, dtype=i32)

    def lookup(table, idx):
        return jnp.sum(jnp.where(idx[None, :] == eids[:, None], table[:, None], 0), axis=0)

    dest = _dest_rows(route_t, starts, min(_DEST_TN, n))

    first_tile = starts // te
    last_tile = (ends - 1) // te
    items_e = jnp.where(counts > 0, last_tile - first_tile + 1, 0)
    item_end = jnp.cumsum(items_e)
    item_start = item_end - items_e
    total = item_end[-1]
    n_items = (2 * n) // te + N_EXPERTS - 1
    w = jnp.minimum(jnp.arange(n_items, dtype=i32), total - 1)
    item_e = jnp.sum((item_end[:, None] <= w[None, :]).astype(i32), axis=0)
    item_tile = lookup(first_tile, item_e) + w - lookup(item_start, item_e)
    lo = jnp.clip(lookup(starts, item_e) - item_tile * te, 0, te)
    hi = jnp.clip(lookup(ends, item_e) - item_tile * te, 0, te)
    return dest, (item_tile, item_e, lo, hi, total.reshape(1))


def _dispatch_kernel(d_ref, h_ref, xs_hbm, sem):
    tm = d_ref.shape[-1]

    def row_copy(r, dst):
        return pltpu.make_async_copy(h_ref.at[pl.ds(r, 1)], xs_hbm.at[pl.ds(dst, 1)], sem.at[0])

    def wait(r, carry):
        row_copy(r, 0).wait()
        row_copy(r, 0).wait()
        return carry

    for r in range(tm):
        row_copy(r, d_ref[0, 0, 0, r]).start(priority=0)
        row_copy(r, d_ref[1, 0, 0, r]).start(priority=1)
    lax.fori_loop(0, tm, wait, 0, unroll=8)


def _dispatch(h, dest, tm):
    n, d = h.shape
    assert n % tm == 0
    n_steps = n // tm
    return pl.pallas_call(
        _dispatch_kernel,
        grid=(n_steps,),
        in_specs=[pl.BlockSpec((2, 1, 1, tm), lambda i: (0, i, 0, 0), memory_space=pltpu.SMEM),
                  pl.BlockSpec((tm, d), lambda i: (i, 0))],
        out_specs=pl.BlockSpec(memory_space=pl.ANY),
        out_shape=jax.ShapeDtypeStruct((2 * n, d), h.dtype),
        scratch_shapes=[pltpu.SemaphoreType.DMA((1,))],
        compiler_params=pltpu.CompilerParams(
            dimension_semantics=("arbitrary",), vmem_limit_bytes=VMEM_LIMIT),
        name="moe_dispatch",
    )(dest.reshape(2, n_steps, 1, tm), h)


def _expert_kernel(tile_ref, exp_ref, lo_ref, hi_ref, tot_ref, xs_ref, wg_ref, wu_ref, wd_ref,
                   out_ref, wgb_ref, wub_ref, wdb_ref):
    w = pl.program_id(0)
    prev = jnp.maximum(w - 1, 0)
    live = w < tot_ref[0]
    new_expert = (w == 0) | (exp_ref[w] != exp_ref[prev])
    first_of_tile = (w == 0) | (tile_ref[w] != tile_ref[prev])

    @pl.when(live & new_expert)
    def _():
        wgb_ref[...] = wg_ref[0].astype(BF16)
        wub_ref[...] = wu_ref[0].astype(BF16)
        wdb_ref[...] = wd_ref[0].astype(BF16)

    @pl.when(live)
    def _():
        x_lo, x_hi = _unpack_pair(xs_ref[...])
        x = jnp.concatenate([x_lo.astype(BF16), x_hi.astype(BF16)], axis=1)
        hg = _dot(x, wgb_ref[...])
        hu = _dot(x, wub_ref[...])
        row = lax.broadcasted_iota(jnp.int32, (x.shape[0], 1), 0)
        mine = (row >= lo_ref[w]) & (row < hi_ref[w])
        act = jnp.where(mine, _silu(hg) * hu, 0.0).astype(BF16)
        part = _dot(act, wdb_ref[...])
        half_d = part.shape[1] // 2
        packed = _pack_pair(part[:, :half_d], part[:, half_d:])

        @pl.when(first_of_tile)
        def _():
            out_ref[...] = packed

        @pl.when(jnp.logical_not(first_of_tile))
        def _():
            out_ref[...] = jnp.where(mine, packed, out_ref[...])


def _experts(xs, items, p, te):
    rows, dp = xs.shape
    ne, d, f = p["w_gate"].shape
    assert d == 2 * dp
    item_tile, item_e, lo, hi, total = items
    n_items = item_tile.shape[0]
    tile_map = lambda w, t_, e_, lo_, hi_, n_: (t_[w], 0)
    exp_map = lambda w, t_, e_, lo_, hi_, n_: (e_[w], 0, 0)
    grid_spec = pltpu.PrefetchScalarGridSpec(
        num_scalar_prefetch=5,
        grid=(n_items,),
        in_specs=[
            pl.BlockSpec((te, dp), tile_map),
            pl.BlockSpec((1, d, f), exp_map),
            pl.BlockSpec((1, d, f), exp_map),
            pl.BlockSpec((1, f, d), exp_map),
        ],
        out_specs=pl.BlockSpec((te, dp), tile_map),
        scratch_shapes=[pltpu.VMEM((d, f), BF16), pltpu.VMEM((d, f), BF16), pltpu.VMEM((f, d), BF16)],
    )
    return pl.pallas_call(
        _expert_kernel,
        grid_spec=grid_spec,
        out_shape=jax.ShapeDtypeStruct((rows, dp), jnp.uint32),
        compiler_params=pltpu.CompilerParams(
            dimension_semantics=("arbitrary",), vmem_limit_bytes=VMEM_LIMIT),
        name="moe_experts",
    )(item_tile, item_e, lo, hi, total, xs, p["w_gate"], p["w_up"], p["w_down"])


def _combine_kernel(alpha, d_ref, dn_ref, h_ref, route_ref, rows_hbm, g_ref, b_ref, y_ref, o_ref, sem):
    i = pl.program_id(0)
    n_i = pl.num_programs(0)
    tm = h_ref.shape[0]
    slot = i % 2

    def row_copy(src_row, slot_, k, r):
        return pltpu.make_async_copy(rows_hbm.at[pl.ds(src_row, 1)], o_ref.at[slot_, k, pl.ds(r, 1)],
                                     sem.at[slot_])

    def start(dref, slot_):
        for r in range(tm):
            row_copy(dref[0, 0, 0, r], slot_, 0, r).start(priority=0)
            row_copy(dref[1, 0, 0, r], slot_, 1, r).start(priority=1)

    def wait(slot_):
        def body(r, carry):
            row_copy(0, slot_, 0, r).wait()
            row_copy(0, slot_, 1, r).wait()
            return carry
        lax.fori_loop(0, tm, body, 0, unroll=8)

    @pl.when(i == 0)
    def _():
        start(d_ref, 0)

    for nxt in (0, 1):
        @pl.when((i + 1 < n_i) & (slot == 1 - nxt))
        def _():
            start(dn_ref, nxt)

    wait(slot)
    route = route_ref[...]
    col = lax.broadcasted_iota(jnp.int32, route.shape, 1)
    w1 = jnp.sum(jnp.where(col == _R_W1, route, 0.0), axis=-1, keepdims=True)
    w2 = jnp.sum(jnp.where(col == _R_W2, route, 0.0), axis=-1, keepdims=True)
    a_lo, a_hi = _unpack_pair(o_ref[slot, 0])
    b_lo, b_hi = _unpack_pair(o_ref[slot, 1])
    moe = jnp.concatenate([w1 * a_lo + w2 * b_lo, w1 * a_hi + w2 * b_hi], axis=1)
    y_ref[...] = _layer_norm(alpha * h_ref[...] + moe, g_ref[...], b_ref[...])


def _combine(h, route, rows, dest, p, alpha, tm):
    n, d = h.shape
    assert n % tm == 0
    n_i = n // tm
    dest4 = dest.reshape(2, n_i, 1, tm)
    cur = pl.BlockSpec((2, 1, 1, tm), lambda i: (0, i, 0, 0), memory_space=pltpu.SMEM)
    nxt = pl.BlockSpec((2, 1, 1, tm), lambda i: (0, jnp.minimum(i + 1, n_i - 1), 0, 0),
                       memory_space=pltpu.SMEM)
    return pl.pallas_call(
        functools.partial(_combine_kernel, alpha),
        grid=(n_i,),
        in_specs=[cur, nxt,
                  pl.BlockSpec((tm, d), lambda i: (i, 0)),
                  pl.BlockSpec((tm, LANES), lambda i: (i, 0)),
                  pl.BlockSpec(memory_space=pl.ANY),
                  pl.BlockSpec((1, d), lambda i: (0, 0)),
                  pl.BlockSpec((1, d), lambda i: (0, 0))],
        out_specs=pl.BlockSpec((tm, d), lambda i: (i, 0)),
        out_shape=jax.ShapeDtypeStruct((n, d), F32),
        scratch_shapes=[pltpu.VMEM((2, 2, tm, d // 2), jnp.uint32), pltpu.SemaphoreType.DMA((2,))],
        compiler_params=pltpu.CompilerParams(
            dimension_semantics=("arbitrary",), vmem_limit_bytes=VMEM_LIMIT),
        name="moe_combine",
    )(dest4, dest4, h, route, rows, p["ln2_g"], p["ln2_b"])


def _pack_layer(w_in, b_in, w_dw, b_dw, lnc_g, lnc_b, w_conv_out, w_short, a_log, dt_bias, o_norm_g,
                w_o, w_out, ln1_g, ln1_b, w_rg, b_rg, w_re, b_re, w_gate, w_up, w_down, ln2_g, ln2_b):
    d = w_in.shape[0]
    c_conv = w_dw.shape[1]
    qkv_dim = w_short.shape[1]
    o_qkv = 2 * c_conv
    o_ba = o_qkv + qkv_dim
    o_tail = o_ba + 2 * N_HEADS

    def pad_cols(a, width):
        return jnp.pad(a, ((0, 0), (0, width - a.shape[1])))

    b2 = b_in[None, :].astype(F32)
    nh = N_HEADS
    zeros_h = jnp.zeros((nh,), F32)
    head_params = jnp.stack([jnp.concatenate([zeros_h, a_log.astype(F32)]),
                             jnp.concatenate([zeros_h, dt_bias.astype(F32)])])
    half = _TN // 2
    w_main = jnp.concatenate([w_in[:, :o_ba], w_in[:, o_tail:]], axis=1).astype(BF16)
    w_blk = w_main.reshape(d, -1, half).transpose(1, 0, 2)
    w_ba = w_in[:, o_ba:o_tail].astype(BF16)
    w_router = jnp.concatenate([w_rg, w_re], axis=1).astype(F32)
    b_router = jnp.concatenate([b_rg, b_re])[None, :].astype(F32)
    return dict(
        c_conv=c_conv, qkv_dim=qkv_dim,
        w_blk=w_blk, b_in=b2,
        w_ba=pad_cols(w_ba, LANES),
        b_ba=pad_cols(b2[:, o_ba:o_tail], LANES),
        w_bat=w_ba.T,
        b_bat=jnp.broadcast_to(b_in[o_ba:o_tail, None].astype(F32), (2 * nh, LANES)),
        b_tail=b2[:, o_tail:],
        p_row=pad_cols(head_params, LANES),
        p_col=pad_cols(head_params.T, LANES),
        w_dw=w_dw.astype(F32), b_dw=b_dw[None, :].astype(F32),
        lnc_g=lnc_g[None, :].astype(F32), lnc_b=lnc_b[None, :].astype(F32),
        w_conv_out=w_conv_out.astype(BF16),
        w_short=w_short.astype(F32),
        o_norm_g=o_norm_g[None, :].astype(F32),
        w_o=w_o.astype(BF16), w_out=w_out.astype(BF16),
        ln1_g=ln1_g[None, :].astype(F32), ln1_b=ln1_b[None, :].astype(F32),
        w_router=pad_cols(w_router, LANES), b_router=pad_cols(b_router, LANES),
        w_gate=w_gate.astype(F32), w_up=w_up.astype(F32), w_down=w_down.astype(F32),
        ln2_g=ln2_g[None, :].astype(F32), ln2_b=ln2_b[None, :].astype(F32),
    )


_MERGE_TM = 512
_COMBINE_TM = 256
_DEST_TN = 2048
_EXPERT_TE = 256
_EXPERT_TE_LARGE = 512
_DISPATCH_TM = 512


def _block(x, conv_buf, short_buf, s0, p, alpha, tm, conv_tt):
    b, t, d = x.shape
    n = b * t
    x2 = x.reshape(n, d).astype(F32)
    u, qkv, bg, bgt, gsil, sgc, sgd = _inproj(x2, p, tm)
    c_conv = u.shape[1]
    u3 = u.reshape(b, t, c_conv)
    qkv3 = qkv.reshape(b, t, -1)
    convg = _conv_branch(u3, conv_buf, sgc.reshape(b, t, d), p, conv_tt)
    og, s_new = _delta_branch(qkv3, short_buf, bg, bgt, gsil.reshape(b, t, -1), s0, p)
    h, hp, route, route_t, cnt = _merge(convg.reshape(n, d), og.reshape(n, -1), sgd, x2, p, alpha, min(_MERGE_TM, n))
    te = _EXPERT_TE_LARGE if 2 * n >= 2 * _EXPERT_TE_LARGE * N_EXPERTS else _EXPERT_TE
    dest, items = _route_plan(route_t, cnt, te)
    xs = _dispatch(hp, dest, min(_DISPATCH_TM, n))
    rows = _experts(xs, items, p, te)
    y = _combine(h, route, rows, dest, p, alpha, min(_COMBINE_TM, n))
    kc = conv_buf.shape[1]
    ks = short_buf.shape[1]
    assert t >= kc and t >= ks
    return (y.reshape(b, t, d).astype(x.dtype), u3[:, t - kc:].astype(x.dtype),
            qkv3[:, t - ks:].astype(x.dtype), s_new.astype(s0.dtype))


def kernel(x_prompt, x_sample, cache_conv, cache_short, state_delta, w_in, b_in, w_dw, b_dw, lnc_g, lnc_b, w_conv_out, w_short, a_log, dt_bias, o_norm_g, w_o, w_out, ln1_g, ln1_b, w_rg, b_rg, w_re, b_re, w_gate, w_up, w_down, ln2_g, ln2_b):
    weights = (w_in, b_in, w_dw, b_dw, lnc_g, lnc_b, w_conv_out, w_short, a_log, dt_bias, o_norm_g,
               w_o, w_out, ln1_g, ln1_b, w_rg, b_rg, w_re, b_re, w_gate, w_up, w_down, ln2_g, ln2_b)
    depth = w_in.shape[0]
    alpha = (2.0 * depth) ** 0.25
    yp, ys = x_prompt, x_sample
    bp = x_prompt.shape[0]
    outs = [[] for _ in range(6)]
    for l in range(depth):
        p = _pack_layer(*(wt[l] for wt in weights))
        zc = jnp.zeros((bp,) + cache_conv.shape[2:], x_prompt.dtype)
        zs = jnp.zeros((bp,) + cache_short.shape[2:], x_prompt.dtype)
        zd = jnp.zeros((bp,) + state_delta.shape[2:], state_delta.dtype)
        yp, c, s, dl = _block(yp, zc, zs, zd, p, alpha, 1024, 256)
        outs[0].append(c), outs[1].append(s), outs[2].append(dl)
        ys, c, s, dl = _block(ys, cache_conv[l], cache_short[l], state_delta[l], p, alpha, 1024, 64)
        outs[3].append(c), outs[4].append(s), outs[5].append(dl)
    return (yp, ys) + tuple(jnp.stack(o) for o in outs)
```
